```python
import math
import jax, jax.numpy as jnp
from jax import lax
import numpy as np

D_MODEL = 1024
BATCH = 2
SEQ = 8192
DEPTH = 2

MEM_LEN = 256
HEAD_DIM = 64
N_MIX_HEADS = D_MODEL // HEAD_DIM
N_MEM_HEADS = 4
N_TOK_HEADS = N_MIX_HEADS - N_MEM_HEADS
TOK_WIDTH = N_TOK_HEADS * HEAD_DIM
MEM_WIDTH = N_MEM_HEADS * HEAD_DIM
MIX_WIDTH = TOK_WIDTH + MEM_WIDTH
Q_LORA = 384
KV_LORA = 256
QK_NOPE = 64
QK_ROPE = 32
V_DIM = HEAD_DIM
QK_DIM = QK_NOPE + QK_ROPE
ROPE_THETA = 10000.0
Q_BLOCK = 128
CONV_W = 4
LRU_C = 8.0
N_LRU_BLOCKS = N_TOK_HEADS
LRU_BLOCK = TOK_WIDTH // N_LRU_BLOCKS
ALPHA = (2.0 * DEPTH) ** 0.25
BETA = (8.0 * DEPTH) ** -0.25
NORM_EPS = 1e-6
N_MLA = (DEPTH + 1) // 2
N_LRU = DEPTH // 2
MLA_IN = Q_LORA + KV_LORA + QK_ROPE + MIX_WIDTH + MEM_WIDTH
LRU_IN = TOK_WIDTH + MIX_WIDTH + MEM_WIDTH

kernel_name = "hybrid_mla_rglru_memory_deepnorm"


def _split(t, sizes):
    idx = np.cumsum(sizes)[:-1].tolist()
    return jnp.split(t, idx, axis=-1)


def rms_norm(t, g):
    t32 = t.astype(jnp.float32)
    t32 = t32 * lax.rsqrt(jnp.mean(t32 * t32, axis=-1, keepdims=True) + NORM_EPS)
    return (t32 * g.astype(jnp.float32)).astype(t.dtype)


def layer_norm(t, g, b):
    t32 = t.astype(jnp.float32)
    mu = jnp.mean(t32, axis=-1, keepdims=True)
    var = jnp.mean(jnp.square(t32 - mu), axis=-1, keepdims=True)
    y = (t32 - mu) * lax.rsqrt(var + NORM_EPS)
    return (y * g.astype(jnp.float32) + b.astype(jnp.float32)).astype(t.dtype)


def apply_rope(t, positions):
    half = t.shape[-1] // 2
    inv_freq = ROPE_THETA ** (-jnp.arange(half, dtype=jnp.float32) / half)
    ang = positions.astype(jnp.float32)[..., None] * inv_freq
    cos = jnp.cos(ang)[:, :, None, :].astype(t.dtype)
    sin = jnp.sin(ang)[:, :, None, :].astype(t.dtype)
    t1, t2 = t[..., :half], t[..., half:]
    return jnp.concatenate([t1 * cos - t2 * sin, t1 * sin + t2 * cos], axis=-1)


def causal_attention(q, k, v):
    b, s, h, d = q.shape
    nb = s // Q_BLOCK
    scale = 1.0 / math.sqrt(d)
    qb = q.reshape(b, nb, Q_BLOCK, h, d).transpose(1, 0, 2, 3, 4)
    k_pos = jnp.arange(s)

    def one_block(args):
        q_blk, blk = args
        sc = jnp.einsum('bqhd,bkhd->bhqk', q_blk, k,
                        preferred_element_type=jnp.float32) * scale
        q_pos = blk * Q_BLOCK + jnp.arange(Q_BLOCK)
        mask = k_pos[None, :] <= q_pos[:, None]
        sc = jnp.where(mask[None, None], sc, -jnp.inf)
        p = jax.nn.softmax(sc, axis=-1).astype(v.dtype)
        return jnp.einsum('bhqk,bkhd->bqhd', p, v)

    out = lax.map(one_block, (qb, jnp.arange(nb)))
    return out.transpose(1, 0, 2, 3, 4).reshape(b, s, h, v.shape[-1])


def memory_attention(q, mem_k, mem_v):
    sc = jnp.einsum('bshd,bmhd->bhsm', q, mem_k,
                    preferred_element_type=jnp.float32) / math.sqrt(HEAD_DIM)
    p = jax.nn.softmax(sc, axis=-1).astype(mem_v.dtype)
    return jnp.einsum('bhsm,bmhd->bshd', p, mem_v)


def _lin_rec_combine(left, right):
    a1, b1 = left
    a2, b2 = right
    return a1 * a2, a2 * b1 + b2


def rg_lru_branch(u, conv_w, conv_b, w_r, b_r, w_i, b_i, lam):
    b, s, w = u.shape
    u_pad = jnp.pad(u, ((0, 0), (CONV_W - 1, 0), (0, 0)))
    xc = conv_b + u_pad[:, 0:s] * conv_w[0]
    for tap in range(1, CONV_W):
        xc = xc + u_pad[:, tap:tap + s] * conv_w[tap]
    xb = xc.reshape(b, s, N_LRU_BLOCKS, LRU_BLOCK)
    r = jax.nn.sigmoid(jnp.einsum('bsgi,gij->bsgj', xb, w_r).reshape(b, s, w) + b_r)
    i = jax.nn.sigmoid(jnp.einsum('bsgi,gij->bsgj', xb, w_i).reshape(b, s, w) + b_i)
    log_a = (-LRU_C * jax.nn.softplus(-lam.astype(jnp.float32))) * r.astype(jnp.float32)
    a = jnp.exp(log_a)
    gated_x = jnp.sqrt(-jnp.expm1(2.0 * log_a)) * (i * xc).astype(jnp.float32)
    _, hs = lax.associative_scan(_lin_rec_combine, (a, gated_x), axis=1)
    return hs.astype(u.dtype)


def setup_inputs(seed: int = 0) -> dict:
    key = jax.random.key(seed)
    ks = jax.random.split(key, 24)
    f32 = jnp.float32
    nrm = lambda k, shape, s: jax.random.normal(k, shape, f32) * s
    x = nrm(ks[0], (BATCH, SEQ, D_MODEL), 1.0)
    mem = nrm(ks[1], (BATCH, MEM_LEN, D_MODEL), 1.0)
    offset = jax.random.randint(ks[2], (BATCH, 1), 0, 4096, dtype=jnp.int32)
    positions = (offset + jnp.arange(SEQ, dtype=jnp.int32)[None, :]).astype(jnp.int32)
    mla_w_in = nrm(ks[3], (N_MLA, D_MODEL, MLA_IN), D_MODEL ** -0.5)
    mla_q_norm = 1.0 + nrm(ks[4], (N_MLA, Q_LORA), 0.01)
    mla_w_uq = nrm(ks[5], (N_MLA, Q_LORA, N_TOK_HEADS * QK_DIM), Q_LORA ** -0.5)
    mla_kv_norm = 1.0 + nrm(ks[6], (N_MLA, KV_LORA), 0.01)
    mla_w_ukv = nrm(ks[7], (N_MLA, KV_LORA, N_TOK_HEADS * (QK_NOPE + V_DIM)), KV_LORA ** -0.5)
    lru_w_in = nrm(ks[8], (N_LRU, D_MODEL, LRU_IN), D_MODEL ** -0.5)
    lru_conv_w = nrm(ks[9], (N_LRU, CONV_W, TOK_WIDTH), CONV_W ** -0.5)
    lru_conv_b = nrm(ks[10], (N_LRU, TOK_WIDTH), 0.01)
    lru_w_rgate = nrm(ks[11], (N_LRU, N_LRU_BLOCKS, LRU_BLOCK, LRU_BLOCK), LRU_BLOCK ** -0.5)
    lru_b_rgate = nrm(ks[12], (N_LRU, TOK_WIDTH), 0.01)
    lru_w_igate = nrm(ks[13], (N_LRU, N_LRU_BLOCKS, LRU_BLOCK, LRU_BLOCK), LRU_BLOCK ** -0.5)
    lru_b_igate = nrm(ks[14], (N_LRU, TOK_WIDTH), 0.01)
    a_c = jax.random.uniform(ks[15], (N_LRU, TOK_WIDTH), f32, 0.9, 0.999)
    a0 = a_c ** (1.0 / LRU_C)
    lru_lambda = jnp.log(a0) - jnp.log1p(-a0)
    w_mem_kv = nrm(ks[16], (DEPTH, D_MODEL, 2 * MEM_WIDTH), D_MODEL ** -0.5)
    w_out = nrm(ks[17], (DEPTH, MIX_WIDTH, D_MODEL), BETA * MIX_WIDTH ** -0.5)
    ln_g = 1.0 + nrm(ks[18], (DEPTH, D_MODEL), 0.01)
    ln_b = nrm(ks[19], (DEPTH, D_MODEL), 0.01)
    return {"x": x, "mem": mem, "positions": positions,
            "mla_w_in": mla_w_in, "mla_q_norm": mla_q_norm, "mla_w_uq": mla_w_uq,
            "mla_kv_norm": mla_kv_norm, "mla_w_ukv": mla_w_ukv,
            "lru_w_in": lru_w_in, "lru_conv_w": lru_conv_w, "lru_conv_b": lru_conv_b,
            "lru_w_rgate": lru_w_rgate, "lru_b_rgate": lru_b_rgate,
            "lru_w_igate": lru_w_igate, "lru_b_igate": lru_b_igate, "lru_lambda": lru_lambda,
            "w_mem_kv": w_mem_kv, "w_out": w_out, "ln_g": ln_g, "ln_b": ln_b}


def reference(x, mem, positions, mla_w_in, mla_q_norm, mla_w_uq, mla_kv_norm, mla_w_ukv,
              lru_w_in, lru_conv_w, lru_conv_b, lru_w_rgate, lru_b_rgate,
              lru_w_igate, lru_b_igate, lru_lambda, w_mem_kv, w_out, ln_g, ln_b):
    b, s, _ = x.shape
    h = x
    for layer in range(DEPTH):
        j = layer // 2
        if layer % 2 == 0:
            z = h @ mla_w_in[j]
            c_q, c_kv, k_r, gate, q_mem = _split(
                z, [Q_LORA, KV_LORA, QK_ROPE, MIX_WIDTH, MEM_WIDTH])
            q = (rms_norm(c_q, mla_q_norm[j]) @ mla_w_uq[j]).reshape(b, s, N_TOK_HEADS, QK_DIM)
            q = jnp.concatenate([q[..., :QK_NOPE], apply_rope(q[..., QK_NOPE:], positions)], axis=-1)
            kv = (rms_norm(c_kv, mla_kv_norm[j]) @ mla_w_ukv[j]).reshape(
                b, s, N_TOK_HEADS, QK_NOPE + V_DIM)
            k_nope, v = kv[..., :QK_NOPE], kv[..., QK_NOPE:]
            k_rope = apply_rope(k_r[:, :, None, :], positions)
            k = jnp.concatenate(
                [k_nope, jnp.broadcast_to(k_rope, (b, s, N_TOK_HEADS, QK_ROPE))], axis=-1)
            tok = causal_attention(q, k, v).reshape(b, s, TOK_WIDTH)
        else:
            z = h @ lru_w_in[j]
            u, gate, q_mem = _split(z, [TOK_WIDTH, MIX_WIDTH, MEM_WIDTH])
            tok = rg_lru_branch(u, lru_conv_w[j], lru_conv_b[j], lru_w_rgate[j], lru_b_rgate[j],
                                lru_w_igate[j], lru_b_igate[j], lru_lambda[j])
        mem_kv = (mem @ w_mem_kv[layer]).reshape(b, MEM_LEN, 2, N_MEM_HEADS, HEAD_DIM)
        mem_out = memory_attention(q_mem.reshape(b, s, N_MEM_HEADS, HEAD_DIM),
                                   mem_kv[:, :, 0], mem_kv[:, :, 1]).reshape(b, s, MEM_WIDTH)
        y = jnp.concatenate([tok, mem_out], axis=-1) * jax.nn.silu(gate)
        h = layer_norm(ALPHA * h + y @ w_out[layer], ln_g[layer], ln_b[layer])
    return h
```

```python
import functools
import math

import jax
import jax.numpy as jnp
import numpy as np
from jax import lax
from jax.experimental import pallas as pl
from jax.experimental.pallas import tpu as pltpu

F32 = jnp.float32
BF16 = jnp.bfloat16

D_MODEL = 1024
DEPTH = 2
MEM_LEN = 256
HEAD_DIM = 64
N_MEM_HEADS = 4
N_TOK_HEADS = 12
TOK_WIDTH = N_TOK_HEADS * HEAD_DIM
MEM_WIDTH = N_MEM_HEADS * HEAD_DIM
MIX_WIDTH = TOK_WIDTH + MEM_WIDTH
Q_LORA = 384
KV_LORA = 256
QK_NOPE = 64
QK_ROPE = 32
QK_DIM = QK_NOPE + QK_ROPE
ROPE_THETA = 10000.0
CONV_W = 4
LRU_C = 8.0
ALPHA = (2.0 * DEPTH) ** 0.25
NORM_EPS = 1e-6

LANES = 128
SUBLANES = 8
HALF_ROPE = QK_ROPE // 2
LOG2E = math.log2(math.e)
Q_SCALE = LOG2E / math.sqrt(QK_DIM)

TM_PROJ = 512
TQ_ATTN = 512
TM_LRU = 256
VMEM_LIMIT = 48 * 1024 * 1024

_NT = (((1,), (1,)), ((), ()))


def _const_spec(shape):
    nd = len(shape)
    return pl.BlockSpec(shape, lambda *_: (0,) * nd, pipeline_mode=pl.Buffered(1))


def _silu(t):
    return t * (1.0 / (1.0 + jnp.exp(-t)))


def _sigmoid(t):
    return 1.0 / (1.0 + jnp.exp(-t))


def _rms_norm(t, g):
    return t * lax.rsqrt(jnp.mean(t * t, axis=-1, keepdims=True) + NORM_EPS) * g


def _layer_norm(t, g, b):
    mu = jnp.mean(t, axis=-1, keepdims=True)
    c = t - mu
    var = jnp.mean(c * c, axis=-1, keepdims=True)
    return c * lax.rsqrt(var + NORM_EPS) * g + b


def _memory_attention(qm, kbd_ref, vbd_ref):
    s = jnp.dot(qm.astype(BF16), kbd_ref[...], preferred_element_type=F32)
    ps = []
    for h in range(N_MEM_HEADS):
        sh = s[:, h * MEM_LEN:(h + 1) * MEM_LEN]
        e = jnp.exp(sh - jnp.max(sh, axis=-1, keepdims=True))
        ps.append((e * (1.0 / jnp.sum(e, axis=-1, keepdims=True))).astype(BF16))
    p = jnp.concatenate(ps, axis=-1)
    return jnp.dot(p, vbd_ref[...], preferred_element_type=F32)


def _mem_kv_kernel(mem_ref, wkT_ref, wv_ref, kbd_ref, vbd_ref):
    mb = mem_ref[0].astype(BF16)
    kT = lax.dot_general(wkT_ref[0].astype(BF16), mb, _NT, preferred_element_type=F32)
    kT = (kT * (1.0 / math.sqrt(HEAD_DIM))).astype(BF16)
    v = jnp.dot(mb, wv_ref[0].astype(BF16), preferred_element_type=F32).astype(BF16)
    row = lax.broadcasted_iota(jnp.int32, (MEM_WIDTH, MEM_LEN), 0)
    col = lax.broadcasted_iota(jnp.int32, (MEM_LEN, MEM_WIDTH), 1)
    zero = jnp.zeros((), BF16)
    for h in range(N_MEM_HEADS):
        lo, hi = h * HEAD_DIM, (h + 1) * HEAD_DIM
        kbd_ref[0, 0, :, h * MEM_LEN:(h + 1) * MEM_LEN] = jnp.where((row >= lo) & (row < hi), kT, zero)
        vbd_ref[0, 0, h * MEM_LEN:(h + 1) * MEM_LEN, :] = jnp.where((col >= lo) & (col < hi), v, zero)


def _mem_kv(mem, w_mem_kv):
    b = mem.shape[0]
    wkT = jnp.swapaxes(w_mem_kv[:, :, :MEM_WIDTH], 1, 2)
    wv = w_mem_kv[:, :, MEM_WIDTH:]
    return pl.pallas_call(
        _mem_kv_kernel,
        grid=(DEPTH, b),
        in_specs=[
            pl.BlockSpec((1, MEM_LEN, D_MODEL), lambda l, i: (i, 0, 0)),
            pl.BlockSpec((1, MEM_WIDTH, D_MODEL), lambda l, i: (l, 0, 0)),
            pl.BlockSpec((1, D_MODEL, MEM_WIDTH), lambda l, i: (l, 0, 0)),
        ],
        out_specs=[
            pl.BlockSpec((1, 1, MEM_WIDTH, N_MEM_HEADS * MEM_LEN), lambda l, i: (l, i, 0, 0)),
            pl.BlockSpec((1, 1, N_MEM_HEADS * MEM_LEN, MEM_WIDTH), lambda l, i: (l, i, 0, 0)),
        ],
        out_shape=[
            jax.ShapeDtypeStruct((DEPTH, b, MEM_WIDTH, N_MEM_HEADS * MEM_LEN), BF16),
            jax.ShapeDtypeStruct((DEPTH, b, N_MEM_HEADS * MEM_LEN, MEM_WIDTH), BF16),
        ],
        name="mem_kv",
    )(mem, wkT, wv)


_A_CQ = (0, Q_LORA)
_A_CKV = (_A_CQ[1], _A_CQ[1] + KV_LORA)
_A_KR = (_A_CKV[1], _A_CKV[1] + LANES)
_A_GATE = (_A_KR[1], _A_KR[1] + MIX_WIDTH)
_A_QM = (_A_GATE[1], _A_GATE[1] + MEM_WIDTH)
_A_WIDTH = _A_QM[1]


def _mla_proj_kernel(x_ref, pos_ref, w_in_ref, qn_ref, kvn_ref, w_uq_ref, w_k_ref, w_vT_ref, rope_ref,
                     kbd_ref, vbd_ref, q_ref, k_ref, vT_ref, g_ref, mg_ref):
    xb = x_ref[0].astype(BF16)

    def proj(seg):
        return jnp.dot(xb, w_in_ref[:, seg[0]:seg[1]], preferred_element_type=F32)

    ang = pos_ref[0] * rope_ref[0:1, :]
    cos = jnp.cos(ang)
    sin = jnp.sin(ang)
    sin_lo = sin * rope_ref[1:2, :]
    sin_hi = sin * rope_ref[2:3, :]

    def rope(t):
        return (t * cos + pltpu.roll(t, LANES - HALF_ROPE, 1) * sin_lo
                + pltpu.roll(t, HALF_ROPE, 1) * sin_hi)

    cqn = _rms_norm(proj(_A_CQ), qn_ref[...]).astype(BF16)
    q = jnp.dot(cqn, w_uq_ref[...], preferred_element_type=F32)
    for h in range(N_TOK_HEADS):
        q_ref[0, h] = (rope(q[:, h * LANES:(h + 1) * LANES]) * Q_SCALE).astype(BF16)

    ckvn = _rms_norm(proj(_A_CKV), kvn_ref[...]).astype(BF16)
    k_rope = rope(proj(_A_KR))
    k = jnp.dot(ckvn, w_k_ref[...], preferred_element_type=F32)
    for h in range(N_TOK_HEADS):
        k_ref[0, h] = (k[:, h * LANES:(h + 1) * LANES] + k_rope).astype(BF16)

    vT = lax.dot_general(w_vT_ref[...], ckvn, _NT, preferred_element_type=F32)
    for h in range(N_TOK_HEADS):
        vT_ref[0, h, 0] = vT[h * HEAD_DIM:(h + 1) * HEAD_DIM, :].astype(BF16)

    g = _silu(proj(_A_GATE))
    g_ref[0] = g[:, :TOK_WIDTH]
    mg_ref[0] = _memory_attention(proj(_A_QM), kbd_ref.at[0, 0], vbd_ref.at[0, 0]) * g[:, TOK_WIDTH:]


def _mla_proj(x, pos, w_in_a, qn, kvn, w_uq, w_k, w_vT, rope_c, kbd, vbd):
    b, s, _ = x.shape
    tm = TM_PROJ
    nt = s // tm
    return pl.pallas_call(
        _mla_proj_kernel,
        grid=(b, nt),
        in_specs=[
            pl.BlockSpec((1, tm, D_MODEL), lambda i, j: (i, j, 0)),
            pl.BlockSpec((1, tm, 1), lambda i, j: (i, j, 0)),
            _const_spec((D_MODEL, _A_WIDTH)),
            _const_spec((1, Q_LORA)),
            _const_spec((1, KV_LORA)),
            _const_spec((Q_LORA, N_TOK_HEADS * LANES)),
            _const_spec((KV_LORA, N_TOK_HEADS * LANES)),
            _const_spec((TOK_WIDTH, KV_LORA)),
            _const_spec((SUBLANES, LANES)),
            pl.BlockSpec((1, 1, MEM_WIDTH, N_MEM_HEADS * MEM_LEN), lambda i, j: (0, i, 0, 0)),
            pl.BlockSpec((1, 1, N_MEM_HEADS * MEM_LEN, MEM_WIDTH), lambda i, j: (0, i, 0, 0)),
        ],
        out_specs=[
            pl.BlockSpec((1, N_TOK_HEADS, tm, LANES), lambda i, j: (i, 0, j, 0)),
            pl.BlockSpec((1, N_TOK_HEADS, tm, LANES), lambda i, j: (i, 0, j, 0)),
            pl.BlockSpec((1, N_TOK_HEADS, 1, HEAD_DIM, tm), lambda i, j: (i, 0, j, 0, 0)),
            pl.BlockSpec((1, tm, TOK_WIDTH), lambda i, j: (i, j, 0)),
            pl.BlockSpec((1, tm, MEM_WIDTH), lambda i, j: (i, j, 0)),
        ],
        out_shape=[
            jax.ShapeDtypeStruct((b, N_TOK_HEADS, s, LANES), BF16),
            jax.ShapeDtypeStruct((b, N_TOK_HEADS, s, LANES), BF16),
            jax.ShapeDtypeStruct((b, N_TOK_HEADS, nt, HEAD_DIM, tm), BF16),
            jax.ShapeDtypeStruct((b, s, TOK_WIDTH), F32),
            jax.ShapeDtypeStruct((b, s, MEM_WIDTH), F32),
        ],
        compiler_params=pltpu.CompilerParams(
            dimension_semantics=("arbitrary", "arbitrary"), vmem_limit_bytes=VMEM_LIMIT),
        name="mla_proj",
    )(x, pos, w_in_a, qn, kvn, w_uq, w_k, w_vT, rope_c, kbd, vbd)


def _attn_kernel(q_ref, k_ref, vT_ref, o_ref):
    tq = TQ_ATTN
    qi = pl.program_id(2)
    q = q_ref[0, 0]

    def block(j, carry, masked):
        m, l, acc = carry
        start = pl.multiple_of(j * tq, tq)
        kt = k_ref[0, 0, pl.ds(start, tq), :]
        st = lax.dot_general(kt, q, _NT, preferred_element_type=F32)
        if masked:
            key = lax.broadcasted_iota(jnp.int32, (tq, tq), 0)
            qry = lax.broadcasted_iota(jnp.int32, (tq, tq), 1)
            st = jnp.where(key <= qry, st, -jnp.inf)
        m_new = jnp.maximum(m, jnp.max(st, axis=0, keepdims=True))
        alpha = jnp.exp2(m - m_new)
        p = jnp.exp2(st - m_new)
        l = alpha * l + jnp.sum(p, axis=0, keepdims=True)
        pv = jnp.dot(vT_ref[0, 0, j], p.astype(BF16), preferred_element_type=F32)
        return m_new, l, alpha * acc + pv

    init = (jnp.full((1, tq), -jnp.inf, F32), jnp.zeros((1, tq), F32), jnp.zeros((HEAD_DIM, tq), F32))
    carry = lax.fori_loop(0, qi, lambda j, c: block(j, c, False), init)
    _, l, acc = block(qi, carry, True)
    o_ref[0] = acc * (1.0 / l)


def _attention(q, k, vT):
    b, h, s, _ = q.shape
    tq = TQ_ATTN
    nkv = vT.shape[2]
    return pl.pallas_call(
        _attn_kernel,
        grid=(b, h, s // tq),
        in_specs=[
            pl.BlockSpec((1, 1, tq, LANES), lambda i, j, t: (i, j, t, 0)),
            pl.BlockSpec((1, 1, s, LANES), lambda i, j, t: (i, j, 0, 0)),
            pl.BlockSpec((1, 1, nkv, HEAD_DIM, tq), lambda i, j, t: (i, j, 0, 0, 0)),
        ],
        out_specs=pl.BlockSpec((1, HEAD_DIM, tq), lambda i, j, t: (i, j, t)),
        out_shape=jax.ShapeDtypeStruct((b, h * HEAD_DIM, s), F32),
        compiler_params=pltpu.CompilerParams(
            dimension_semantics=("arbitrary", "arbitrary", "arbitrary"), vmem_limit_bytes=VMEM_LIMIT),
        name="mla_attn",
    )(q, k, vT)


def _out_ln_kernel(tokT_ref, g_ref, mg_ref, h_ref, w_out_ref, lng_ref, lnb_ref, o_ref):
    tok = tokT_ref[0].T
    y = (tok * g_ref[0]).astype(BF16)
    o = jnp.dot(y, w_out_ref[:TOK_WIDTH, :], preferred_element_type=F32)
    o = o + jnp.dot(mg_ref[0].astype(BF16), w_out_ref[TOK_WIDTH:, :], preferred_element_type=F32)
    o_ref[0] = _layer_norm(ALPHA * h_ref[0] + o, lng_ref[...], lnb_ref[...])


def _out_ln(tokT, g, mg, h, w_out, ln_g, ln_b):
    b, s, _ = h.shape
    tm = TM_PROJ
    return pl.pallas_call(
        _out_ln_kernel,
        grid=(b, s // tm),
        in_specs=[
            pl.BlockSpec((1, TOK_WIDTH, tm), lambda i, j: (i, 0, j)),
            pl.BlockSpec((1, tm, TOK_WIDTH), lambda i, j: (i, j, 0)),
            pl.BlockSpec((1, tm, MEM_WIDTH), lambda i, j: (i, j, 0)),
            pl.BlockSpec((1, tm, D_MODEL), lambda i, j: (i, j, 0)),
            _const_spec((MIX_WIDTH, D_MODEL)),
            _const_spec((1, D_MODEL)),
            _const_spec((1, D_MODEL)),
        ],
        out_specs=pl.BlockSpec((1, tm, D_MODEL), lambda i, j: (i, j, 0)),
        out_shape=jax.ShapeDtypeStruct((b, s, D_MODEL), F32),
        compiler_params=pltpu.CompilerParams(
            dimension_semantics=("arbitrary", "arbitrary"), vmem_limit_bytes=VMEM_LIMIT),
        name="mla_out_ln",
    )(tokT, g, mg, h, w_out, ln_g, ln_b)


_B_U = (0, TOK_WIDTH)
_B_GATE = (_B_U[1], _B_U[1] + MIX_WIDTH)
_B_QM = (_B_GATE[1], _B_GATE[1] + MEM_WIDTH)
_B_WIDTH = _B_QM[1]
GATE_TILE = 256


def _linear_scan(a, b):
    n = a.shape[0]
    row = lax.broadcasted_iota(jnp.int32, a.shape, 0)
    d = 1
    while d < n:
        live = row >= d
        b = a * jnp.where(live, pltpu.roll(b, d, 0), 0.0) + b
        if 2 * d < n:
            a = a * jnp.where(live, pltpu.roll(a, d, 0), 1.0)
        d *= 2
    return b


def _lru_layer_kernel(h_ref, w_in_ref, conv_w_ref, conv_b_ref, w_r_ref, b_r_ref, w_i_ref, b_i_ref, lam_ref,
                      kbd_ref, vbd_ref, w_out_ref, lng_ref, lnb_ref, o_ref, ubuf_ref, carry_ref):
    tm = TM_LRU
    step = pl.program_id(1)

    @pl.when(step == 0)
    def _():
        ubuf_ref[0:SUBLANES, :] = jnp.zeros((SUBLANES, TOK_WIDTH), F32)
        carry_ref[...] = jnp.zeros((1, TOK_WIDTH), F32)

    hin = h_ref[0]
    hb = hin.astype(BF16)

    def proj(seg):
        return jnp.dot(hb, w_in_ref[:, seg[0]:seg[1]], preferred_element_type=F32)

    ubuf_ref[SUBLANES:SUBLANES + tm, :] = proj(_B_U)
    xc = conv_b_ref[...] + ubuf_ref[SUBLANES:SUBLANES + tm, :] * conv_w_ref[CONV_W - 1:CONV_W, :]
    for back in range(1, CONV_W):
        tap = CONV_W - 1 - back
        xc = xc + ubuf_ref[SUBLANES - back:SUBLANES - back + tm, :] * conv_w_ref[tap:tap + 1, :]
    ubuf_ref[0:SUBLANES, :] = ubuf_ref[tm:tm + SUBLANES, :]

    xcb = xc.astype(BF16)
    r_parts, i_parts = [], []
    for c in range(TOK_WIDTH // GATE_TILE):
        xs = xcb[:, c * GATE_TILE:(c + 1) * GATE_TILE]
        r_parts.append(jnp.dot(xs, w_r_ref[c], preferred_element_type=F32))
        i_parts.append(jnp.dot(xs, w_i_ref[c], preferred_element_type=F32))
    r = _sigmoid(jnp.concatenate(r_parts, axis=-1) + b_r_ref[...])
    i = _sigmoid(jnp.concatenate(i_parts, axis=-1) + b_i_ref[...])

    neg_lam = -lam_ref[...]
    softplus = jnp.maximum(neg_lam, 0.0) + jnp.log1p(jnp.exp(-jnp.abs(neg_lam)))
    log_a = (-LRU_C * softplus) * r
    a = jnp.exp(log_a)
    gated = jnp.sqrt(-jnp.tanh(log_a) * (a * a + 1.0)) * (i * xc)

    row = lax.broadcasted_iota(jnp.int32, (tm, TOK_WIDTH), 0)
    gated = jnp.where(row == 0, a * carry_ref[...] + gated, gated)
    hs = _linear_scan(a, gated)
    carry_ref[...] = hs[tm - 1:tm, :]

    g = _silu(proj(_B_GATE))
    mem_out = _memory_attention(proj(_B_QM), kbd_ref.at[0, 0], vbd_ref.at[0, 0])
    y_tok = (hs * g[:, :TOK_WIDTH]).astype(BF16)
    y_mem = (mem_out * g[:, TOK_WIDTH:]).astype(BF16)
    o = jnp.dot(y_tok, w_out_ref[:TOK_WIDTH, :], preferred_element_type=F32)
    o = o + jnp.dot(y_mem, w_out_ref[TOK_WIDTH:, :], preferred_element_type=F32)
    o_ref[0] = _layer_norm(ALPHA * hin + o, lng_ref[...], lnb_ref[...])


def _lru_layer(h, w_in, conv_w, conv_b, w_r, b_r, w_i, b_i, lam, kbd, vbd, w_out, ln_g, ln_b):
    b, s, _ = h.shape
    tm = TM_LRU
    n_gate_tiles = TOK_WIDTH // GATE_TILE
    return pl.pallas_call(
        _lru_layer_kernel,
        grid=(b, s // tm),
        in_specs=[
            pl.BlockSpec((1, tm, D_MODEL), lambda i, j: (i, j, 0)),
            _const_spec((D_MODEL, _B_WIDTH)),
            _const_spec((CONV_W, TOK_WIDTH)),
            _const_spec((1, TOK_WIDTH)),
            _const_spec((n_gate_tiles, GATE_TILE, GATE_TILE)),
            _const_spec((1, TOK_WIDTH)),
            _const_spec((n_gate_tiles, GATE_TILE, GATE_TILE)),
            _const_spec((1, TOK_WIDTH)),
            _const_spec((1, TOK_WIDTH)),
            pl.BlockSpec((1, 1, MEM_WIDTH, N_MEM_HEADS * MEM_LEN), lambda i, j: (1, i, 0, 0)),
            pl.BlockSpec((1, 1, N_MEM_HEADS * MEM_LEN, MEM_WIDTH), lambda i, j: (1, i, 0, 0)),
            _const_spec((MIX_WIDTH, D_MODEL)),
            _const_spec((1, D_MODEL)),
            _const_spec((1, D_MODEL)),
        ],
        out_specs=pl.BlockSpec((1, tm, D_MODEL), lambda i, j: (i, j, 0)),
        out_shape=jax.ShapeDtypeStruct((b, s, D_MODEL), F32),
        scratch_shapes=[
            pltpu.VMEM((tm + SUBLANES, TOK_WIDTH), F32),
            pltpu.VMEM((1, TOK_WIDTH), F32),
        ],
        compiler_params=pltpu.CompilerParams(
            dimension_semantics=("arbitrary", "arbitrary"), vmem_limit_bytes=VMEM_LIMIT),
        name="lru_layer",
    )(h, w_in, conv_w, conv_b, w_r, b_r, w_i, b_i, lam, kbd, vbd, w_out, ln_g, ln_b)


def _block_diag_tiles(w):
    per_tile = GATE_TILE // HEAD_DIM
    w = w.reshape(TOK_WIDTH // GATE_TILE, per_tile, HEAD_DIM, HEAD_DIM)
    eye = jnp.eye(per_tile, dtype=w.dtype)
    t = w[:, :, :, None, :] * eye[None, :, None, :, None]
    return t.reshape(TOK_WIDTH // GATE_TILE, GATE_TILE, GATE_TILE)


def _rope_constants():
    inv_freq = ROPE_THETA ** (-np.arange(HALF_ROPE, dtype=np.float32) / np.float32(HALF_ROPE))
    c = np.zeros((SUBLANES, LANES), np.float32)
    c[0, QK_NOPE:QK_NOPE + HALF_ROPE] = inv_freq
    c[0, QK_NOPE + HALF_ROPE:QK_NOPE + QK_ROPE] = inv_freq
    c[1, QK_NOPE:QK_NOPE + HALF_ROPE] = -1.0
    c[2, QK_NOPE + HALF_ROPE:QK_NOPE + QK_ROPE] = 1.0
    return c


def kernel(x, mem, positions, mla_w_in, mla_q_norm, mla_w_uq, mla_kv_norm, mla_w_ukv, lru_w_in, lru_conv_w,
           lru_conv_b, lru_w_rgate, lru_b_rgate, lru_w_igate, lru_b_igate, lru_lambda, w_mem_kv, w_out, ln_g, ln_b):
    b, s, _ = x.shape
    assert s % TM_PROJ == 0 and s % TM_LRU == 0 and TQ_ATTN == TM_PROJ

    kbd, vbd = _mem_kv(mem, w_mem_kv)

    w_in = mla_w_in[0]
    o_q, o_kv, o_kr, o_gate = Q_LORA, Q_LORA + KV_LORA, Q_LORA + KV_LORA + QK_ROPE, Q_LORA + KV_LORA + QK_ROPE + MIX_WIDTH
    w_kr = jnp.pad(w_in[:, o_kv:o_kr], ((0, 0), (QK_NOPE, LANES - QK_DIM)))
    w_in_a = jnp.concatenate(
        [w_in[:, :o_q], w_in[:, o_q:o_kv], w_kr, w_in[:, o_kr:o_gate], w_in[:, o_gate:]], axis=1).astype(BF16)
    w_uq = jnp.pad(mla_w_uq[0].reshape(Q_LORA, N_TOK_HEADS, QK_DIM), ((0, 0), (0, 0), (0, LANES - QK_DIM)))
    w_uq = w_uq.reshape(Q_LORA, N_TOK_HEADS * LANES).astype(BF16)
    w_ukv = mla_w_ukv[0].reshape(KV_LORA, N_TOK_HEADS, QK_NOPE + HEAD_DIM)
    w_k = jnp.pad(w_ukv[:, :, :QK_NOPE], ((0, 0), (0, 0), (0, LANES - QK_NOPE)))
    w_k = w_k.reshape(KV_LORA, N_TOK_HEADS * LANES).astype(BF16)
    w_vT = w_ukv[:, :, QK_NOPE:].reshape(KV_LORA, TOK_WIDTH).T.astype(BF16)
    pos = positions.astype(F32)[..., None]
    w_out_b = w_out.astype(BF16)

    q, k, vT, g, mg = _mla_proj(x, pos, w_in_a, mla_q_norm[0][None], mla_kv_norm[0][None], w_uq, w_k, w_vT,
                                jnp.asarray(_rope_constants()), kbd, vbd)
    tokT = _attention(q, k, vT)
    h1 = _out_ln(tokT, g, mg, x, w_out_b[0], ln_g[0][None], ln_b[0][None])

    return _lru_layer(h1, lru_w_in[0].astype(BF16), lru_conv_w[0], lru_conv_b[0][None],
                      _block_diag_tiles(lru_w_rgate[0]).astype(BF16), lru_b_rgate[0][None],
                      _block_diag_tiles(lru_w_igate[0]).astype(BF16), lru_b_igate[0][None],
                      lru_lambda[0][None], kbd, vbd, w_out_b[1], ln_g[1][None], ln_b[1][None])
```

```python
import functools
import math

import jax
import jax.numpy as jnp
import numpy as np
from jax import lax
from jax.experimental import pallas as pl
from jax.experimental.pallas import tpu as pltpu

F32 = jnp.float32
BF16 = jnp.bfloat16

D_MODEL = 1024
DEPTH = 2
MEM_LEN = 256
HEAD_DIM = 64
N_MEM_HEADS = 4
N_TOK_HEADS = 12
TOK_WIDTH = N_TOK_HEADS * HEAD_DIM
MEM_WIDTH = N_MEM_HEADS * HEAD_DIM
MIX_WIDTH = TOK_WIDTH + MEM_WIDTH
Q_LORA = 384
KV_LORA = 256
QK_NOPE = 64
QK_ROPE = 32
QK_DIM = QK_NOPE + QK_ROPE
ROPE_THETA = 10000.0
CONV_W = 4
LRU_C = 8.0
ALPHA = (2.0 * DEPTH) ** 0.25
NORM_EPS = 1e-6

LANES = 128
SUBLANES = 8
HALF_ROPE = QK_ROPE // 2
LOG2E = math.log2(math.e)
Q_SCALE = LOG2E / math.sqrt(QK_DIM)

TM_PROJ = 512
TK_ATTN = TM_PROJ
TQ_ATTN = 2 * TK_ATTN
TM_LRU = 256
VMEM_LIMIT = 48 * 1024 * 1024

_NT = (((1,), (1,)), ((), ()))


def _const_spec(shape):
    nd = len(shape)
    return pl.BlockSpec(shape, lambda *_: (0,) * nd, pipeline_mode=pl.Buffered(1))


def _silu(t):
    return t * (1.0 / (1.0 + jnp.exp(-t)))


def _sigmoid(t):
    return 1.0 / (1.0 + jnp.exp(-t))


def _rms_norm(t, g):
    return t * lax.rsqrt(jnp.mean(t * t, axis=-1, keepdims=True) + NORM_EPS) * g


def _layer_norm(t, g, b):
    mu = jnp.mean(t, axis=-1, keepdims=True)
    c = t - mu
    var = jnp.mean(c * c, axis=-1, keepdims=True)
    return c * lax.rsqrt(var + NORM_EPS) * g + b


def _memory_attention(qm, kbd_ref, vbd_ref):
    s = jnp.dot(qm.astype(BF16), kbd_ref[...], preferred_element_type=F32)
    ps = []
    for h in range(N_MEM_HEADS):
        sh = s[:, h * MEM_LEN:(h + 1) * MEM_LEN]
        e = jnp.exp(sh - jnp.max(sh, axis=-1, keepdims=True))
        ps.append((e * (1.0 / jnp.sum(e, axis=-1, keepdims=True))).astype(BF16))
    p = jnp.concatenate(ps, axis=-1)
    return jnp.dot(p, vbd_ref[...], preferred_element_type=F32)


def _mem_kv_kernel(mem_ref, wkT_ref, wv_ref, kbd_ref, vbd_ref):
    mb = mem_ref[0].astype(BF16)
    kT = lax.dot_general(wkT_ref[0].astype(BF16), mb, _NT, preferred_element_type=F32)
    kT = (kT * (1.0 / math.sqrt(HEAD_DIM))).astype(BF16)
    v = jnp.dot(mb, wv_ref[0].astype(BF16), preferred_element_type=F32).astype(BF16)
    row = lax.broadcasted_iota(jnp.int32, (MEM_WIDTH, MEM_LEN), 0)
    col = lax.broadcasted_iota(jnp.int32, (MEM_LEN, MEM_WIDTH), 1)
    zero = jnp.zeros((), BF16)
    for h in range(N_MEM_HEADS):
        lo, hi = h * HEAD_DIM, (h + 1) * HEAD_DIM
        kbd_ref[0, 0, :, h * MEM_LEN:(h + 1) * MEM_LEN] = jnp.where((row >= lo) & (row < hi), kT, zero)
        vbd_ref[0, 0, h * MEM_LEN:(h + 1) * MEM_LEN, :] = jnp.where((col >= lo) & (col < hi), v, zero)


def _mem_kv(mem, w_mem_kv):
    b = mem.shape[0]
    wkT = jnp.swapaxes(w_mem_kv[:, :, :MEM_WIDTH], 1, 2)
    wv = w_mem_kv[:, :, MEM_WIDTH:]
    return pl.pallas_call(
        _mem_kv_kernel,
        grid=(DEPTH, b),
        in_specs=[
            pl.BlockSpec((1, MEM_LEN, D_MODEL), lambda l, i: (i, 0, 0)),
            pl.BlockSpec((1, MEM_WIDTH, D_MODEL), lambda l, i: (l, 0, 0)),
            pl.BlockSpec((1, D_MODEL, MEM_WIDTH), lambda l, i: (l, 0, 0)),
        ],
        out_specs=[
            pl.BlockSpec((1, 1, MEM_WIDTH, N_MEM_HEADS * MEM_LEN), lambda l, i: (l, i, 0, 0)),
            pl.BlockSpec((1, 1, N_MEM_HEADS * MEM_LEN, MEM_WIDTH), lambda l, i: (l, i, 0, 0)),
        ],
        out_shape=[
            jax.ShapeDtypeStruct((DEPTH, b, MEM_WIDTH, N_MEM_HEADS * MEM_LEN), BF16),
            jax.ShapeDtypeStruct((DEPTH, b, N_MEM_HEADS * MEM_LEN, MEM_WIDTH), BF16),
        ],
        name="mem_kv",
    )(mem, wkT, wv)


_A_CQ = (0, Q_LORA)
_A_CKV = (_A_CQ[1], _A_CQ[1] + KV_LORA)
_A_KR = (_A_CKV[1], _A_CKV[1] + LANES)
_A_GATE = (_A_KR[1], _A_KR[1] + MIX_WIDTH)
_A_QM = (_A_GATE[1], _A_GATE[1] + MEM_WIDTH)
_A_WIDTH = _A_QM[1]


def _mla_proj_kernel(x_ref, pos_ref, w_in_ref, qn_ref, kvn_ref, w_uq_ref, w_k_ref, w_vT_ref, rope_ref,
                     kbd_ref, vbd_ref, q_ref, k_ref, vT_ref, g_ref, mg_ref):
    xb = x_ref[0].astype(BF16)

    def proj(seg):
        return jnp.dot(xb, w_in_ref[:, seg[0]:seg[1]], preferred_element_type=F32)

    ang = pos_ref[0] * rope_ref[0:1, :]
    cos = jnp.cos(ang)
    sin = jnp.sin(ang)
    sin_lo = sin * rope_ref[1:2, :]
    sin_hi = sin * rope_ref[2:3, :]

    def rope(t):
        return (t * cos + pltpu.roll(t, LANES - HALF_ROPE, 1) * sin_lo
                + pltpu.roll(t, HALF_ROPE, 1) * sin_hi)

    cqn = _rms_norm(proj(_A_CQ), qn_ref[...]).astype(BF16)
    q = jnp.dot(cqn, w_uq_ref[...], preferred_element_type=F32)
    for h in range(N_TOK_HEADS):
        q_ref[0, h] = (rope(q[:, h * LANES:(h + 1) * LANES]) * Q_SCALE).astype(BF16)

    ckvn = _rms_norm(proj(_A_CKV), kvn_ref[...]).astype(BF16)
    k_rope = rope(proj(_A_KR))
    k = jnp.dot(ckvn, w_k_ref[...], preferred_element_type=F32)
    for h in range(N_TOK_HEADS):
        k_ref[0, h] = (k[:, h * LANES:(h + 1) * LANES] + k_rope).astype(BF16)

    vT = lax.dot_general(w_vT_ref[...], ckvn, _NT, preferred_element_type=F32)
    for h in range(N_TOK_HEADS):
        vT_ref[0, h, 0] = vT[h * HEAD_DIM:(h + 1) * HEAD_DIM, :].astype(BF16)

    g = _silu(proj(_A_GATE))
    g_ref[0] = g[:, :TOK_WIDTH]
    mg_ref[0] = _memory_attention(proj(_A_QM), kbd_ref.at[0, 0], vbd_ref.at[0, 0]) * g[:, TOK_WIDTH:]


def _mla_proj(x, pos, w_in_a, qn, kvn, w_uq, w_k, w_vT, rope_c, kbd, vbd):
    b, s, _ = x.shape
    tm = TM_PROJ
    nt = s // tm
    return pl.pallas_call(
        _mla_proj_kernel,
        grid=(b, nt),
        in_specs=[
            pl.BlockSpec((1, tm, D_MODEL), lambda i, j: (i, j, 0)),
            pl.BlockSpec((1, tm, 1), lambda i, j: (i, j, 0)),
            _const_spec((D_MODEL, _A_WIDTH)),
            _const_spec((1, Q_LORA)),
            _const_spec((1, KV_LORA)),
            _const_spec((Q_LORA, N_TOK_HEADS * LANES)),
            _const_spec((KV_LORA, N_TOK_HEADS * LANES)),
            _const_spec((TOK_WIDTH, KV_LORA)),
            _const_spec((SUBLANES, LANES)),
            pl.BlockSpec((1, 1, MEM_WIDTH, N_MEM_HEADS * MEM_LEN), lambda i, j: (0, i, 0, 0)),
            pl.BlockSpec((1, 1, N_MEM_HEADS * MEM_LEN, MEM_WIDTH), lambda i, j: (0, i, 0, 0)),
        ],
        out_specs=[
            pl.BlockSpec((1, N_TOK_HEADS, tm, LANES), lambda i, j: (i, 0, j, 0)),
            pl.BlockSpec((1, N_TOK_HEADS, tm, LANES), lambda i, j: (i, 0, j, 0)),
            pl.BlockSpec((1, N_TOK_HEADS, 1, HEAD_DIM, tm), lambda i, j: (i, 0, j, 0, 0)),
            pl.BlockSpec((1, tm, TOK_WIDTH), lambda i, j: (i, j, 0)),
            pl.BlockSpec((1, tm, MEM_WIDTH), lambda i, j: (i, j, 0)),
        ],
        out_shape=[
            jax.ShapeDtypeStruct((b, N_TOK_HEADS, s, LANES), BF16),
            jax.ShapeDtypeStruct((b, N_TOK_HEADS, s, LANES), BF16),
            jax.ShapeDtypeStruct((b, N_TOK_HEADS, nt, HEAD_DIM, tm), BF16),
            jax.ShapeDtypeStruct((b, s, TOK_WIDTH), F32),
            jax.ShapeDtypeStruct((b, s, MEM_WIDTH), F32),
        ],
        compiler_params=pltpu.CompilerParams(
            dimension_semantics=("arbitrary", "arbitrary"), vmem_limit_bytes=VMEM_LIMIT),
        name="mla_proj",
    )(x, pos, w_in_a, qn, kvn, w_uq, w_k, w_vT, rope_c, kbd, vbd)


def _attn_kernel(q_ref, k_ref, vT_ref, o_ref, s_ref, bm_ref, m_ref, l_ref, acc_ref):
    tq, tk = TQ_ATTN, TK_ATTN
    qi = pl.program_id(2)
    m_ref[...] = jnp.full((1, tq), -jnp.inf, F32)
    l_ref[...] = jnp.zeros((1, tq), F32)
    acc_ref[...] = jnp.zeros((HEAD_DIM, tq), F32)

    def scores(chunk, c0, c1, slot, on_diagonal):
        start = pl.multiple_of(chunk * tk, tk)
        kt = k_ref[0, 0, pl.ds(start, tk), :]
        st = lax.dot_general(kt, q_ref[0, 0, c0:c1, :], _NT, preferred_element_type=F32)
        if on_diagonal:
            key = lax.broadcasted_iota(jnp.int32, st.shape, 0)
            qry = lax.broadcasted_iota(jnp.int32, st.shape, 1)
            st = jnp.where(key <= qry, st, -jnp.inf)
        s_ref[slot, :, c0:c1] = st
        bm_ref[slot, :, c0:c1] = jnp.max(st, axis=0, keepdims=True)

    def update(chunk, c0, c1, slot):
        m_old = m_ref[:, c0:c1]
        m_new = jnp.maximum(m_old, bm_ref[slot, :, c0:c1])
        alpha = jnp.exp2(m_old - m_new)
        p = jnp.exp2(s_ref[slot, :, c0:c1] - m_new)
        l_ref[:, c0:c1] = alpha * l_ref[:, c0:c1] + jnp.sum(p, axis=0, keepdims=True)
        m_ref[:, c0:c1] = m_new
        pv = jnp.dot(vT_ref[0, 0, chunk], p.astype(BF16), preferred_element_type=F32)
        acc_ref[:, c0:c1] = alpha * acc_ref[:, c0:c1] + pv

    @pl.when(qi > 0)
    def _():
        scores(0, 0, tq, 0, False)

        def pair(p, _):
            c = 2 * p
            scores(c + 1, 0, tq, 1, False)
            update(c, 0, tq, 0)
            scores(c + 2, 0, tq, 0, False)
            update(c + 1, 0, tq, 1)
            return 0

        lax.fori_loop(0, qi - 1, pair, 0)
        c = 2 * (qi - 1)
        scores(c + 1, 0, tq, 1, False)
        update(c, 0, tq, 0)
        update(c + 1, 0, tq, 1)

    d = 2 * qi
    scores(d, 0, tk, 0, True)
    scores(d, tk, tq, 0, False)
    scores(d + 1, tk, tq, 1, True)
    update(d, 0, tq, 0)
    update(d + 1, tk, tq, 1)
    o_ref[0] = acc_ref[...] * (1.0 / l_ref[...])


def _attention(q, k, vT):
    b, h, s, _ = q.shape
    tq, tk = TQ_ATTN, TK_ATTN
    nkv = vT.shape[2]
    return pl.pallas_call(
        _attn_kernel,
        grid=(b, h, s // tq),
        in_specs=[
            pl.BlockSpec((1, 1, tq, LANES), lambda i, j, t: (i, j, t, 0)),
            pl.BlockSpec((1, 1, s, LANES), lambda i, j, t: (i, j, 0, 0)),
            pl.BlockSpec((1, 1, nkv, HEAD_DIM, tk), lambda i, j, t: (i, j, 0, 0, 0)),
        ],
        out_specs=pl.BlockSpec((1, HEAD_DIM, tq), lambda i, j, t: (i, j, t)),
        out_shape=jax.ShapeDtypeStruct((b, h * HEAD_DIM, s), F32),
        scratch_shapes=[
            pltpu.VMEM((2, tk, tq), F32),
            pltpu.VMEM((2, 1, tq), F32),
            pltpu.VMEM((1, tq), F32),
            pltpu.VMEM((1, tq), F32),
            pltpu.VMEM((HEAD_DIM, tq), F32),
        ],
        compiler_params=pltpu.CompilerParams(
            dimension_semantics=("arbitrary", "arbitrary", "arbitrary"), vmem_limit_bytes=VMEM_LIMIT),
        name="mla_attn",
    )(q, k, vT)


def _out_ln_kernel(tokT_ref, g_ref, mg_ref, h_ref, w_out_ref, lng_ref, lnb_ref, o_ref):
    tok = tokT_ref[0].T
    y = (tok * g_ref[0]).astype(BF16)
    o = jnp.dot(y, w_out_ref[:TOK_WIDTH, :], preferred_element_type=F32)
    o = o + jnp.dot(mg_ref[0].astype(BF16), w_out_ref[TOK_WIDTH:, :], preferred_element_type=F32)
    o_ref[0] = _layer_norm(ALPHA * h_ref[0] + o, lng_ref[...], lnb_ref[...])


def _out_ln(tokT, g, mg, h, w_out, ln_g, ln_b):
    b, s, _ = h.shape
    tm = TM_PROJ
    return pl.pallas_call(
        _out_ln_kernel,
        grid=(b, s // tm),
        in_specs=[
            pl.BlockSpec((1, TOK_WIDTH, tm), lambda i, j: (i, 0, j)),
            pl.BlockSpec((1, tm, TOK_WIDTH), lambda i, j: (i, j, 0)),
            pl.BlockSpec((1, tm, MEM_WIDTH), lambda i, j: (i, j, 0)),
            pl.BlockSpec((1, tm, D_MODEL), lambda i, j: (i, j, 0)),
            _const_spec((MIX_WIDTH, D_MODEL)),
            _const_spec((1, D_MODEL)),
            _const_spec((1, D_MODEL)),
        ],
        out_specs=pl.BlockSpec((1, tm, D_MODEL), lambda i, j: (i, j, 0)),
        out_shape=jax.ShapeDtypeStruct((b, s, D_MODEL), F32),
        compiler_params=pltpu.CompilerParams(
            dimension_semantics=("arbitrary", "arbitrary"), vmem_limit_bytes=VMEM_LIMIT),
        name="mla_out_ln",
    )(tokT, g, mg, h, w_out, ln_g, ln_b)


_B_U = (0, TOK_WIDTH)
_B_GATE = (_B_U[1], _B_U[1] + MIX_WIDTH)
_B_QM = (_B_GATE[1], _B_GATE[1] + MEM_WIDTH)
_B_WIDTH = _B_QM[1]
GATE_TILE = 256


def _linear_scan(a, b):
    n = a.shape[0]
    row = lax.broadcasted_iota(jnp.int32, a.shape, 0)
    d = 1
    while d < n:
        live = row >= d
        b = a * jnp.where(live, pltpu.roll(b, d, 0), 0.0) + b
        if 2 * d < n:
            a = a * jnp.where(live, pltpu.roll(a, d, 0), 1.0)
        d *= 2
    return b


def _lru_layer_kernel(h_ref, w_in_ref, conv_w_ref, conv_b_ref, w_r_ref, b_r_ref, w_i_ref, b_i_ref, lam_ref,
                      kbd_ref, vbd_ref, w_out_ref, lng_ref, lnb_ref, o_ref, ubuf_ref, carry_ref):
    tm = TM_LRU
    step = pl.program_id(1)

    @pl.when(step == 0)
    def _():
        ubuf_ref[0:SUBLANES, :] = jnp.zeros((SUBLANES, TOK_WIDTH), F32)
        carry_ref[...] = jnp.zeros((1, TOK_WIDTH), F32)

    hin = h_ref[0]
    hb = hin.astype(BF16)

    def proj(seg):
        return jnp.dot(hb, w_in_ref[:, seg[0]:seg[1]], preferred_element_type=F32)

    ubuf_ref[SUBLANES:SUBLANES + tm, :] = proj(_B_U)
    xc = conv_b_ref[...] + ubuf_ref[SUBLANES:SUBLANES + tm, :] * conv_w_ref[CONV_W - 1:CONV_W, :]
    for back in range(1, CONV_W):
        tap = CONV_W - 1 - back
        xc = xc + ubuf_ref[SUBLANES - back:SUBLANES - back + tm, :] * conv_w_ref[tap:tap + 1, :]
    ubuf_ref[0:SUBLANES, :] = ubuf_ref[tm:tm + SUBLANES, :]

    xcb = xc.astype(BF16)
    r_parts, i_parts = [], []
    for c in range(TOK_WIDTH // GATE_TILE):
        xs = xcb[:, c * GATE_TILE:(c + 1) * GATE_TILE]
        r_parts.append(jnp.dot(xs, w_r_ref[c], preferred_element_type=F32))
        i_parts.append(jnp.dot(xs, w_i_ref[c], preferred_element_type=F32))
    r = _sigmoid(jnp.concatenate(r_parts, axis=-1) + b_r_ref[...])
    i = _sigmoid(jnp.concatenate(i_parts, axis=-1) + b_i_ref[...])

    neg_lam = -lam_ref[...]
    softplus = jnp.maximum(neg_lam, 0.0) + jnp.log1p(jnp.exp(-jnp.abs(neg_lam)))
    log_a = (-LRU_C * softplus) * r
    a = jnp.exp(log_a)
    gated = jnp.sqrt(-jnp.tanh(log_a) * (a * a + 1.0)) * (i * xc)

    row = lax.broadcasted_iota(jnp.int32, (tm, TOK_WIDTH), 0)
    gated = jnp.where(row == 0, a * carry_ref[...] + gated, gated)
    hs = _linear_scan(a, gated)
    carry_ref[...] = hs[tm - 1:tm, :]

    g = _silu(proj(_B_GATE))
    mem_out = _memory_attention(proj(_B_QM), kbd_ref.at[0, 0], vbd_ref.at[0, 0])
    y_tok = (hs * g[:, :TOK_WIDTH]).astype(BF16)
    y_mem = (mem_out * g[:, TOK_WIDTH:]).astype(BF16)
    o = jnp.dot(y_tok, w_out_ref[:TOK_WIDTH, :], preferred_element_type=F32)
    o = o + jnp.dot(y_mem, w_out_ref[TOK_WIDTH:, :], preferred_element_type=F32)
    o_ref[0] = _layer_norm(ALPHA * hin + o, lng_ref[...], lnb_ref[...])


def _lru_layer(h, w_in, conv_w, conv_b, w_r, b_r, w_i, b_i, lam, kbd, vbd, w_out, ln_g, ln_b):
    b, s, _ = h.shape
    tm = TM_LRU
    n_gate_tiles = TOK_WIDTH // GATE_TILE
    return pl.pallas_call(
        _lru_layer_kernel,
        grid=(b, s // tm),
        in_specs=[
            pl.BlockSpec((1, tm, D_MODEL), lambda i, j: (i, j, 0)),
            _const_spec((D_MODEL, _B_WIDTH)),
            _const_spec((CONV_W, TOK_WIDTH)),
            _const_spec((1, TOK_WIDTH)),
            _const_spec((n_gate_tiles, GATE_TILE, GATE_TILE)),
            _const_spec((1, TOK_WIDTH)),
            _const_spec((n_gate_tiles, GATE_TILE, GATE_TILE)),
            _const_spec((1, TOK_WIDTH)),
            _const_spec((1, TOK_WIDTH)),
            pl.BlockSpec((1, 1, MEM_WIDTH, N_MEM_HEADS * MEM_LEN), lambda i, j: (1, i, 0, 0)),
            pl.BlockSpec((1, 1, N_MEM_HEADS * MEM_LEN, MEM_WIDTH), lambda i, j: (1, i, 0, 0)),
            _const_spec((MIX_WIDTH, D_MODEL)),
            _const_spec((1, D_MODEL)),
            _const_spec((1, D_MODEL)),
        ],
        out_specs=pl.BlockSpec((1, tm, D_MODEL), lambda i, j: (i, j, 0)),
        out_shape=jax.ShapeDtypeStruct((b, s, D_MODEL), F32),
        scratch_shapes=[
            pltpu.VMEM((tm + SUBLANES, TOK_WIDTH), F32),
            pltpu.VMEM((1, TOK_WIDTH), F32),
        ],
        compiler_params=pltpu.CompilerParams(
            dimension_semantics=("arbitrary", "arbitrary"), vmem_limit_bytes=VMEM_LIMIT),
        name="lru_layer",
    )(h, w_in, conv_w, conv_b, w_r, b_r, w_i, b_i, lam, kbd, vbd, w_out, ln_g, ln_b)


def _block_diag_tiles(w):
    per_tile = GATE_TILE // HEAD_DIM
    w = w.reshape(TOK_WIDTH // GATE_TILE, per_tile, HEAD_DIM, HEAD_DIM)
    eye = jnp.eye(per_tile, dtype=w.dtype)
    t = w[:, :, :, None, :] * eye[None, :, None, :, None]
    return t.reshape(TOK_WIDTH // GATE_TILE, GATE_TILE, GATE_TILE)


def _rope_constants():
    inv_freq = ROPE_THETA ** (-np.arange(HALF_ROPE, dtype=np.float32) / np.float32(HALF_ROPE))
    c = np.zeros((SUBLANES, LANES), np.float32)
    c[0, QK_NOPE:QK_NOPE + HALF_ROPE] = inv_freq
    c[0, QK_NOPE + HALF_ROPE:QK_NOPE + QK_ROPE] = inv_freq
    c[1, QK_NOPE:QK_NOPE + HALF_ROPE] = -1.0
    c[2, QK_NOPE + HALF_ROPE:QK_NOPE + QK_ROPE] = 1.0
    return c


def kernel(x, mem, positions, mla_w_in, mla_q_norm, mla_w_uq, mla_kv_norm, mla_w_ukv, lru_w_in, lru_conv_w,
           lru_conv_b, lru_w_rgate, lru_b_rgate, lru_w_igate, lru_b_igate, lru_lambda, w_mem_kv, w_out, ln_g, ln_b):
    b, s, _ = x.shape
    assert s % TQ_ATTN == 0 and s % TM_LRU == 0

    kbd, vbd = _mem_kv(mem, w_mem_kv)

    w_in = mla_w_in[0]
    o_q, o_kv, o_kr, o_gate = Q_LORA, Q_LORA + KV_LORA, Q_LORA + KV_LORA + QK_ROPE, Q_LORA + KV_LORA + QK_ROPE + MIX_WIDTH
    w_kr = jnp.pad(w_in[:, o_kv:o_kr], ((0, 0), (QK_NOPE, LANES - QK_DIM)))
    w_in_a = jnp.concatenate(
        [w_in[:, :o_q], w_in[:, o_q:o_kv], w_kr, w_in[:, o_kr:o_gate], w_in[:, o_gate:]], axis=1).astype(BF16)
    w_uq = jnp.pad(mla_w_uq[0].reshape(Q_LORA, N_TOK_HEADS, QK_DIM), ((0, 0), (0, 0), (0, LANES - QK_DIM)))
    w_uq = w_uq.reshape(Q_LORA, N_TOK_HEADS * LANES).astype(BF16)
    w_ukv = mla_w_ukv[0].reshape(KV_LORA, N_TOK_HEADS, QK_NOPE + HEAD_DIM)
    w_k = jnp.pad(w_ukv[:, :, :QK_NOPE], ((0, 0), (0, 0), (0, LANES - QK_NOPE)))
    w_k = w_k.reshape(KV_LORA, N_TOK_HEADS * LANES).astype(BF16)
    w_vT = w_ukv[:, :, QK_NOPE:].reshape(KV_LORA, TOK_WIDTH).T.astype(BF16)
    pos = positions.astype(F32)[..., None]
    w_out_b = w_out.astype(BF16)

    q, k, vT, g, mg = _mla_proj(x, pos, w_in_a, mla_q_norm[0][None], mla_kv_norm[0][None], w_uq, w_k, w_vT,
                                jnp.asarray(_rope_constants()), kbd, vbd)
    tokT = _attention(q, k, vT)
    h1 = _out_ln(tokT, g, mg, x, w_out_b[0], ln_g[0][None], ln_b[0][None])

    return _lru_layer(h1, lru_w_in[0].astype(BF16), lru_conv_w[0], lru_conv_b[0][None],
                      _block_diag_tiles(lru_w_rgate[0]).astype(BF16), lru_b_rgate[0][None],
                      _block_diag_tiles(lru_w_igate[0]).astype(BF16), lru_b_igate[0][None],
                      lru_lambda[0][None], kbd, vbd, w_out_b[1], ln_g[1][None], ln_b[1][None])
```

```python
import functools
import math

import jax
import jax.numpy as jnp
import numpy as np
from jax import lax
from jax.experimental import pallas as pl
from jax.experimental.pallas import tpu as pltpu

F32 = jnp.float32
BF16 = jnp.bfloat16

D_MODEL = 1024
DEPTH = 2
MEM_LEN = 256
HEAD_DIM = 64
N_MEM_HEADS = 4
N_TOK_HEADS = 12
TOK_WIDTH = N_TOK_HEADS * HEAD_DIM
MEM_WIDTH = N_MEM_HEADS * HEAD_DIM
MIX_WIDTH = TOK_WIDTH + MEM_WIDTH
Q_LORA = 384
KV_LORA = 256
QK_NOPE = 64
QK_ROPE = 32
QK_DIM = QK_NOPE + QK_ROPE
ROPE_THETA = 10000.0
CONV_W = 4
LRU_C = 8.0
ALPHA = (2.0 * DEPTH) ** 0.25
NORM_EPS = 1e-6

LANES = 128
SUBLANES = 8
HALF_ROPE = QK_ROPE // 2
LOG2E = math.log2(math.e)
Q_SCALE = LOG2E / math.sqrt(QK_DIM)

TM_PROJ = 512
TK_ATTN = TM_PROJ
TQ_ATTN = 2 * TK_ATTN
GW_ATTN = 256
TM_LRU = 512
VMEM_LIMIT = 48 * 1024 * 1024

_NT = (((1,), (1,)), ((), ()))


def _const_spec(shape):
    nd = len(shape)
    return pl.BlockSpec(shape, lambda *_: (0,) * nd, pipeline_mode=pl.Buffered(1))


def _sigmoid(t):
    return 0.5 * jnp.tanh(0.5 * t) + 0.5


def _silu(t):
    h = 0.5 * t
    return h * jnp.tanh(h) + h


def _rms_norm(t, g):
    return t * lax.rsqrt(jnp.mean(t * t, axis=-1, keepdims=True) + NORM_EPS) * g


def _layer_norm(t, g, b):
    mu = jnp.mean(t, axis=-1, keepdims=True)
    c = t - mu
    var = jnp.mean(c * c, axis=-1, keepdims=True)
    return c * lax.rsqrt(var + NORM_EPS) * g + b


def _memory_attention(qm, kbd_ref, vbd_ref):
    s = jnp.dot(qm.astype(BF16), kbd_ref[...], preferred_element_type=F32)
    ps = []
    for h in range(N_MEM_HEADS):
        sh = s[:, h * MEM_LEN:(h + 1) * MEM_LEN]
        e = jnp.exp(sh - jnp.max(sh, axis=-1, keepdims=True))
        ps.append((e * (1.0 / jnp.sum(e, axis=-1, keepdims=True))).astype(BF16))
    p = jnp.concatenate(ps, axis=-1)
    return jnp.dot(p, vbd_ref[...], preferred_element_type=F32)


def _mem_kv_kernel(mem_ref, wkT_ref, wv_ref, kbd_ref, vbd_ref):
    mb = mem_ref[0].astype(BF16)
    kT = lax.dot_general(wkT_ref[0].astype(BF16), mb, _NT, preferred_element_type=F32)
    kT = (kT * (1.0 / math.sqrt(HEAD_DIM))).astype(BF16)
    v = jnp.dot(mb, wv_ref[0].astype(BF16), preferred_element_type=F32).astype(BF16)
    row = lax.broadcasted_iota(jnp.int32, (MEM_WIDTH, MEM_LEN), 0)
    col = lax.broadcasted_iota(jnp.int32, (MEM_LEN, MEM_WIDTH), 1)
    zero = jnp.zeros((), BF16)
    for h in range(N_MEM_HEADS):
        lo, hi = h * HEAD_DIM, (h + 1) * HEAD_DIM
        kbd_ref[0, 0, :, h * MEM_LEN:(h + 1) * MEM_LEN] = jnp.where((row >= lo) & (row < hi), kT, zero)
        vbd_ref[0, 0, h * MEM_LEN:(h + 1) * MEM_LEN, :] = jnp.where((col >= lo) & (col < hi), v, zero)


def _mem_kv(mem, w_mem_kv):
    b = mem.shape[0]
    wkT = jnp.swapaxes(w_mem_kv[:, :, :MEM_WIDTH], 1, 2)
    wv = w_mem_kv[:, :, MEM_WIDTH:]
    return pl.pallas_call(
        _mem_kv_kernel,
        grid=(DEPTH, b),
        in_specs=[
            pl.BlockSpec((1, MEM_LEN, D_MODEL), lambda l, i: (i, 0, 0)),
            pl.BlockSpec((1, MEM_WIDTH, D_MODEL), lambda l, i: (l, 0, 0)),
            pl.BlockSpec((1, D_MODEL, MEM_WIDTH), lambda l, i: (l, 0, 0)),
        ],
        out_specs=[
            pl.BlockSpec((1, 1, MEM_WIDTH, N_MEM_HEADS * MEM_LEN), lambda l, i: (l, i, 0, 0)),
            pl.BlockSpec((1, 1, N_MEM_HEADS * MEM_LEN, MEM_WIDTH), lambda l, i: (l, i, 0, 0)),
        ],
        out_shape=[
            jax.ShapeDtypeStruct((DEPTH, b, MEM_WIDTH, N_MEM_HEADS * MEM_LEN), BF16),
            jax.ShapeDtypeStruct((DEPTH, b, N_MEM_HEADS * MEM_LEN, MEM_WIDTH), BF16),
        ],
        name="mem_kv",
    )(mem, wkT, wv)


_A_CQ = (0, Q_LORA)
_A_CKV = (_A_CQ[1], _A_CQ[1] + KV_LORA)
_A_KR = (_A_CKV[1], _A_CKV[1] + LANES)
_A_GATE = (_A_KR[1], _A_KR[1] + MIX_WIDTH)
_A_QM = (_A_GATE[1], _A_GATE[1] + MEM_WIDTH)
_A_WIDTH = _A_QM[1]


def _mla_proj_kernel(x_ref, pos_ref, w_in_ref, qn_ref, kvn_ref, w_uq_ref, w_k_ref, w_vT_ref, rope_ref,
                     kbd_ref, vbd_ref, q_ref, k_ref, vT_ref, g_ref, mg_ref):
    xb = x_ref[0].astype(BF16)

    def proj(seg):
        return jnp.dot(xb, w_in_ref[:, seg[0]:seg[1]], preferred_element_type=F32)

    ang = pos_ref[0] * rope_ref[0:1, :]
    cos = jnp.cos(ang)
    sin = jnp.sin(ang)
    sin_lo = sin * rope_ref[1:2, :]
    sin_hi = sin * rope_ref[2:3, :]

    def rope(t):
        return (t * cos + pltpu.roll(t, LANES - HALF_ROPE, 1) * sin_lo
                + pltpu.roll(t, HALF_ROPE, 1) * sin_hi)

    cqn = _rms_norm(proj(_A_CQ), qn_ref[...]).astype(BF16)
    q = jnp.dot(cqn, w_uq_ref[...], preferred_element_type=F32)
    for h in range(N_TOK_HEADS):
        q_ref[0, h] = (rope(q[:, h * LANES:(h + 1) * LANES]) * Q_SCALE).astype(BF16)

    ckvn = _rms_norm(proj(_A_CKV), kvn_ref[...]).astype(BF16)
    k_rope = rope(proj(_A_KR))
    k = jnp.dot(ckvn, w_k_ref[...], preferred_element_type=F32)
    for h in range(N_TOK_HEADS):
        k_ref[0, h] = (k[:, h * LANES:(h + 1) * LANES] + k_rope).astype(BF16)

    vT = lax.dot_general(w_vT_ref[...], ckvn, _NT, preferred_element_type=F32)
    for h in range(N_TOK_HEADS):
        vT_ref[0, h, 0] = vT[h * HEAD_DIM:(h + 1) * HEAD_DIM, :].astype(BF16)

    g = _silu(proj(_A_GATE))
    g_ref[0] = g[:, :TOK_WIDTH]
    mg_ref[0] = _memory_attention(proj(_A_QM), kbd_ref.at[0, 0], vbd_ref.at[0, 0]) * g[:, TOK_WIDTH:]


def _mla_proj(x, pos, w_in_a, qn, kvn, w_uq, w_k, w_vT, rope_c, kbd, vbd):
    b, s, _ = x.shape
    tm = TM_PROJ
    nt = s // tm
    return pl.pallas_call(
        _mla_proj_kernel,
        grid=(b, nt),
        in_specs=[
            pl.BlockSpec((1, tm, D_MODEL), lambda i, j: (i, j, 0)),
            pl.BlockSpec((1, tm, 1), lambda i, j: (i, j, 0)),
            _const_spec((D_MODEL, _A_WIDTH)),
            _const_spec((1, Q_LORA)),
            _const_spec((1, KV_LORA)),
            _const_spec((Q_LORA, N_TOK_HEADS * LANES)),
            _const_spec((KV_LORA, N_TOK_HEADS * LANES)),
            _const_spec((TOK_WIDTH, KV_LORA)),
            _const_spec((SUBLANES, LANES)),
            pl.BlockSpec((1, 1, MEM_WIDTH, N_MEM_HEADS * MEM_LEN), lambda i, j: (0, i, 0, 0)),
            pl.BlockSpec((1, 1, N_MEM_HEADS * MEM_LEN, MEM_WIDTH), lambda i, j: (0, i, 0, 0)),
        ],
        out_specs=[
            pl.BlockSpec((1, N_TOK_HEADS, tm, LANES), lambda i, j: (i, 0, j, 0)),
            pl.BlockSpec((1, N_TOK_HEADS, tm, LANES), lambda i, j: (i, 0, j, 0)),
            pl.BlockSpec((1, N_TOK_HEADS, 1, HEAD_DIM, tm), lambda i, j: (i, 0, j, 0, 0)),
            pl.BlockSpec((1, tm, TOK_WIDTH), lambda i, j: (i, j, 0)),
            pl.BlockSpec((1, tm, MEM_WIDTH), lambda i, j: (i, j, 0)),
        ],
        out_shape=[
            jax.ShapeDtypeStruct((b, N_TOK_HEADS, s, LANES), BF16),
            jax.ShapeDtypeStruct((b, N_TOK_HEADS, s, LANES), BF16),
            jax.ShapeDtypeStruct((b, N_TOK_HEADS, nt, HEAD_DIM, tm), BF16),
            jax.ShapeDtypeStruct((b, s, TOK_WIDTH), F32),
            jax.ShapeDtypeStruct((b, s, MEM_WIDTH), F32),
        ],
        compiler_params=pltpu.CompilerParams(
            dimension_semantics=("arbitrary", "arbitrary"), vmem_limit_bytes=VMEM_LIMIT),
        name="mla_proj",
    )(x, pos, w_in_a, qn, kvn, w_uq, w_k, w_vT, rope_c, kbd, vbd)


def _attn_kernel(q_ref, k_ref, vT_ref, o_ref, s_ref, bm_ref, m_ref, l_ref, acc_ref):
    tq, tk, gw = TQ_ATTN, TK_ATTN, GW_ATTN
    groups = range(tq // gw)
    qi = pl.program_id(2)
    m_ref[...] = jnp.full((1, tq), -jnp.inf, F32)
    l_ref[...] = jnp.zeros((1, tq), F32)
    acc_ref[...] = jnp.zeros((HEAD_DIM, tq), F32)

    def scores(chunk, g, slot, key_off=None):
        c0 = g * gw
        if key_off is not None and key_off > c0 + gw - 1:
            return
        start = pl.multiple_of(chunk * tk, tk)
        kt = k_ref[0, 0, pl.ds(start, tk), :]
        st = lax.dot_general(kt, q_ref[0, 0, c0:c0 + gw, :], _NT, preferred_element_type=F32)
        if key_off is not None and key_off + tk - 1 > c0:
            key = lax.broadcasted_iota(jnp.int32, st.shape, 0) + key_off
            qry = lax.broadcasted_iota(jnp.int32, st.shape, 1) + c0
            st = jnp.where(key <= qry, st, -jnp.inf)
        s_ref[slot, :, c0:c0 + gw] = st
        bm_ref[slot, :, c0:c0 + gw] = jnp.max(st, axis=0, keepdims=True)

    def update(chunk, g, slot):
        c0, c1 = g * gw, (g + 1) * gw
        m_old = m_ref[:, c0:c1]
        m_new = jnp.maximum(m_old, bm_ref[slot, :, c0:c1])
        alpha = jnp.exp2(m_old - m_new)
        p = jnp.exp2(s_ref[slot, :, c0:c1] - m_new)
        l_ref[:, c0:c1] = alpha * l_ref[:, c0:c1] + jnp.sum(p, axis=0, keepdims=True)
        m_ref[:, c0:c1] = m_new
        pv = jnp.dot(vT_ref[0, 0, chunk], p.astype(BF16), preferred_element_type=F32)
        acc_ref[:, c0:c1] = alpha * acc_ref[:, c0:c1] + pv

    def stage(nxt_chunk, nxt_key_off, cur_chunk, cur_slot, cur_groups=groups):
        for g in groups:
            if nxt_chunk is not None:
                scores(nxt_chunk, g, 1 - cur_slot, nxt_key_off)
            if g in cur_groups:
                update(cur_chunk, g, cur_slot)

    d = 2 * qi
    right = range(tk // gw, tq // gw)

    def diagonal_tail():
        stage(d + 1, tk, d, 0)
        stage(None, None, d + 1, 1, right)
        o_ref[0] = acc_ref[...] * (1.0 / l_ref[...])

    @pl.when(qi > 0)
    def _():
        for g in groups:
            scores(0, g, 0)

        def pair(p, _):
            c = 2 * p
            stage(c + 1, None, c, 0)
            stage(c + 2, None, c + 1, 1)
            return 0

        lax.fori_loop(0, qi - 1, pair, 0)
        stage(d - 1, None, d - 2, 0)
        stage(d, 0, d - 1, 1)
        diagonal_tail()

    @pl.when(qi == 0)
    def _():
        for g in groups:
            scores(d, g, 0, 0)
        diagonal_tail()


def _attention(q, k, vT):
    b, h, s, _ = q.shape
    tq, tk = TQ_ATTN, TK_ATTN
    nkv = vT.shape[2]
    return pl.pallas_call(
        _attn_kernel,
        grid=(b, h, s // tq),
        in_specs=[
            pl.BlockSpec((1, 1, tq, LANES), lambda i, j, t: (i, j, t, 0)),
            pl.BlockSpec((1, 1, s, LANES), lambda i, j, t: (i, j, 0, 0)),
            pl.BlockSpec((1, 1, nkv, HEAD_DIM, tk), lambda i, j, t: (i, j, 0, 0, 0)),
        ],
        out_specs=pl.BlockSpec((1, HEAD_DIM, tq), lambda i, j, t: (i, j, t)),
        out_shape=jax.ShapeDtypeStruct((b, h * HEAD_DIM, s), F32),
        scratch_shapes=[
            pltpu.VMEM((2, tk, tq), F32),
            pltpu.VMEM((2, 1, tq), F32),
            pltpu.VMEM((1, tq), F32),
            pltpu.VMEM((1, tq), F32),
            pltpu.VMEM((HEAD_DIM, tq), F32),
        ],
        compiler_params=pltpu.CompilerParams(
            dimension_semantics=("arbitrary", "arbitrary", "arbitrary"), vmem_limit_bytes=VMEM_LIMIT),
        name="mla_attn",
    )(q, k, vT)


def _out_ln_kernel(tokT_ref, g_ref, mg_ref, h_ref, w_out_ref, lng_ref, lnb_ref, o_ref):
    tok = tokT_ref[0].T
    y = (tok * g_ref[0]).astype(BF16)
    o = jnp.dot(y, w_out_ref[:TOK_WIDTH, :], preferred_element_type=F32)
    o = o + jnp.dot(mg_ref[0].astype(BF16), w_out_ref[TOK_WIDTH:, :], preferred_element_type=F32)
    o_ref[0] = _layer_norm(ALPHA * h_ref[0] + o, lng_ref[...], lnb_ref[...])


def _out_ln(tokT, g, mg, h, w_out, ln_g, ln_b):
    b, s, _ = h.shape
    tm = TM_PROJ
    return pl.pallas_call(
        _out_ln_kernel,
        grid=(b, s // tm),
        in_specs=[
            pl.BlockSpec((1, TOK_WIDTH, tm), lambda i, j: (i, 0, j)),
            pl.BlockSpec((1, tm, TOK_WIDTH), lambda i, j: (i, j, 0)),
            pl.BlockSpec((1, tm, MEM_WIDTH), lambda i, j: (i, j, 0)),
            pl.BlockSpec((1, tm, D_MODEL), lambda i, j: (i, j, 0)),
            _const_spec((MIX_WIDTH, D_MODEL)),
            _const_spec((1, D_MODEL)),
            _const_spec((1, D_MODEL)),
        ],
        out_specs=pl.BlockSpec((1, tm, D_MODEL), lambda i, j: (i, j, 0)),
        out_shape=jax.ShapeDtypeStruct((b, s, D_MODEL), F32),
        compiler_params=pltpu.CompilerParams(
            dimension_semantics=("arbitrary", "arbitrary"), vmem_limit_bytes=VMEM_LIMIT),
        name="mla_out_ln",
    )(tokT, g, mg, h, w_out, ln_g, ln_b)


_B_U = (0, TOK_WIDTH)
_B_GATE = (_B_U[1], _B_U[1] + MIX_WIDTH)
_B_QM = (_B_GATE[1], _B_GATE[1] + MEM_WIDTH)
_B_WIDTH = _B_QM[1]
GATE_TILE = 256


def _linear_scan(a, b):
    n = a.shape[0]
    row = lax.broadcasted_iota(jnp.int32, a.shape, 0)
    d = 1
    while d < n:
        live = row >= d
        b = a * jnp.where(live, pltpu.roll(b, d, 0), 0.0) + b
        if 2 * d < n:
            a = a * jnp.where(live, pltpu.roll(a, d, 0), 1.0)
        d *= 2
    return b


def _lru_layer_kernel(h_ref, w_in_ref, conv_w_ref, conv_b_ref, w_r_ref, b_r_ref, w_i_ref, b_i_ref, lam_ref,
                      kbd_ref, vbd_ref, w_out_ref, lng_ref, lnb_ref, o_ref,
                      ebuf_ref, tail_ref, carry_ref, perm_in_ref, perm_out_ref):
    tm, sl = TM_LRU, SUBLANES
    n = tm // sl
    halo = (CONV_W - 1) * sl
    step = pl.program_id(1)

    @pl.when(step == 0)
    def _():
        tail_ref[...] = jnp.zeros((halo, TOK_WIDTH), F32)
        carry_ref[...] = jnp.zeros((sl, TOK_WIDTH), F32)

    lane_tiles = D_MODEL // LANES

    def segment_rows(j):
        s, i0 = divmod(sl * j, n)
        return pl.ds(sl * i0 + s, sl, stride=sl)

    for j in range(n):
        rows = h_ref[0, j * sl:(j + 1) * sl, :]
        for c in range(lane_tiles):
            perm_in_ref[c, segment_rows(j), :] = rows[:, c * LANES:(c + 1) * LANES]
    hin = jnp.concatenate([perm_in_ref[c] for c in range(lane_tiles)], axis=1)
    hb = hin.astype(BF16)

    def proj(seg):
        return jnp.dot(hb, w_in_ref[:, seg[0]:seg[1]], preferred_element_type=F32)

    sub8 = lax.broadcasted_iota(jnp.int32, (sl, TOK_WIDTH), 0)

    def from_previous_segment(prev_tile, cur_tile):
        return jnp.where(sub8 == 0, pltpu.roll(prev_tile, 1, 0), pltpu.roll(cur_tile, 1, 0))

    u = proj(_B_U)
    for k in range(CONV_W - 1):
        cur = u[tm - halo + k * sl:tm - halo + (k + 1) * sl, :]
        ebuf_ref[k * sl:(k + 1) * sl, :] = from_previous_segment(tail_ref[k * sl:(k + 1) * sl, :], cur)
    tail_ref[...] = u[tm - halo:, :]
    ebuf_ref[halo:halo + tm, :] = u
    xc = conv_b_ref[...] + u * conv_w_ref[CONV_W - 1:CONV_W, :]
    for back in range(1, CONV_W):
        tap = CONV_W - 1 - back
        xc = xc + ebuf_ref[halo - back * sl:halo - back * sl + tm, :] * conv_w_ref[tap:tap + 1, :]

    xcb = xc.astype(BF16)
    r_parts, i_parts = [], []
    for c in range(TOK_WIDTH // GATE_TILE):
        xs = xcb[:, c * GATE_TILE:(c + 1) * GATE_TILE]
        r_parts.append(jnp.dot(xs, w_r_ref[c], preferred_element_type=F32))
        i_parts.append(jnp.dot(xs, w_i_ref[c], preferred_element_type=F32))
    gate_r = _sigmoid(jnp.concatenate(r_parts, axis=-1) + b_r_ref[...])
    gate_i = _sigmoid(jnp.concatenate(i_parts, axis=-1) + b_i_ref[...])

    neg_lam = -lam_ref[...]
    softplus = jnp.maximum(neg_lam, 0.0) + jnp.log1p(jnp.exp(-jnp.abs(neg_lam)))
    log_a = (-LRU_C * softplus) * gate_r
    a = jnp.exp(log_a)
    one_minus_a2 = -jnp.tanh(log_a) * (a * a + 1.0)
    root = jnp.where(one_minus_a2 > 0.0, one_minus_a2 * lax.rsqrt(one_minus_a2), 0.0)
    b = root * (gate_i * xc)

    h_loc, a_run = [b[0:sl, :]], [a[0:sl, :]]
    for g in range(1, n):
        ag = a[g * sl:(g + 1) * sl, :]
        h_loc.append(ag * h_loc[-1] + b[g * sl:(g + 1) * sl, :])
        a_run.append(ag * a_run[-1])
    state_in = pltpu.roll(carry_ref[...], 1, 0)
    seg_b = jnp.where(sub8 == 0, a_run[-1] * state_in + h_loc[-1], h_loc[-1])
    seg_end = _linear_scan(a_run[-1], seg_b)
    start = from_previous_segment(carry_ref[...], seg_end)
    carry_ref[...] = seg_end
    hs = jnp.concatenate([h_loc[g] + a_run[g] * start for g in range(n)], axis=0)

    gate = _silu(proj(_B_GATE))
    mem_out = _memory_attention(proj(_B_QM), kbd_ref.at[0, 0], vbd_ref.at[0, 0])
    y_tok = (hs * gate[:, :TOK_WIDTH]).astype(BF16)
    y_mem = (mem_out * gate[:, TOK_WIDTH:]).astype(BF16)
    o = jnp.dot(y_tok, w_out_ref[:TOK_WIDTH, :], preferred_element_type=F32)
    o = o + jnp.dot(y_mem, w_out_ref[TOK_WIDTH:, :], preferred_element_type=F32)
    out = _layer_norm(ALPHA * hin + o, lng_ref[...], lnb_ref[...])
    for c in range(lane_tiles):
        perm_out_ref[c] = out[:, c * LANES:(c + 1) * LANES]
    for j in range(n):
        o_ref[0, j * sl:(j + 1) * sl, :] = jnp.concatenate(
            [perm_out_ref[c, segment_rows(j), :] for c in range(lane_tiles)], axis=1)


def _lru_layer(h, w_in, conv_w, conv_b, w_r, b_r, w_i, b_i, lam, kbd, vbd, w_out, ln_g, ln_b):
    b, s, _ = h.shape
    tm = TM_LRU
    n_gate_tiles = TOK_WIDTH // GATE_TILE
    return pl.pallas_call(
        _lru_layer_kernel,
        grid=(b, s // tm),
        in_specs=[
            pl.BlockSpec((1, tm, D_MODEL), lambda i, j: (i, j, 0)),
            _const_spec((D_MODEL, _B_WIDTH)),
            _const_spec((CONV_W, TOK_WIDTH)),
            _const_spec((1, TOK_WIDTH)),
            _const_spec((n_gate_tiles, GATE_TILE, GATE_TILE)),
            _const_spec((1, TOK_WIDTH)),
            _const_spec((n_gate_tiles, GATE_TILE, GATE_TILE)),
            _const_spec((1, TOK_WIDTH)),
            _const_spec((1, TOK_WIDTH)),
            pl.BlockSpec((1, 1, MEM_WIDTH, N_MEM_HEADS * MEM_LEN), lambda i, j: (1, i, 0, 0)),
            pl.BlockSpec((1, 1, N_MEM_HEADS * MEM_LEN, MEM_WIDTH), lambda i, j: (1, i, 0, 0)),
            _const_spec((MIX_WIDTH, D_MODEL)),
            _const_spec((1, D_MODEL)),
            _const_spec((1, D_MODEL)),
        ],
        out_specs=pl.BlockSpec((1, tm, D_MODEL), lambda i, j: (i, j, 0)),
        out_shape=jax.ShapeDtypeStruct((b, s, D_MODEL), F32),
        scratch_shapes=[
            pltpu.VMEM((tm + (CONV_W - 1) * SUBLANES, TOK_WIDTH), F32),
            pltpu.VMEM(((CONV_W - 1) * SUBLANES, TOK_WIDTH), F32),
            pltpu.VMEM((SUBLANES, TOK_WIDTH), F32),
            pltpu.VMEM((D_MODEL // LANES, tm, LANES), F32),
            pltpu.VMEM((D_MODEL // LANES, tm, LANES), F32),
        ],
        compiler_params=pltpu.CompilerParams(
            dimension_semantics=("arbitrary", "arbitrary"), vmem_limit_bytes=VMEM_LIMIT),
        name="lru_layer",
    )(h, w_in, conv_w, conv_b, w_r, b_r, w_i, b_i, lam, kbd, vbd, w_out, ln_g, ln_b)


def _block_diag_tiles(w):
    per_tile = GATE_TILE // HEAD_DIM
    w = w.reshape(TOK_WIDTH // GATE_TILE, per_tile, HEAD_DIM, HEAD_DIM)
    eye = jnp.eye(per_tile, dtype=w.dtype)
    t = w[:, :, :, None, :] * eye[None, :, None, :, None]
    return t.reshape(TOK_WIDTH // GATE_TILE, GATE_TILE, GATE_TILE)


def _rope_constants():
    inv_freq = ROPE_THETA ** (-np.arange(HALF_ROPE, dtype=np.float32) / np.float32(HALF_ROPE))
    c = np.zeros((SUBLANES, LANES), np.float32)
    c[0, QK_NOPE:QK_NOPE + HALF_ROPE] = inv_freq
    c[0, QK_NOPE + HALF_ROPE:QK_NOPE + QK_ROPE] = inv_freq
    c[1, QK_NOPE:QK_NOPE + HALF_ROPE] = -1.0
    c[2, QK_NOPE + HALF_ROPE:QK_NOPE + QK_ROPE] = 1.0
    return c


def kernel(x, mem, positions, mla_w_in, mla_q_norm, mla_w_uq, mla_kv_norm, mla_w_ukv, lru_w_in, lru_conv_w,
           lru_conv_b, lru_w_rgate, lru_b_rgate, lru_w_igate, lru_b_igate, lru_lambda, w_mem_kv, w_out, ln_g, ln_b):
    b, s, _ = x.shape
    assert s % TQ_ATTN == 0 and s % TM_LRU == 0

    kbd, vbd = _mem_kv(mem, w_mem_kv)

    w_in = mla_w_in[0]
    o_q, o_kv, o_kr, o_gate = Q_LORA, Q_LORA + KV_LORA, Q_LORA + KV_LORA + QK_ROPE, Q_LORA + KV_LORA + QK_ROPE + MIX_WIDTH
    w_kr = jnp.pad(w_in[:, o_kv:o_kr], ((0, 0), (QK_NOPE, LANES - QK_DIM)))
    w_in_a = jnp.concatenate(
        [w_in[:, :o_q], w_in[:, o_q:o_kv], w_kr, w_in[:, o_kr:o_gate], w_in[:, o_gate:]], axis=1).astype(BF16)
    w_uq = jnp.pad(mla_w_uq[0].reshape(Q_LORA, N_TOK_HEADS, QK_DIM), ((0, 0), (0, 0), (0, LANES - QK_DIM)))
    w_uq = w_uq.reshape(Q_LORA, N_TOK_HEADS * LANES).astype(BF16)
    w_ukv = mla_w_ukv[0].reshape(KV_LORA, N_TOK_HEADS, QK_NOPE + HEAD_DIM)
    w_k = jnp.pad(w_ukv[:, :, :QK_NOPE], ((0, 0), (0, 0), (0, LANES - QK_NOPE)))
    w_k = w_k.reshape(KV_LORA, N_TOK_HEADS * LANES).astype(BF16)
    w_vT = w_ukv[:, :, QK_NOPE:].reshape(KV_LORA, TOK_WIDTH).T.astype(BF16)
    pos = positions.astype(F32)[..., None]
    w_out_b = w_out.astype(BF16)

    q, k, vT, g, mg = _mla_proj(x, pos, w_in_a, mla_q_norm[0][None], mla_kv_norm[0][None], w_uq, w_k, w_vT,
                                jnp.asarray(_rope_constants()), kbd, vbd)
    tokT = _attention(q, k, vT)
    h1 = _out_ln(tokT, g, mg, x, w_out_b[0], ln_g[0][None], ln_b[0][None])

    return _lru_layer(h1, lru_w_in[0].astype(BF16), lru_conv_w[0], lru_conv_b[0][None],
                      _block_diag_tiles(lru_w_rgate[0]).astype(BF16), lru_b_rgate[0][None],
                      _block_diag_tiles(lru_w_igate[0]).astype(BF16), lru_b_igate[0][None],
                      lru_lambda[0][None], kbd, vbd, w_out_b[1], ln_g[1][None], ln_b[1][None])
```

```python
import functools
import math

import jax
import jax.numpy as jnp
import numpy as np
from jax import lax
from jax.experimental import pallas as pl
from jax.experimental.pallas import tpu as pltpu

F32 = jnp.float32
BF16 = jnp.bfloat16

D_MODEL = 1024
DEPTH = 2
MEM_LEN = 256
HEAD_DIM = 64
N_MEM_HEADS = 4
N_TOK_HEADS = 12
TOK_WIDTH = N_TOK_HEADS * HEAD_DIM
MEM_WIDTH = N_MEM_HEADS * HEAD_DIM
MIX_WIDTH = TOK_WIDTH + MEM_WIDTH
Q_LORA = 384
KV_LORA = 256
QK_NOPE = 64
QK_ROPE = 32
QK_DIM = QK_NOPE + QK_ROPE
ROPE_THETA = 10000.0
CONV_W = 4
LRU_C = 8.0
ALPHA = (2.0 * DEPTH) ** 0.25
NORM_EPS = 1e-6

MXU_N = 256
LANES = 128
SUBLANES = 8
HALF_ROPE = QK_ROPE // 2
LOG2E = math.log2(math.e)
Q_SCALE = LOG2E / math.sqrt(QK_DIM)

TM_PROJ = 512
TK_ATTN = TM_PROJ
TQ_ATTN = 2 * TK_ATTN
GW_ATTN = 256
TM_LRU = 1024
SUB_LRU = 256
VMEM_LIMIT = 48 * 1024 * 1024

_NT = (((1,), (1,)), ((), ()))


def _const_spec(shape):
    nd = len(shape)
    return pl.BlockSpec(shape, lambda *_: (0,) * nd, pipeline_mode=pl.Buffered(1))


def _dot_cols(x, w_ref, lo, hi, row_lo=None, row_hi=None):
    parts = [jnp.dot(x, w_ref[row_lo:row_hi, c:min(c + MXU_N, hi)], preferred_element_type=F32)
             for c in range(lo, hi, MXU_N)]
    return parts[0] if len(parts) == 1 else jnp.concatenate(parts, axis=-1)


def _sigmoid(t):
    return 0.5 * jnp.tanh(0.5 * t) + 0.5


def _silu(t):
    h = 0.5 * t
    return h * jnp.tanh(h) + h


def _rms_norm(t, g):
    return t * lax.rsqrt(jnp.mean(t * t, axis=-1, keepdims=True) + NORM_EPS) * g


def _layer_norm(t, g, b):
    mu = jnp.mean(t, axis=-1, keepdims=True)
    c = t - mu
    var = jnp.mean(c * c, axis=-1, keepdims=True)
    return c * lax.rsqrt(var + NORM_EPS) * g + b


def _memory_attention(qm, kbd_ref, vbd_ref):
    s = _dot_cols(qm.astype(BF16), kbd_ref, 0, N_MEM_HEADS * MEM_LEN)
    ps = []
    for h in range(N_MEM_HEADS):
        sh = s[:, h * MEM_LEN:(h + 1) * MEM_LEN]
        e = jnp.exp(sh - jnp.max(sh, axis=-1, keepdims=True))
        ps.append((e * (1.0 / jnp.sum(e, axis=-1, keepdims=True))).astype(BF16))
    p = jnp.concatenate(ps, axis=-1)
    return jnp.dot(p, vbd_ref[...], preferred_element_type=F32)


def _mem_kv_kernel(mem_ref, wkT_ref, wv_ref, kbd_ref, vbd_ref):
    mb = mem_ref[0].astype(BF16)
    kT = lax.dot_general(wkT_ref[0].astype(BF16), mb, _NT, preferred_element_type=F32)
    kT = (kT * (1.0 / math.sqrt(HEAD_DIM))).astype(BF16)
    v = jnp.dot(mb, wv_ref[0].astype(BF16), preferred_element_type=F32).astype(BF16)
    row = lax.broadcasted_iota(jnp.int32, (MEM_WIDTH, MEM_LEN), 0)
    col = lax.broadcasted_iota(jnp.int32, (MEM_LEN, MEM_WIDTH), 1)
    zero = jnp.zeros((), BF16)
    for h in range(N_MEM_HEADS):
        lo, hi = h * HEAD_DIM, (h + 1) * HEAD_DIM
        kbd_ref[0, 0, :, h * MEM_LEN:(h + 1) * MEM_LEN] = jnp.where((row >= lo) & (row < hi), kT, zero)
        vbd_ref[0, 0, h * MEM_LEN:(h + 1) * MEM_LEN, :] = jnp.where((col >= lo) & (col < hi), v, zero)


def _mem_kv(mem, w_mem_kv):
    b = mem.shape[0]
    wkT = jnp.swapaxes(w_mem_kv[:, :, :MEM_WIDTH], 1, 2)
    wv = w_mem_kv[:, :, MEM_WIDTH:]
    return pl.pallas_call(
        _mem_kv_kernel,
        grid=(DEPTH, b),
        in_specs=[
            pl.BlockSpec((1, MEM_LEN, D_MODEL), lambda l, i: (i, 0, 0)),
            pl.BlockSpec((1, MEM_WIDTH, D_MODEL), lambda l, i: (l, 0, 0)),
            pl.BlockSpec((1, D_MODEL, MEM_WIDTH), lambda l, i: (l, 0, 0)),
        ],
        out_specs=[
            pl.BlockSpec((1, 1, MEM_WIDTH, N_MEM_HEADS * MEM_LEN), lambda l, i: (l, i, 0, 0)),
            pl.BlockSpec((1, 1, N_MEM_HEADS * MEM_LEN, MEM_WIDTH), lambda l, i: (l, i, 0, 0)),
        ],
        out_shape=[
            jax.ShapeDtypeStruct((DEPTH, b, MEM_WIDTH, N_MEM_HEADS * MEM_LEN), BF16),
            jax.ShapeDtypeStruct((DEPTH, b, N_MEM_HEADS * MEM_LEN, MEM_WIDTH), BF16),
        ],
        name="mem_kv",
    )(mem, wkT, wv)


_A_CQ = (0, Q_LORA)
_A_CKV = (_A_CQ[1], _A_CQ[1] + KV_LORA)
_A_KR = (_A_CKV[1], _A_CKV[1] + LANES)
_A_GATE = (_A_KR[1], _A_KR[1] + MIX_WIDTH)
_A_QM = (_A_GATE[1], _A_GATE[1] + MEM_WIDTH)
_A_WIDTH = _A_QM[1]


def _mla_proj_kernel(x_ref, pos_ref, w_in_ref, qn_ref, kvn_ref, w_uq_ref, w_k_ref, w_vT_ref, rope_ref,
                     kbd_ref, vbd_ref, q_ref, k_ref, vT_ref, g_ref, mg_ref):
    xb = x_ref[0].astype(BF16)

    def proj(seg):
        return _dot_cols(xb, w_in_ref, seg[0], seg[1])

    ang = rope_ref[...] * pos_ref[0]
    cos_t, sin_t = jnp.cos(ang), jnp.sin(ang)
    gap = LANES // 2 - HALF_ROPE
    ones, zeros = jnp.ones((gap, ang.shape[1]), F32), jnp.zeros((gap, ang.shape[1]), F32)
    cos = jnp.concatenate([cos_t, ones, cos_t, ones], axis=0).T
    sin = jnp.concatenate([-sin_t, zeros, sin_t, zeros], axis=0).T

    def rope(t):
        return t * cos + pltpu.roll(t, LANES // 2, 1) * sin

    cqn = _rms_norm(proj(_A_CQ), qn_ref[...]).astype(BF16)
    q = _dot_cols(cqn, w_uq_ref, 0, N_TOK_HEADS * LANES)
    for h in range(N_TOK_HEADS):
        q_ref[0, h] = (rope(q[:, h * LANES:(h + 1) * LANES]) * Q_SCALE).astype(BF16)

    ckvn = _rms_norm(proj(_A_CKV), kvn_ref[...]).astype(BF16)
    k_rope = rope(proj(_A_KR))
    k = _dot_cols(ckvn, w_k_ref, 0, N_TOK_HEADS * LANES)
    for h in range(N_TOK_HEADS):
        k_ref[0, h] = (k[:, h * LANES:(h + 1) * LANES] + k_rope).astype(BF16)

    vT = jnp.concatenate(
        [lax.dot_general(w_vT_ref[...], ckvn[r:r + MXU_N, :], _NT, preferred_element_type=F32)
         for r in range(0, ckvn.shape[0], MXU_N)], axis=1)
    for h in range(N_TOK_HEADS):
        vT_ref[0, h, 0] = vT[h * HEAD_DIM:(h + 1) * HEAD_DIM, :].astype(BF16)

    g = _silu(proj(_A_GATE))
    g_ref[0] = g[:, :TOK_WIDTH]
    mg_ref[0] = _memory_attention(proj(_A_QM), kbd_ref.at[0, 0], vbd_ref.at[0, 0]) * g[:, TOK_WIDTH:]


def _mla_proj(x, pos, w_in_a, qn, kvn, w_uq, w_k, w_vT, rope_c, kbd, vbd):
    b, s, _ = x.shape
    tm = TM_PROJ
    nt = s // tm
    return pl.pallas_call(
        _mla_proj_kernel,
        grid=(b, nt),
        in_specs=[
            pl.BlockSpec((1, tm, D_MODEL), lambda i, j: (i, j, 0)),
            pl.BlockSpec((1, 1, tm), lambda i, j: (i, 0, j)),
            _const_spec((D_MODEL, _A_WIDTH)),
            _const_spec((1, Q_LORA)),
            _const_spec((1, KV_LORA)),
            _const_spec((Q_LORA, N_TOK_HEADS * LANES)),
            _const_spec((KV_LORA, N_TOK_HEADS * LANES)),
            _const_spec((TOK_WIDTH, KV_LORA)),
            _const_spec((HALF_ROPE, tm)),
            pl.BlockSpec((1, 1, MEM_WIDTH, N_MEM_HEADS * MEM_LEN), lambda i, j: (0, i, 0, 0)),
            pl.BlockSpec((1, 1, N_MEM_HEADS * MEM_LEN, MEM_WIDTH), lambda i, j: (0, i, 0, 0)),
        ],
        out_specs=[
            pl.BlockSpec((1, N_TOK_HEADS, tm, LANES), lambda i, j: (i, 0, j, 0)),
            pl.BlockSpec((1, N_TOK_HEADS, tm, LANES), lambda i, j: (i, 0, j, 0)),
            pl.BlockSpec((1, N_TOK_HEADS, 1, HEAD_DIM, tm), lambda i, j: (i, 0, j, 0, 0)),
            pl.BlockSpec((1, tm, TOK_WIDTH), lambda i, j: (i, j, 0)),
            pl.BlockSpec((1, tm, MEM_WIDTH), lambda i, j: (i, j, 0)),
        ],
        out_shape=[
            jax.ShapeDtypeStruct((b, N_TOK_HEADS, s, LANES), BF16),
            jax.ShapeDtypeStruct((b, N_TOK_HEADS, s, LANES), BF16),
            jax.ShapeDtypeStruct((b, N_TOK_HEADS, nt, HEAD_DIM, tm), BF16),
            jax.ShapeDtypeStruct((b, s, TOK_WIDTH), F32),
            jax.ShapeDtypeStruct((b, s, MEM_WIDTH), F32),
        ],
        compiler_params=pltpu.CompilerParams(
            dimension_semantics=("arbitrary", "arbitrary"), vmem_limit_bytes=VMEM_LIMIT),
        name="mla_proj",
    )(x, pos, w_in_a, qn, kvn, w_uq, w_k, w_vT, rope_c, kbd, vbd)


def _attn_kernel(q_ref, k_ref, vT_ref, o_ref, s_ref, bm_ref, m_ref, l_ref, acc_ref):
    tq, tk, gw = TQ_ATTN, TK_ATTN, GW_ATTN
    groups = range(tq // gw)
    qi = pl.program_id(2)
    m_ref[...] = jnp.full((1, tq), -jnp.inf, F32)
    l_ref[...] = jnp.zeros((1, tq), F32)
    acc_ref[...] = jnp.zeros((HEAD_DIM, tq), F32)

    def scores(chunk, g, slot, key_off=None):
        c0 = g * gw
        if key_off is not None and key_off > c0 + gw - 1:
            return
        start = pl.multiple_of(chunk * tk, tk)
        kt = k_ref[0, 0, pl.ds(start, tk), :]
        st = lax.dot_general(kt, q_ref[0, 0, c0:c0 + gw, :], _NT, preferred_element_type=F32)
        if key_off is not None and key_off + tk - 1 > c0:
            key = lax.broadcasted_iota(jnp.int32, st.shape, 0) + key_off
            qry = lax.broadcasted_iota(jnp.int32, st.shape, 1) + c0
            st = jnp.where(key <= qry, st, -jnp.inf)
        s_ref[slot, :, c0:c0 + gw] = st
        bm_ref[slot, :, c0:c0 + gw] = jnp.max(st, axis=0, keepdims=True)

    def update(chunk, g, slot):
        c0, c1 = g * gw, (g + 1) * gw
        m_old = m_ref[:, c0:c1]
        m_new = jnp.maximum(m_old, bm_ref[slot, :, c0:c1])
        alpha = jnp.exp2(m_old - m_new)
        p = jnp.exp2(s_ref[slot, :, c0:c1] - m_new)
        l_ref[:, c0:c1] = alpha * l_ref[:, c0:c1] + jnp.sum(p, axis=0, keepdims=True)
        m_ref[:, c0:c1] = m_new
        pv = jnp.dot(vT_ref[0, 0, chunk], p.astype(BF16), preferred_element_type=F32)
        acc_ref[:, c0:c1] = alpha * acc_ref[:, c0:c1] + pv

    def stage(nxt_chunk, nxt_key_off, cur_chunk, cur_slot, cur_groups=groups):
        for g in groups:
            if nxt_chunk is not None:
                scores(nxt_chunk, g, 1 - cur_slot, nxt_key_off)
            if g in cur_groups:
                update(cur_chunk, g, cur_slot)

    d = 2 * qi
    right = range(tk // gw, tq // gw)

    def diagonal_tail():
        stage(d + 1, tk, d, 0)
        stage(None, None, d + 1, 1, right)
        o_ref[0] = acc_ref[...] * (1.0 / l_ref[...])

    @pl.when(qi > 0)
    def _():
        for g in groups:
            scores(0, g, 0)

        def pair(p, _):
            c = 2 * p
            stage(c + 1, None, c, 0)
            stage(c + 2, None, c + 1, 1)
            return 0

        lax.fori_loop(0, qi - 1, pair, 0)
        stage(d - 1, None, d - 2, 0)
        stage(d, 0, d - 1, 1)
        diagonal_tail()

    @pl.when(qi == 0)
    def _():
        for g in groups:
            scores(d, g, 0, 0)
        diagonal_tail()


def _attention(q, k, vT):
    b, h, s, _ = q.shape
    tq, tk = TQ_ATTN, TK_ATTN
    nkv = vT.shape[2]
    return pl.pallas_call(
        _attn_kernel,
        grid=(b, h, s // tq),
        in_specs=[
            pl.BlockSpec((1, 1, tq, LANES), lambda i, j, t: (i, j, t, 0)),
            pl.BlockSpec((1, 1, s, LANES), lambda i, j, t: (i, j, 0, 0)),
            pl.BlockSpec((1, 1, nkv, HEAD_DIM, tk), lambda i, j, t: (i, j, 0, 0, 0)),
        ],
        out_specs=pl.BlockSpec((1, HEAD_DIM, tq), lambda i, j, t: (i, j, t)),
        out_shape=jax.ShapeDtypeStruct((b, h * HEAD_DIM, s), F32),
        scratch_shapes=[
            pltpu.VMEM((2, tk, tq), F32),
            pltpu.VMEM((2, 1, tq), F32),
            pltpu.VMEM((1, tq), F32),
            pltpu.VMEM((1, tq), F32),
            pltpu.VMEM((HEAD_DIM, tq), F32),
        ],
        compiler_params=pltpu.CompilerParams(
            dimension_semantics=("arbitrary", "arbitrary", "arbitrary"), vmem_limit_bytes=VMEM_LIMIT),
        name="mla_attn",
    )(q, k, vT)


def _out_ln_kernel(tokT_ref, g_ref, mg_ref, h_ref, w_out_ref, lng_ref, lnb_ref, o_ref):
    tok = tokT_ref[0].T
    y = (tok * g_ref[0]).astype(BF16)
    o = (_dot_cols(y, w_out_ref, 0, D_MODEL, 0, TOK_WIDTH)
         + _dot_cols(mg_ref[0].astype(BF16), w_out_ref, 0, D_MODEL, TOK_WIDTH, MIX_WIDTH))
    o_ref[0] = _layer_norm(ALPHA * h_ref[0] + o, lng_ref[...], lnb_ref[...])


def _out_ln(tokT, g, mg, h, w_out, ln_g, ln_b):
    b, s, _ = h.shape
    tm = TM_PROJ
    return pl.pallas_call(
        _out_ln_kernel,
        grid=(b, s // tm),
        in_specs=[
            pl.BlockSpec((1, TOK_WIDTH, tm), lambda i, j: (i, 0, j)),
            pl.BlockSpec((1, tm, TOK_WIDTH), lambda i, j: (i, j, 0)),
            pl.BlockSpec((1, tm, MEM_WIDTH), lambda i, j: (i, j, 0)),
            pl.BlockSpec((1, tm, D_MODEL), lambda i, j: (i, j, 0)),
            _const_spec((MIX_WIDTH, D_MODEL)),
            _const_spec((1, D_MODEL)),
            _const_spec((1, D_MODEL)),
        ],
        out_specs=pl.BlockSpec((1, tm, D_MODEL), lambda i, j: (i, j, 0)),
        out_shape=jax.ShapeDtypeStruct((b, s, D_MODEL), F32),
        compiler_params=pltpu.CompilerParams(
            dimension_semantics=("arbitrary", "arbitrary"), vmem_limit_bytes=VMEM_LIMIT),
        name="mla_out_ln",
    )(tokT, g, mg, h, w_out, ln_g, ln_b)


_B_U = (0, TOK_WIDTH)
_B_GATE = (_B_U[1], _B_U[1] + MIX_WIDTH)
_B_QM = (_B_GATE[1], _B_GATE[1] + MEM_WIDTH)
_B_WIDTH = _B_QM[1]
GATE_TILE = 256


def _linear_scan(a, b):
    n = a.shape[0]
    row = lax.broadcasted_iota(jnp.int32, a.shape, 0)
    d = 1
    while d < n:
        live = row >= d
        b = a * jnp.where(live, pltpu.roll(b, d, 0), 0.0) + b
        if 2 * d < n:
            a = a * jnp.where(live, pltpu.roll(a, d, 0), 1.0)
        d *= 2
    return b


def _emit_round_robin(*piece_lists):
    lists = [list(p) for p in piece_lists if p]
    longest = max((len(p) for p in lists), default=0)
    for k in range(longest):
        for p in lists:
            lo, hi = (k * len(p)) // longest, ((k + 1) * len(p)) // longest
            for piece in p[lo:hi]:
                piece()


def _lru_layer_kernel(h_ref, w_in_ref, conv_w_ref, conv_b_ref, w_r_ref, b_r_ref, w_i_ref, b_i_ref, lam_ref,
                      kbd_ref, vbd_ref, w_out_ref, lng_ref, lnb_ref, o_ref,
                      ebuf_ref, tail_ref, carry_ref, perm_in_ref, perm_out_ref):
    tm, sl, ct = SUB_LRU, SUBLANES, GATE_TILE
    n = tm // sl
    halo = (CONV_W - 1) * sl
    lane_tiles = D_MODEL // LANES
    n_sub = TM_LRU // SUB_LRU
    step = pl.program_id(1)

    @pl.when(step == 0)
    def _():
        tail_ref[...] = jnp.zeros((halo, TOK_WIDTH), F32)
        carry_ref[...] = jnp.zeros((sl, TOK_WIDTH), F32)

    sub8 = lax.broadcasted_iota(jnp.int32, (sl, ct), 0)

    def segment_rows(j):
        s, i0 = divmod(sl * j, n)
        return pl.ds(sl * i0 + s, sl, stride=sl)

    def from_previous_segment(prev_tile, cur_tile):
        return jnp.where(sub8 == 0, pltpu.roll(prev_tile, 1, 0), pltpu.roll(cur_tile, 1, 0))

    tails = [tail_ref[:, c * ct:(c + 1) * ct] for c in range(TOK_WIDTH // ct)]
    carries = [carry_ref[:, c * ct:(c + 1) * ct] for c in range(TOK_WIDTH // ct)]

    def sub_tile_phases(r):
        row0 = r * tm
        v = {}

        def load():
            for j in range(n):
                rows = h_ref[0, row0 + j * sl:row0 + (j + 1) * sl, :]
                for c in range(lane_tiles):
                    perm_in_ref[r, c, segment_rows(j), :] = rows[:, c * LANES:(c + 1) * LANES]
            v["hin"] = jnp.concatenate([perm_in_ref[r, c] for c in range(lane_tiles)], axis=1)
            v["hb"] = v["hin"].astype(BF16)

        def project(c):
            v["z", c] = jnp.dot(v["hb"], w_in_ref[:, c * MXU_N:(c + 1) * MXU_N], preferred_element_type=F32)

        def recurrence(c):
            cols = slice(c * ct, (c + 1) * ct)
            u = v["z", _B_U[0] // MXU_N + c]
            for k in range(CONV_W - 1):
                cur = u[tm - halo + k * sl:tm - halo + (k + 1) * sl, :]
                ebuf_ref[r, k * sl:(k + 1) * sl, cols] = from_previous_segment(tails[c][k * sl:(k + 1) * sl, :], cur)
            ebuf_ref[r, halo:halo + tm, cols] = u
            tails[c] = u[tm - halo:, :]
            xc = conv_b_ref[:, cols] + u * conv_w_ref[CONV_W - 1:CONV_W, cols]
            for back in range(1, CONV_W):
                tap = CONV_W - 1 - back
                xc = xc + ebuf_ref[r, halo - back * sl:halo - back * sl + tm, cols] * conv_w_ref[tap:tap + 1, cols]

            xcb = xc.astype(BF16)
            gate_r = _sigmoid(jnp.dot(xcb, w_r_ref[c], preferred_element_type=F32) + b_r_ref[:, cols])
            gate_i = _sigmoid(jnp.dot(xcb, w_i_ref[c], preferred_element_type=F32) + b_i_ref[:, cols])

            neg_lam = -lam_ref[:, cols]
            softplus = jnp.maximum(neg_lam, 0.0) + jnp.log1p(jnp.exp(-jnp.abs(neg_lam)))
            log_a = (-LRU_C * softplus) * gate_r
            a = jnp.exp(log_a)
            one_minus_a2 = -jnp.tanh(log_a) * (a * a + 1.0)
            root = jnp.where(one_minus_a2 > 0.0, one_minus_a2 * lax.rsqrt(one_minus_a2), 0.0)
            b = root * (gate_i * xc)

            h_loc, a_run = [b[0:sl, :]], [a[0:sl, :]]
            for g in range(1, n):
                ag = a[g * sl:(g + 1) * sl, :]
                h_loc.append(ag * h_loc[-1] + b[g * sl:(g + 1) * sl, :])
                a_run.append(ag * a_run[-1])
            state_in = pltpu.roll(carries[c], 1, 0)
            seg_b = jnp.where(sub8 == 0, a_run[-1] * state_in + h_loc[-1], h_loc[-1])
            seg_end = _linear_scan(a_run[-1], seg_b)
            start = from_previous_segment(carries[c], seg_end)
            carries[c] = seg_end
            v["hs", c] = jnp.concatenate([h_loc[g] + a_run[g] * start for g in range(n)], axis=0)

        def gate(c):
            g = _silu(v["z", _B_GATE[0] // MXU_N + c])
            if c < TOK_WIDTH // ct:
                v["y", c] = (v["hs", c] * g).astype(BF16)
            else:
                mem_out = _memory_attention(v["z", _B_QM[0] // MXU_N], kbd_ref.at[0, 0], vbd_ref.at[0, 0])
                v["y", c] = (mem_out * g).astype(BF16)

        def out_project(c):
            if c == 0:
                v["yb"] = jnp.concatenate([v["y", k] for k in range(MIX_WIDTH // ct)], axis=1)
            v["o", c] = jnp.dot(v["yb"], w_out_ref[:, c * MXU_N:(c + 1) * MXU_N], preferred_element_type=F32)

        def finish():
            o = jnp.concatenate([v["o", c] for c in range(D_MODEL // MXU_N)], axis=1)
            out = _layer_norm(ALPHA * v["hin"] + o, lng_ref[...], lnb_ref[...])
            for c in range(lane_tiles):
                perm_out_ref[r, c] = out[:, c * LANES:(c + 1) * LANES]
            for j in range(n):
                o_ref[0, row0 + j * sl:row0 + (j + 1) * sl, :] = jnp.concatenate(
                    [perm_out_ref[r, c, segment_rows(j), :] for c in range(lane_tiles)], axis=1)

        n_tok, n_mix = TOK_WIDTH // ct, MIX_WIDTH // ct
        phase_a = [load] + [functools.partial(project, c) for c in range(_B_WIDTH // MXU_N)]
        phase_b = []
        for c in range(n_mix):
            if c < n_tok:
                phase_b.append(functools.partial(recurrence, c))
            phase_b.append(functools.partial(gate, c))
        phase_cd = [functools.partial(out_project, c) for c in range(D_MODEL // MXU_N)] + [finish]
        return phase_a, phase_b, phase_cd

    phases = [sub_tile_phases(r) for r in range(n_sub)]
    _emit_round_robin(phases[0][0])
    for r in range(n_sub + 1):
        vector_heavy = phases[r][1] if r < n_sub else []
        matmul_heavy = (phases[r + 1][0] if r + 1 < n_sub else []) + (phases[r - 1][2] if r >= 1 else [])
        _emit_round_robin(matmul_heavy, vector_heavy)

    for c in range(TOK_WIDTH // ct):
        tail_ref[:, c * ct:(c + 1) * ct] = tails[c]
        carry_ref[:, c * ct:(c + 1) * ct] = carries[c]


def _lru_layer(h, w_in, conv_w, conv_b, w_r, b_r, w_i, b_i, lam, kbd, vbd, w_out, ln_g, ln_b):
    b, s, _ = h.shape
    tm = TM_LRU
    n_gate_tiles = TOK_WIDTH // GATE_TILE
    n_sub = TM_LRU // SUB_LRU
    return pl.pallas_call(
        _lru_layer_kernel,
        grid=(b, s // tm),
        in_specs=[
            pl.BlockSpec((1, tm, D_MODEL), lambda i, j: (i, j, 0)),
            _const_spec((D_MODEL, _B_WIDTH)),
            _const_spec((CONV_W, TOK_WIDTH)),
            _const_spec((1, TOK_WIDTH)),
            _const_spec((n_gate_tiles, GATE_TILE, GATE_TILE)),
            _const_spec((1, TOK_WIDTH)),
            _const_spec((n_gate_tiles, GATE_TILE, GATE_TILE)),
            _const_spec((1, TOK_WIDTH)),
            _const_spec((1, TOK_WIDTH)),
            pl.BlockSpec((1, 1, MEM_WIDTH, N_MEM_HEADS * MEM_LEN), lambda i, j: (1, i, 0, 0)),
            pl.BlockSpec((1, 1, N_MEM_HEADS * MEM_LEN, MEM_WIDTH), lambda i, j: (1, i, 0, 0)),
            _const_spec((MIX_WIDTH, D_MODEL)),
            _const_spec((1, D_MODEL)),
            _const_spec((1, D_MODEL)),
        ],
        out_specs=pl.BlockSpec((1, tm, D_MODEL), lambda i, j: (i, j, 0)),
        out_shape=jax.ShapeDtypeStruct((b, s, D_MODEL), F32),
        scratch_shapes=[
            pltpu.VMEM((n_sub, SUB_LRU + (CONV_W - 1) * SUBLANES, TOK_WIDTH), F32),
            pltpu.VMEM(((CONV_W - 1) * SUBLANES, TOK_WIDTH), F32),
            pltpu.VMEM((SUBLANES, TOK_WIDTH), F32),
            pltpu.VMEM((n_sub, D_MODEL // LANES, SUB_LRU, LANES), F32),
            pltpu.VMEM((n_sub, D_MODEL // LANES, SUB_LRU, LANES), F32),
        ],
        compiler_params=pltpu.CompilerParams(
            dimension_semantics=("arbitrary", "arbitrary"), vmem_limit_bytes=VMEM_LIMIT),
        name="lru_layer",
    )(h, w_in, conv_w, conv_b, w_r, b_r, w_i, b_i, lam, kbd, vbd, w_out, ln_g, ln_b)


def _block_diag_tiles(w):
    per_tile = GATE_TILE // HEAD_DIM
    w = w.reshape(TOK_WIDTH // GATE_TILE, per_tile, HEAD_DIM, HEAD_DIM)
    eye = jnp.eye(per_tile, dtype=w.dtype)
    t = w[:, :, :, None, :] * eye[None, :, None, :, None]
    return t.reshape(TOK_WIDTH // GATE_TILE, GATE_TILE, GATE_TILE)


def _rope_frequencies(tm):
    inv_freq = ROPE_THETA ** (-np.arange(HALF_ROPE, dtype=np.float32) / np.float32(HALF_ROPE))
    return np.ascontiguousarray(np.broadcast_to(inv_freq[:, None], (HALF_ROPE, tm)))


def _head_lanes(w, nope, t1, t2):
    zero = w.shape[-1]
    idx = np.full((LANES,), zero, np.int32)
    head = LANES // 2 - HALF_ROPE
    if t1 is not None:
        idx[:HALF_ROPE] = t1 + np.arange(HALF_ROPE)
        idx[LANES // 2:LANES // 2 + HALF_ROPE] = t2 + np.arange(HALF_ROPE)
    if nope is not None:
        idx[HALF_ROPE:LANES // 2] = nope + np.arange(head)
        idx[LANES // 2 + HALF_ROPE:LANES // 2 + HALF_ROPE + QK_NOPE - head] = nope + head + np.arange(QK_NOPE - head)
    w_ext = jnp.concatenate([w, jnp.zeros(w.shape[:-1] + (1,), w.dtype)], axis=-1)
    return jnp.take(w_ext, jnp.asarray(idx), axis=-1)


def kernel(x, mem, positions, mla_w_in, mla_q_norm, mla_w_uq, mla_kv_norm, mla_w_ukv, lru_w_in, lru_conv_w,
           lru_conv_b, lru_w_rgate, lru_b_rgate, lru_w_igate, lru_b_igate, lru_lambda, w_mem_kv, w_out, ln_g, ln_b):
    b, s, _ = x.shape
    assert s % TQ_ATTN == 0 and s % TM_LRU == 0

    kbd, vbd = _mem_kv(mem, w_mem_kv)

    w_in = mla_w_in[0]
    o_q, o_kv, o_kr, o_gate = Q_LORA, Q_LORA + KV_LORA, Q_LORA + KV_LORA + QK_ROPE, Q_LORA + KV_LORA + QK_ROPE + MIX_WIDTH
    w_kr = _head_lanes(w_in[:, o_kv:o_kr], None, 0, HALF_ROPE)
    w_in_a = jnp.concatenate(
        [w_in[:, :o_q], w_in[:, o_q:o_kv], w_kr, w_in[:, o_kr:o_gate], w_in[:, o_gate:]], axis=1).astype(BF16)
    w_uq = _head_lanes(mla_w_uq[0].reshape(Q_LORA, N_TOK_HEADS, QK_DIM), 0, QK_NOPE, QK_NOPE + HALF_ROPE)
    w_uq = w_uq.reshape(Q_LORA, N_TOK_HEADS * LANES).astype(BF16)
    w_ukv = mla_w_ukv[0].reshape(KV_LORA, N_TOK_HEADS, QK_NOPE + HEAD_DIM)
    w_k = _head_lanes(w_ukv[:, :, :QK_NOPE], 0, None, None)
    w_k = w_k.reshape(KV_LORA, N_TOK_HEADS * LANES).astype(BF16)
    w_vT = w_ukv[:, :, QK_NOPE:].reshape(KV_LORA, TOK_WIDTH).T.astype(BF16)
    pos = positions.astype(F32)[:, None, :]
    w_out_b = w_out.astype(BF16)

    q, k, vT, g, mg = _mla_proj(x, pos, w_in_a, mla_q_norm[0][None], mla_kv_norm[0][None], w_uq, w_k, w_vT,
                                jnp.asarray(_rope_frequencies(TM_PROJ)), kbd, vbd)
    tokT = _attention(q, k, vT)
    h1 = _out_ln(tokT, g, mg, x, w_out_b[0], ln_g[0][None], ln_b[0][None])

    return _lru_layer(h1, lru_w_in[0].astype(BF16), lru_conv_w[0], lru_conv_b[0][None],
                      _block_diag_tiles(lru_w_rgate[0]).astype(BF16), lru_b_rgate[0][None],
                      _block_diag_tiles(lru_w_igate[0]).astype(BF16), lru_b_igate[0][None],
                      lru_lambda[0][None], kbd, vbd, w_out_b[1], ln_g[1][None], ln_b[1][None])
```

```python
import functools
import math

import jax
import jax.numpy as jnp
import numpy as np
from jax import lax
from jax.experimental import pallas as pl
from jax.experimental.pallas import tpu as pltpu

F32 = jnp.float32
BF16 = jnp.bfloat16

D_MODEL = 1024
DEPTH = 2
MEM_LEN = 256
HEAD_DIM = 64
N_MEM_HEADS = 4
N_TOK_HEADS = 12
TOK_WIDTH = N_TOK_HEADS * HEAD_DIM
MEM_WIDTH = N_MEM_HEADS * HEAD_DIM
MIX_WIDTH = TOK_WIDTH + MEM_WIDTH
Q_LORA = 384
KV_LORA = 256
QK_NOPE = 64
QK_ROPE = 32
QK_DIM = QK_NOPE + QK_ROPE
ROPE_THETA = 10000.0
CONV_W = 4
LRU_C = 8.0
ALPHA = (2.0 * DEPTH) ** 0.25
NORM_EPS = 1e-6

MXU_N = 256
LANES = 128
SUBLANES = 8
HALF_ROPE = QK_ROPE // 2
LOG2E = math.log2(math.e)
Q_SCALE = LOG2E / math.sqrt(QK_DIM)

TM_PROJ = 512
TK_ATTN = 2 * TM_PROJ
TQ_ATTN = 2 * TK_ATTN
GW_ATTN = 256
TM_LRU = 1024
SUB_LRU = 256
VMEM_LIMIT = 48 * 1024 * 1024

_NT = (((1,), (1,)), ((), ()))


def _const_spec(shape):
    nd = len(shape)
    return pl.BlockSpec(shape, lambda *_: (0,) * nd, pipeline_mode=pl.Buffered(1))


def _dot_cols(x, w_ref, lo, hi, row_lo=None, row_hi=None):
    parts = [jnp.dot(x, w_ref[row_lo:row_hi, c:min(c + MXU_N, hi)], preferred_element_type=F32)
             for c in range(lo, hi, MXU_N)]
    return parts[0] if len(parts) == 1 else jnp.concatenate(parts, axis=-1)


def _sigmoid(t):
    return 0.5 * jnp.tanh(0.5 * t) + 0.5


def _silu(t):
    h = 0.5 * t
    return h * jnp.tanh(h) + h


def _rms_norm(t, g):
    return t * lax.rsqrt(jnp.mean(t * t, axis=-1, keepdims=True) + NORM_EPS) * g


def _layer_norm(t, g, b):
    mu = jnp.mean(t, axis=-1, keepdims=True)
    c = t - mu
    var = jnp.mean(c * c, axis=-1, keepdims=True)
    return c * lax.rsqrt(var + NORM_EPS) * g + b


def _memory_attention(qm, kbd_ref, vbd_ref):
    s = _dot_cols(qm.astype(BF16), kbd_ref, 0, N_MEM_HEADS * MEM_LEN)
    ps = []
    for h in range(N_MEM_HEADS):
        sh = s[:, h * MEM_LEN:(h + 1) * MEM_LEN]
        e = jnp.exp(sh - jnp.max(sh, axis=-1, keepdims=True))
        ps.append((e * (1.0 / jnp.sum(e, axis=-1, keepdims=True))).astype(BF16))
    p = jnp.concatenate(ps, axis=-1)
    return jnp.dot(p, vbd_ref[...], preferred_element_type=F32)


def _mem_kv_kernel(mem_ref, wkT_ref, wv_ref, kbd_ref, vbd_ref):
    mb = mem_ref[0].astype(BF16)
    kT = lax.dot_general(wkT_ref[0].astype(BF16), mb, _NT, preferred_element_type=F32)
    kT = (kT * (1.0 / math.sqrt(HEAD_DIM))).astype(BF16)
    v = jnp.dot(mb, wv_ref[0].astype(BF16), preferred_element_type=F32).astype(BF16)
    row = lax.broadcasted_iota(jnp.int32, (MEM_WIDTH, MEM_LEN), 0)
    col = lax.broadcasted_iota(jnp.int32, (MEM_LEN, MEM_WIDTH), 1)
    zero = jnp.zeros((), BF16)
    for h in range(N_MEM_HEADS):
        lo, hi = h * HEAD_DIM, (h + 1) * HEAD_DIM
        kbd_ref[0, 0, :, h * MEM_LEN:(h + 1) * MEM_LEN] = jnp.where((row >= lo) & (row < hi), kT, zero)
        vbd_ref[0, 0, h * MEM_LEN:(h + 1) * MEM_LEN, :] = jnp.where((col >= lo) & (col < hi), v, zero)


def _mem_kv(mem, w_mem_kv):
    b = mem.shape[0]
    wkT = jnp.swapaxes(w_mem_kv[:, :, :MEM_WIDTH], 1, 2)
    wv = w_mem_kv[:, :, MEM_WIDTH:]
    return pl.pallas_call(
        _mem_kv_kernel,
        grid=(DEPTH, b),
        in_specs=[
            pl.BlockSpec((1, MEM_LEN, D_MODEL), lambda l, i: (i, 0, 0)),
            pl.BlockSpec((1, MEM_WIDTH, D_MODEL), lambda l, i: (l, 0, 0)),
            pl.BlockSpec((1, D_MODEL, MEM_WIDTH), lambda l, i: (l, 0, 0)),
        ],
        out_specs=[
            pl.BlockSpec((1, 1, MEM_WIDTH, N_MEM_HEADS * MEM_LEN), lambda l, i: (l, i, 0, 0)),
            pl.BlockSpec((1, 1, N_MEM_HEADS * MEM_LEN, MEM_WIDTH), lambda l, i: (l, i, 0, 0)),
        ],
        out_shape=[
            jax.ShapeDtypeStruct((DEPTH, b, MEM_WIDTH, N_MEM_HEADS * MEM_LEN), BF16),
            jax.ShapeDtypeStruct((DEPTH, b, N_MEM_HEADS * MEM_LEN, MEM_WIDTH), BF16),
        ],
        name="mem_kv",
    )(mem, wkT, wv)


_A_CQ = (0, Q_LORA)
_A_CKV = (_A_CQ[1], _A_CQ[1] + KV_LORA)
_A_KR = (_A_CKV[1], _A_CKV[1] + LANES)
_A_GATE = (_A_KR[1], _A_KR[1] + MIX_WIDTH)
_A_QM = (_A_GATE[1], _A_GATE[1] + MEM_WIDTH)
_A_WIDTH = _A_QM[1]


def _mla_proj_kernel(x_ref, pos_ref, w_in_ref, qn_ref, kvn_ref, w_uq_ref, w_k_ref, w_vT_ref, rope_ref,
                     kbd_ref, vbd_ref, q_ref, k_ref, vT_ref, g_ref, mg_ref):
    xb = x_ref[0].astype(BF16)

    def proj(seg):
        return _dot_cols(xb, w_in_ref, seg[0], seg[1])

    ang = rope_ref[...] * pos_ref[0]
    cos_t, sin_t = jnp.cos(ang), jnp.sin(ang)
    gap = LANES // 2 - HALF_ROPE
    ones, zeros = jnp.ones((gap, ang.shape[1]), F32), jnp.zeros((gap, ang.shape[1]), F32)
    cos = jnp.concatenate([cos_t, ones, cos_t, ones], axis=0).T
    sin = jnp.concatenate([-sin_t, zeros, sin_t, zeros], axis=0).T

    def rope(t):
        return t * cos + pltpu.roll(t, LANES // 2, 1) * sin

    cqn = _rms_norm(proj(_A_CQ), qn_ref[...]).astype(BF16)
    q = _dot_cols(cqn, w_uq_ref, 0, N_TOK_HEADS * LANES)
    for h in range(N_TOK_HEADS):
        q_ref[0, h] = (rope(q[:, h * LANES:(h + 1) * LANES]) * Q_SCALE).astype(BF16)

    ckvn = _rms_norm(proj(_A_CKV), kvn_ref[...]).astype(BF16)
    k_rope = rope(proj(_A_KR))
    k = _dot_cols(ckvn, w_k_ref, 0, N_TOK_HEADS * LANES)
    for h in range(N_TOK_HEADS):
        k_ref[0, h] = (k[:, h * LANES:(h + 1) * LANES] + k_rope).astype(BF16)

    vT = jnp.concatenate(
        [lax.dot_general(w_vT_ref[...], ckvn[r:r + MXU_N, :], _NT, preferred_element_type=F32)
         for r in range(0, ckvn.shape[0], MXU_N)], axis=1)
    for h in range(N_TOK_HEADS):
        vT_ref[0, h, 0] = vT[h * HEAD_DIM:(h + 1) * HEAD_DIM, :].astype(BF16)

    g = _silu(proj(_A_GATE))
    g_ref[0] = g[:, :TOK_WIDTH].astype(g_ref.dtype)
    mem_out = _memory_attention(proj(_A_QM), kbd_ref.at[0, 0], vbd_ref.at[0, 0])
    mg_ref[0] = (mem_out * g[:, TOK_WIDTH:]).astype(mg_ref.dtype)


def _mla_proj(x, pos, w_in_a, qn, kvn, w_uq, w_k, w_vT, rope_c, kbd, vbd):
    b, s, _ = x.shape
    tm = TM_PROJ
    nt = s // tm
    return pl.pallas_call(
        _mla_proj_kernel,
        grid=(b, nt),
        in_specs=[
            pl.BlockSpec((1, tm, D_MODEL), lambda i, j: (i, j, 0)),
            pl.BlockSpec((1, 1, tm), lambda i, j: (i, 0, j)),
            _const_spec((D_MODEL, _A_WIDTH)),
            _const_spec((1, Q_LORA)),
            _const_spec((1, KV_LORA)),
            _const_spec((Q_LORA, N_TOK_HEADS * LANES)),
            _const_spec((KV_LORA, N_TOK_HEADS * LANES)),
            _const_spec((TOK_WIDTH, KV_LORA)),
            _const_spec((HALF_ROPE, tm)),
            pl.BlockSpec((1, 1, MEM_WIDTH, N_MEM_HEADS * MEM_LEN), lambda i, j: (0, i, 0, 0)),
            pl.BlockSpec((1, 1, N_MEM_HEADS * MEM_LEN, MEM_WIDTH), lambda i, j: (0, i, 0, 0)),
        ],
        out_specs=[
            pl.BlockSpec((1, N_TOK_HEADS, tm, LANES), lambda i, j: (i, 0, j, 0)),
            pl.BlockSpec((1, N_TOK_HEADS, tm, LANES), lambda i, j: (i, 0, j, 0)),
            pl.BlockSpec((1, N_TOK_HEADS, 1, HEAD_DIM, tm), lambda i, j: (i, 0, j, 0, 0)),
            pl.BlockSpec((1, tm, TOK_WIDTH), lambda i, j: (i, j, 0)),
            pl.BlockSpec((1, tm, MEM_WIDTH), lambda i, j: (i, j, 0)),
        ],
        out_shape=[
            jax.ShapeDtypeStruct((b, N_TOK_HEADS, s, LANES), BF16),
            jax.ShapeDtypeStruct((b, N_TOK_HEADS, s, LANES), BF16),
            jax.ShapeDtypeStruct((b, N_TOK_HEADS, nt, HEAD_DIM, tm), BF16),
            jax.ShapeDtypeStruct((b, s, TOK_WIDTH), BF16),
            jax.ShapeDtypeStruct((b, s, MEM_WIDTH), BF16),
        ],
        compiler_params=pltpu.CompilerParams(
            dimension_semantics=("arbitrary", "arbitrary"), vmem_limit_bytes=VMEM_LIMIT),
        name="mla_proj",
    )(x, pos, w_in_a, qn, kvn, w_uq, w_k, w_vT, rope_c, kbd, vbd)


def _attn_kernel(q_ref, k_ref, vT_ref, o_ref, s_ref, bm_ref, m_ref, l_ref, acc_ref):
    tq, tk, gw, tv = TQ_ATTN, TK_ATTN, GW_ATTN, TM_PROJ
    groups = range(tq // gw)
    qi = pl.program_id(2)
    m_ref[...] = jnp.full((1, tq), -jnp.inf, F32)
    l_ref[...] = jnp.zeros((1, tq), F32)
    acc_ref[...] = jnp.zeros((HEAD_DIM, tq), F32)

    def visible_rows(g, key_off):
        return tk if key_off is None else max(0, min(tk, (g + 1) * gw - key_off))

    def scores(chunk, g, slot, key_off=None):
        c0, rows = g * gw, visible_rows(g, key_off)
        if rows == 0:
            return
        start = pl.multiple_of(chunk * tk, tk)
        kt = k_ref[0, 0, pl.ds(start, rows), :]
        st = lax.dot_general(kt, q_ref[0, 0, c0:c0 + gw, :], _NT, preferred_element_type=F32)
        if key_off is not None and key_off + rows - 1 > c0:
            key = lax.broadcasted_iota(jnp.int32, st.shape, 0) + key_off
            qry = lax.broadcasted_iota(jnp.int32, st.shape, 1) + c0
            st = jnp.where(key <= qry, st, -jnp.inf)
        s_ref[slot, :rows, c0:c0 + gw] = st
        bm_ref[slot, :, c0:c0 + gw] = jnp.max(st, axis=0, keepdims=True)

    def update(chunk, g, slot, key_off=None):
        c0, c1, rows = g * gw, (g + 1) * gw, visible_rows(g, key_off)
        if rows == 0:
            return
        m_old = m_ref[:, c0:c1]
        m_new = jnp.maximum(m_old, bm_ref[slot, :, c0:c1])
        alpha = jnp.exp2(m_old - m_new)
        p = jnp.exp2(s_ref[slot, :rows, c0:c1] - m_new)
        l_ref[:, c0:c1] = alpha * l_ref[:, c0:c1] + jnp.sum(p, axis=0, keepdims=True)
        m_ref[:, c0:c1] = m_new
        pb = p.astype(BF16)
        pv = None
        for r0 in range(0, rows, tv):
            r1 = min(rows, r0 + tv)
            part = jnp.dot(vT_ref[0, 0, chunk * (tk // tv) + r0 // tv, :, :r1 - r0], pb[r0:r1, :],
                           preferred_element_type=F32)
            pv = part if pv is None else pv + part
        acc_ref[:, c0:c1] = alpha * acc_ref[:, c0:c1] + pv

    def stage(nxt_chunk, nxt_key_off, cur_chunk, cur_slot, cur_key_off=None):
        for g in groups:
            if nxt_chunk is not None:
                scores(nxt_chunk, g, 1 - cur_slot, nxt_key_off)
            update(cur_chunk, g, cur_slot, cur_key_off)

    d = 2 * qi

    def diagonal_tail():
        stage(d + 1, tk, d, 0, 0)
        stage(None, None, d + 1, 1, tk)
        o_ref[0] = (acc_ref[...] * (1.0 / l_ref[...])).astype(o_ref.dtype)

    @pl.when(qi > 0)
    def _():
        for g in groups:
            scores(0, g, 0)

        def pair(p, _):
            c = 2 * p
            stage(c + 1, None, c, 0)
            stage(c + 2, None, c + 1, 1)
            return 0

        lax.fori_loop(0, qi - 1, pair, 0)
        stage(d - 1, None, d - 2, 0)
        stage(d, 0, d - 1, 1)
        diagonal_tail()

    @pl.when(qi == 0)
    def _():
        for g in groups:
            scores(d, g, 0, 0)
        diagonal_tail()


def _attention(q, k, vT):
    b, h, s, _ = q.shape
    tq, tk = TQ_ATTN, TK_ATTN
    nkv = vT.shape[2]
    return pl.pallas_call(
        _attn_kernel,
        grid=(b, h, s // tq),
        in_specs=[
            pl.BlockSpec((1, 1, tq, LANES), lambda i, j, t: (i, j, t, 0)),
            pl.BlockSpec((1, 1, s, LANES), lambda i, j, t: (i, j, 0, 0)),
            pl.BlockSpec((1, 1, nkv, HEAD_DIM, TM_PROJ), lambda i, j, t: (i, j, 0, 0, 0)),
        ],
        out_specs=pl.BlockSpec((1, HEAD_DIM, tq), lambda i, j, t: (i, j, t)),
        out_shape=jax.ShapeDtypeStruct((b, h * HEAD_DIM, s), BF16),
        scratch_shapes=[
            pltpu.VMEM((2, tk, tq), F32),
            pltpu.VMEM((2, 1, tq), F32),
            pltpu.VMEM((1, tq), F32),
            pltpu.VMEM((1, tq), F32),
            pltpu.VMEM((HEAD_DIM, tq), F32),
        ],
        compiler_params=pltpu.CompilerParams(
            dimension_semantics=("arbitrary", "arbitrary", "arbitrary"), vmem_limit_bytes=VMEM_LIMIT),
        name="mla_attn",
    )(q, k, vT)


def _out_ln_kernel(tokT_ref, g_ref, mg_ref, h_ref, w_out_ref, lng_ref, lnb_ref, o_ref):
    tok = tokT_ref[0].astype(F32).T
    y = (tok * g_ref[0].astype(F32)).astype(BF16)
    o = (_dot_cols(y, w_out_ref, 0, D_MODEL, 0, TOK_WIDTH)
         + _dot_cols(mg_ref[0], w_out_ref, 0, D_MODEL, TOK_WIDTH, MIX_WIDTH))
    o_ref[0] = _layer_norm(ALPHA * h_ref[0] + o, lng_ref[...], lnb_ref[...])


def _out_ln(tokT, g, mg, h, w_out, ln_g, ln_b):
    b, s, _ = h.shape
    tm = TM_PROJ
    return pl.pallas_call(
        _out_ln_kernel,
        grid=(b, s // tm),
        in_specs=[
            pl.BlockSpec((1, TOK_WIDTH, tm), lambda i, j: (i, 0, j)),
            pl.BlockSpec((1, tm, TOK_WIDTH), lambda i, j: (i, j, 0)),
            pl.BlockSpec((1, tm, MEM_WIDTH), lambda i, j: (i, j, 0)),
            pl.BlockSpec((1, tm, D_MODEL), lambda i, j: (i, j, 0)),
            _const_spec((MIX_WIDTH, D_MODEL)),
            _const_spec((1, D_MODEL)),
            _const_spec((1, D_MODEL)),
        ],
        out_specs=pl.BlockSpec((1, tm, D_MODEL), lambda i, j: (i, j, 0)),
        out_shape=jax.ShapeDtypeStruct((b, s, D_MODEL), F32),
        compiler_params=pltpu.CompilerParams(
            dimension_semantics=("arbitrary", "arbitrary"), vmem_limit_bytes=VMEM_LIMIT),
        name="mla_out_ln",
    )(tokT, g, mg, h, w_out, ln_g, ln_b)


_B_U = (0, TOK_WIDTH)
_B_GATE = (_B_U[1], _B_U[1] + MIX_WIDTH)
_B_QM = (_B_GATE[1], _B_GATE[1] + MEM_WIDTH)
_B_WIDTH = _B_QM[1]
GATE_TILE = 256


def _linear_scan(a, b):
    n = a.shape[0]
    row = lax.broadcasted_iota(jnp.int32, a.shape, 0)
    d = 1
    while d < n:
        live = row >= d
        b = a * jnp.where(live, pltpu.roll(b, d, 0), 0.0) + b
        if 2 * d < n:
            a = a * jnp.where(live, pltpu.roll(a, d, 0), 1.0)
        d *= 2
    return b


def _emit_round_robin(*piece_lists):
    lists = [list(p) for p in piece_lists if p]
    longest = max((len(p) for p in lists), default=0)
    for k in range(longest):
        for p in lists:
            lo, hi = (k * len(p)) // longest, ((k + 1) * len(p)) // longest
            for piece in p[lo:hi]:
                piece()


def _lru_layer_kernel(h_ref, w_in_ref, conv_w_ref, conv_b_ref, w_r_ref, b_r_ref, w_i_ref, b_i_ref, lam_ref,
                      kbd_ref, vbd_ref, w_out_ref, lng_ref, lnb_ref, o_ref,
                      ebuf_ref, tail_ref, carry_ref, perm_in_ref, perm_out_ref):
    tm, sl, ct = SUB_LRU, SUBLANES, GATE_TILE
    n = tm // sl
    halo = (CONV_W - 1) * sl
    lane_tiles = D_MODEL // LANES
    n_sub = TM_LRU // SUB_LRU
    step = pl.program_id(1)

    @pl.when(step == 0)
    def _():
        tail_ref[...] = jnp.zeros((halo, TOK_WIDTH), F32)
        carry_ref[...] = jnp.zeros((sl, TOK_WIDTH), F32)

    sub8 = lax.broadcasted_iota(jnp.int32, (sl, ct), 0)

    def segment_rows(j):
        s, i0 = divmod(sl * j, n)
        return pl.ds(sl * i0 + s, sl, stride=sl)

    def from_previous_segment(prev_tile, cur_tile):
        return jnp.where(sub8 == 0, pltpu.roll(prev_tile, 1, 0), pltpu.roll(cur_tile, 1, 0))

    tails = [tail_ref[:, c * ct:(c + 1) * ct] for c in range(TOK_WIDTH // ct)]
    carries = [carry_ref[:, c * ct:(c + 1) * ct] for c in range(TOK_WIDTH // ct)]

    def sub_tile_phases(r):
        row0 = r * tm
        v = {}

        def load():
            for j in range(n):
                rows = h_ref[0, row0 + j * sl:row0 + (j + 1) * sl, :]
                for c in range(lane_tiles):
                    perm_in_ref[r, c, segment_rows(j), :] = rows[:, c * LANES:(c + 1) * LANES]
            v["hin"] = jnp.concatenate([perm_in_ref[r, c] for c in range(lane_tiles)], axis=1)
            v["hb"] = v["hin"].astype(BF16)

        def project(c):
            v["z", c] = jnp.dot(v["hb"], w_in_ref[:, c * MXU_N:(c + 1) * MXU_N], preferred_element_type=F32)

        def recurrence(c):
            cols = slice(c * ct, (c + 1) * ct)
            u = v["z", _B_U[0] // MXU_N + c]
            for k in range(CONV_W - 1):
                cur = u[tm - halo + k * sl:tm - halo + (k + 1) * sl, :]
                ebuf_ref[r, k * sl:(k + 1) * sl, cols] = from_previous_segment(tails[c][k * sl:(k + 1) * sl, :], cur)
            ebuf_ref[r, halo:halo + tm, cols] = u
            tails[c] = u[tm - halo:, :]
            xc = conv_b_ref[:, cols] + u * conv_w_ref[CONV_W - 1:CONV_W, cols]
            for back in range(1, CONV_W):
                tap = CONV_W - 1 - back
                xc = xc + ebuf_ref[r, halo - back * sl:halo - back * sl + tm, cols] * conv_w_ref[tap:tap + 1, cols]

            xcb = xc.astype(BF16)
            gate_r = _sigmoid(jnp.dot(xcb, w_r_ref[c], preferred_element_type=F32) + b_r_ref[:, cols])
            gate_i = _sigmoid(jnp.dot(xcb, w_i_ref[c], preferred_element_type=F32) + b_i_ref[:, cols])

            neg_lam = -lam_ref[:, cols]
            softplus = jnp.maximum(neg_lam, 0.0) + jnp.log1p(jnp.exp(-jnp.abs(neg_lam)))
            log_a = (-LRU_C * softplus) * gate_r
            a = jnp.exp(log_a)
            one_minus_a2 = -jnp.tanh(log_a) * (a * a + 1.0)
            root = jnp.where(one_minus_a2 > 0.0, one_minus_a2 * lax.rsqrt(one_minus_a2), 0.0)
            b = root * (gate_i * xc)

            h_loc, a_run = [b[0:sl, :]], [a[0:sl, :]]
            for g in range(1, n):
                ag = a[g * sl:(g + 1) * sl, :]
                h_loc.append(ag * h_loc[-1] + b[g * sl:(g + 1) * sl, :])
                a_run.append(ag * a_run[-1])
            state_in = pltpu.roll(carries[c], 1, 0)
            seg_b = jnp.where(sub8 == 0, a_run[-1] * state_in + h_loc[-1], h_loc[-1])
            seg_end = _linear_scan(a_run[-1], seg_b)
            start = from_previous_segment(carries[c], seg_end)
            carries[c] = seg_end
            v["hs", c] = jnp.concatenate([h_loc[g] + a_run[g] * start for g in range(n)], axis=0)

        def gate(c):
            g = _silu(v["z", _B_GATE[0] // MXU_N + c])
            if c < TOK_WIDTH // ct:
                v["y", c] = (v["hs", c] * g).astype(BF16)
            else:
                mem_out = _memory_attention(v["z", _B_QM[0] // MXU_N], kbd_ref.at[0, 0], vbd_ref.at[0, 0])
                v["y", c] = (mem_out * g).astype(BF16)

        def out_project(c):
            if c == 0:
                v["yb"] = jnp.concatenate([v["y", k] for k in range(MIX_WIDTH // ct)], axis=1)
            v["o", c] = jnp.dot(v["yb"], w_out_ref[:, c * MXU_N:(c + 1) * MXU_N], preferred_element_type=F32)

        def finish():
            o = jnp.concatenate([v["o", c] for c in range(D_MODEL // MXU_N)], axis=1)
            out = _layer_norm(ALPHA * v["hin"] + o, lng_ref[...], lnb_ref[...])
            for c in range(lane_tiles):
                perm_out_ref[r, c] = out[:, c * LANES:(c + 1) * LANES]
            for j in range(n):
                o_ref[0, row0 + j * sl:row0 + (j + 1) * sl, :] = jnp.concatenate(
                    [perm_out_ref[r, c, segment_rows(j), :] for c in range(lane_tiles)], axis=1)

        n_tok, n_mix = TOK_WIDTH // ct, MIX_WIDTH // ct
        phase_a = [load] + [functools.partial(project, c) for c in range(_B_WIDTH // MXU_N)]
        phase_b = []
        for c in range(n_mix):
            if c < n_tok:
                phase_b.append(functools.partial(recurrence, c))
            phase_b.append(functools.partial(gate, c))
        phase_cd = [functools.partial(out_project, c) for c in range(D_MODEL // MXU_N)] + [finish]
        return phase_a, phase_b, phase_cd

    phases = [sub_tile_phases(r) for r in range(n_sub)]
    _emit_round_robin(phases[0][0])
    for r in range(n_sub + 1):
        vector_heavy = phases[r][1] if r < n_sub else []
        matmul_heavy = (phases[r + 1][0] if r + 1 < n_sub else []) + (phases[r - 1][2] if r >= 1 else [])
        _emit_round_robin(matmul_heavy, vector_heavy)

    for c in range(TOK_WIDTH // ct):
        tail_ref[:, c * ct:(c + 1) * ct] = tails[c]
        carry_ref[:, c * ct:(c + 1) * ct] = carries[c]


def _lru_layer(h, w_in, conv_w, conv_b, w_r, b_r, w_i, b_i, lam, kbd, vbd, w_out, ln_g, ln_b):
    b, s, _ = h.shape
    tm = TM_LRU
    n_gate_tiles = TOK_WIDTH // GATE_TILE
    n_sub = TM_LRU // SUB_LRU
    return pl.pallas_call(
        _lru_layer_kernel,
        grid=(b, s // tm),
        in_specs=[
            pl.BlockSpec((1, tm, D_MODEL), lambda i, j: (i, j, 0)),
            _const_spec((D_MODEL, _B_WIDTH)),
            _const_spec((CONV_W, TOK_WIDTH)),
            _const_spec((1, TOK_WIDTH)),
            _const_spec((n_gate_tiles, GATE_TILE, GATE_TILE)),
            _const_spec((1, TOK_WIDTH)),
            _const_spec((n_gate_tiles, GATE_TILE, GATE_TILE)),
            _const_spec((1, TOK_WIDTH)),
            _const_spec((1, TOK_WIDTH)),
            pl.BlockSpec((1, 1, MEM_WIDTH, N_MEM_HEADS * MEM_LEN), lambda i, j: (1, i, 0, 0)),
            pl.BlockSpec((1, 1, N_MEM_HEADS * MEM_LEN, MEM_WIDTH), lambda i, j: (1, i, 0, 0)),
            _const_spec((MIX_WIDTH, D_MODEL)),
            _const_spec((1, D_MODEL)),
            _const_spec((1, D_MODEL)),
        ],
        out_specs=pl.BlockSpec((1, tm, D_MODEL), lambda i, j: (i, j, 0)),
        out_shape=jax.ShapeDtypeStruct((b, s, D_MODEL), F32),
        scratch_shapes=[
            pltpu.VMEM((n_sub, SUB_LRU + (CONV_W - 1) * SUBLANES, TOK_WIDTH), F32),
            pltpu.VMEM(((CONV_W - 1) * SUBLANES, TOK_WIDTH), F32),
            pltpu.VMEM((SUBLANES, TOK_WIDTH), F32),
            pltpu.VMEM((n_sub, D_MODEL // LANES, SUB_LRU, LANES), F32),
            pltpu.VMEM((n_sub, D_MODEL // LANES, SUB_LRU, LANES), F32),
        ],
        compiler_params=pltpu.CompilerParams(
            dimension_semantics=("arbitrary", "arbitrary"), vmem_limit_bytes=VMEM_LIMIT),
        name="lru_layer",
    )(h, w_in, conv_w, conv_b, w_r, b_r, w_i, b_i, lam, kbd, vbd, w_out, ln_g, ln_b)


def _block_diag_tiles(w):
    per_tile = GATE_TILE // HEAD_DIM
    w = w.reshape(TOK_WIDTH // GATE_TILE, per_tile, HEAD_DIM, HEAD_DIM)
    eye = jnp.eye(per_tile, dtype=w.dtype)
    t = w[:, :, :, None, :] * eye[None, :, None, :, None]
    return t.reshape(TOK_WIDTH // GATE_TILE, GATE_TILE, GATE_TILE)


def _rope_frequencies(tm):
    inv_freq = ROPE_THETA ** (-np.arange(HALF_ROPE, dtype=np.float32) / np.float32(HALF_ROPE))
    return np.ascontiguousarray(np.broadcast_to(inv_freq[:, None], (HALF_ROPE, tm)))


def _head_lanes(w, nope, t1, t2):
    zero = w.shape[-1]
    idx = np.full((LANES,), zero, np.int32)
    head = LANES // 2 - HALF_ROPE
    if t1 is not None:
        idx[:HALF_ROPE] = t1 + np.arange(HALF_ROPE)
        idx[LANES // 2:LANES // 2 + HALF_ROPE] = t2 + np.arange(HALF_ROPE)
    if nope is not None:
        idx[HALF_ROPE:LANES // 2] = nope + np.arange(head)
        idx[LANES // 2 + HALF_ROPE:LANES // 2 + HALF_ROPE + QK_NOPE - head] = nope + head + np.arange(QK_NOPE - head)
    w_ext = jnp.concatenate([w, jnp.zeros(w.shape[:-1] + (1,), w.dtype)], axis=-1)
    return jnp.take(w_ext, jnp.asarray(idx), axis=-1)


def kernel(x, mem, positions, mla_w_in, mla_q_norm, mla_w_uq, mla_kv_norm, mla_w_ukv, lru_w_in, lru_conv_w,
           lru_conv_b, lru_w_rgate, lru_b_rgate, lru_w_igate, lru_b_igate, lru_lambda, w_mem_kv, w_out, ln_g, ln_b):
    b, s, _ = x.shape
    assert s % TQ_ATTN == 0 and s % TM_LRU == 0

    kbd, vbd = _mem_kv(mem, w_mem_kv)

    w_in = mla_w_in[0]
    o_q, o_kv, o_kr, o_gate = Q_LORA, Q_LORA + KV_LORA, Q_LORA + KV_LORA + QK_ROPE, Q_LORA + KV_LORA + QK_ROPE + MIX_WIDTH
    w_kr = _head_lanes(w_in[:, o_kv:o_kr], None, 0, HALF_ROPE)
    w_in_a = jnp.concatenate(
        [w_in[:, :o_q], w_in[:, o_q:o_kv], w_kr, w_in[:, o_kr:o_gate], w_in[:, o_gate:]], axis=1).astype(BF16)
    w_uq = _head_lanes(mla_w_uq[0].reshape(Q_LORA, N_TOK_HEADS, QK_DIM), 0, QK_NOPE, QK_NOPE + HALF_ROPE)
    w_uq = w_uq.reshape(Q_LORA, N_TOK_HEADS * LANES).astype(BF16)
    w_ukv = mla_w_ukv[0].reshape(KV_LORA, N_TOK_HEADS, QK_NOPE + HEAD_DIM)
    w_k = _head_lanes(w_ukv[:, :, :QK_NOPE], 0, None, None)
    w_k = w_k.reshape(KV_LORA, N_TOK_HEADS * LANES).astype(BF16)
    w_vT = w_ukv[:, :, QK_NOPE:].reshape(KV_LORA, TOK_WIDTH).T.astype(BF16)
    pos = positions.astype(F32)[:, None, :]
    w_out_b = w_out.astype(BF16)

    q, k, vT, g, mg = _mla_proj(x, pos, w_in_a, mla_q_norm[0][None], mla_kv_norm[0][None], w_uq, w_k, w_vT,
                                jnp.asarray(_rope_frequencies(TM_PROJ)), kbd, vbd)
    tokT = _attention(q, k, vT)
    h1 = _out_ln(tokT, g, mg, x, w_out_b[0], ln_g[0][None], ln_b[0][None])

    return _lru_layer(h1, lru_w_in[0].astype(BF16), lru_conv_w[0], lru_conv_b[0][None],
                      _block_diag_tiles(lru_w_rgate[0]).astype(BF16), lru_b_rgate[0][None],
                      _block_diag_tiles(lru_w_igate[0]).astype(BF16), lru_b_igate[0][None],
                      lru_lambda[0][None], kbd, vbd, w_out_b[1], ln_g[1][None], ln_b[1][None])
```

```python
import functools
import math

import jax
import jax.numpy as jnp
import numpy as np
from jax import lax
from jax.experimental import pallas as pl
from jax.experimental.pallas import tpu as pltpu

F32 = jnp.float32
BF16 = jnp.bfloat16

D_MODEL = 1024
DEPTH = 2
MEM_LEN = 256
HEAD_DIM = 64
N_MEM_HEADS = 4
N_TOK_HEADS = 12
TOK_WIDTH = N_TOK_HEADS * HEAD_DIM
MEM_WIDTH = N_MEM_HEADS * HEAD_DIM
MIX_WIDTH = TOK_WIDTH + MEM_WIDTH
Q_LORA = 384
KV_LORA = 256
QK_NOPE = 64
QK_ROPE = 32
QK_DIM = QK_NOPE + QK_ROPE
ROPE_THETA = 10000.0
CONV_W = 4
LRU_C = 8.0
ALPHA = (2.0 * DEPTH) ** 0.25
NORM_EPS = 1e-6

MXU_N = 256
LANES = 128
SUBLANES = 8
HALF_ROPE = QK_ROPE // 2
LOG2E = math.log2(math.e)
Q_SCALE = LOG2E / math.sqrt(QK_DIM)

TM_PROJ = 1024
SUB_PROJ = 256
TK_ATTN = 1024
TQ_ATTN = 2 * TK_ATTN
GW_ATTN = 256
TM_OUT = 1024
SUB_OUT = 256
TM_LRU = 1024
SUB_LRU = 256
LRU_DOT_N = MXU_N
VMEM_LIMIT = 48 * 1024 * 1024

_NT = (((1,), (1,)), ((), ()))


def _const_spec(shape):
    nd = len(shape)
    return pl.BlockSpec(shape, lambda *_: (0,) * nd, pipeline_mode=pl.Buffered(1))


def _dot_cols(x, w_ref, lo, hi, row_lo=None, row_hi=None):
    parts = [jnp.dot(x, w_ref[row_lo:row_hi, c:min(c + MXU_N, hi)], preferred_element_type=F32)
             for c in range(lo, hi, MXU_N)]
    return parts[0] if len(parts) == 1 else jnp.concatenate(parts, axis=-1)


def _sigmoid(t):
    return 0.5 * jnp.tanh(0.5 * t) + 0.5


def _silu(t):
    h = 0.5 * t
    return h * jnp.tanh(h) + h


def _rms_norm(t, g):
    return t * lax.rsqrt(jnp.mean(t * t, axis=-1, keepdims=True) + NORM_EPS) * g


def _layer_norm(t, g, b):
    mu = jnp.mean(t, axis=-1, keepdims=True)
    c = t - mu
    var = jnp.mean(c * c, axis=-1, keepdims=True)
    return c * lax.rsqrt(var + NORM_EPS) * g + b


def _memory_attention(qm, kbd_ref, vbd_ref):
    s = _dot_cols(qm.astype(BF16), kbd_ref, 0, N_MEM_HEADS * MEM_LEN)
    ps = []
    for h in range(N_MEM_HEADS):
        sh = s[:, h * MEM_LEN:(h + 1) * MEM_LEN]
        e = jnp.exp(sh - jnp.max(sh, axis=-1, keepdims=True))
        ps.append((e * (1.0 / jnp.sum(e, axis=-1, keepdims=True))).astype(BF16))
    p = jnp.concatenate(ps, axis=-1)
    return jnp.dot(p, vbd_ref[...], preferred_element_type=F32)


def _mem_kv_kernel(mem_ref, wkT_ref, wv_ref, kbd_ref, vbd_ref):
    mb = mem_ref[0].astype(BF16)
    kT = lax.dot_general(wkT_ref[0].astype(BF16), mb, _NT, preferred_element_type=F32)
    kT = (kT * (1.0 / math.sqrt(HEAD_DIM))).astype(BF16)
    v = jnp.dot(mb, wv_ref[0].astype(BF16), preferred_element_type=F32).astype(BF16)
    row = lax.broadcasted_iota(jnp.int32, (MEM_WIDTH, MEM_LEN), 0)
    col = lax.broadcasted_iota(jnp.int32, (MEM_LEN, MEM_WIDTH), 1)
    zero = jnp.zeros((), BF16)
    for h in range(N_MEM_HEADS):
        lo, hi = h * HEAD_DIM, (h + 1) * HEAD_DIM
        kbd_ref[0, 0, :, h * MEM_LEN:(h + 1) * MEM_LEN] = jnp.where((row >= lo) & (row < hi), kT, zero)
        vbd_ref[0, 0, h * MEM_LEN:(h + 1) * MEM_LEN, :] = jnp.where((col >= lo) & (col < hi), v, zero)


def _mem_kv(mem, w_mem_kv):
    b = mem.shape[0]
    wkT = jnp.swapaxes(w_mem_kv[:, :, :MEM_WIDTH], 1, 2)
    wv = w_mem_kv[:, :, MEM_WIDTH:]
    return pl.pallas_call(
        _mem_kv_kernel,
        grid=(DEPTH, b),
        in_specs=[
            pl.BlockSpec((1, MEM_LEN, D_MODEL), lambda l, i: (i, 0, 0)),
            pl.BlockSpec((1, MEM_WIDTH, D_MODEL), lambda l, i: (l, 0, 0)),
            pl.BlockSpec((1, D_MODEL, MEM_WIDTH), lambda l, i: (l, 0, 0)),
        ],
        out_specs=[
            pl.BlockSpec((1, 1, MEM_WIDTH, N_MEM_HEADS * MEM_LEN), lambda l, i: (l, i, 0, 0)),
            pl.BlockSpec((1, 1, N_MEM_HEADS * MEM_LEN, MEM_WIDTH), lambda l, i: (l, i, 0, 0)),
        ],
        out_shape=[
            jax.ShapeDtypeStruct((DEPTH, b, MEM_WIDTH, N_MEM_HEADS * MEM_LEN), BF16),
            jax.ShapeDtypeStruct((DEPTH, b, N_MEM_HEADS * MEM_LEN, MEM_WIDTH), BF16),
        ],
        name="mem_kv",
    )(mem, wkT, wv)


_A_CQ = (0, Q_LORA)
_A_CKV = (_A_CQ[1], _A_CQ[1] + KV_LORA)
_A_KR = (_A_CKV[1], _A_CKV[1] + LANES)
_A_GATE = (_A_KR[1], _A_KR[1] + MIX_WIDTH)
_A_QM = (_A_GATE[1], _A_GATE[1] + MEM_WIDTH)
_A_WIDTH = _A_QM[1]


def _mla_proj_kernel(x_ref, pos_ref, w_in_ref, qn_ref, kvn_ref, w_uq_ref, w_k_ref, w_vT_ref, rope_ref,
                     kbd_ref, vbd_ref, q_ref, k_ref, vT_ref, g_ref, mg_ref):
    sub = SUB_PROJ
    n_sub = x_ref.shape[1] // sub
    heads_per_tile = MXU_N // LANES

    ang = rope_ref[...] * pos_ref[0]
    cos_t, sin_t = jnp.cos(ang), jnp.sin(ang)
    gap = LANES // 2 - HALF_ROPE
    ones, zeros = jnp.ones((gap, ang.shape[1]), F32), jnp.zeros((gap, ang.shape[1]), F32)
    cos_all = jnp.concatenate([cos_t, ones, cos_t, ones], axis=0).T
    sin_all = jnp.concatenate([-sin_t, zeros, sin_t, zeros], axis=0).T

    def sub_tile_pieces(r):
        rows = slice(r * sub, (r + 1) * sub)
        cos, sin = cos_all[rows, :], sin_all[rows, :]
        v = {}

        def rope(t):
            return t * cos + pltpu.roll(t, LANES // 2, 1) * sin

        def load():
            v["xb"] = x_ref[0, rows, :].astype(BF16)

        def project(c):
            v["z", c] = jnp.dot(v["xb"], w_in_ref[:, c * MXU_N:(c + 1) * MXU_N], preferred_element_type=F32)

        def seg(s):
            z = jnp.concatenate([v["z", c] for c in range(s[0] // MXU_N, (s[1] - 1) // MXU_N + 1)], axis=1)
            lo = s[0] - (s[0] // MXU_N) * MXU_N
            return z[:, lo:lo + s[1] - s[0]]

        def q_norm():
            v["cqn"] = _rms_norm(seg(_A_CQ), qn_ref[...]).astype(BF16)

        def q_heads(t):
            q = jnp.dot(v["cqn"], w_uq_ref[:, t * MXU_N:(t + 1) * MXU_N], preferred_element_type=F32)
            for i in range(heads_per_tile):
                q_ref[0, t * heads_per_tile + i, rows, :] = (
                    rope(q[:, i * LANES:(i + 1) * LANES]) * Q_SCALE).astype(BF16)

        def kv_norm():
            v["ckvn"] = _rms_norm(seg(_A_CKV), kvn_ref[...]).astype(BF16)
            v["k_rope"] = rope(seg(_A_KR))

        def k_heads(t):
            k = jnp.dot(v["ckvn"], w_k_ref[:, t * MXU_N:(t + 1) * MXU_N], preferred_element_type=F32)
            for i in range(heads_per_tile):
                k_ref[0, t * heads_per_tile + i, rows, :] = (k[:, i * LANES:(i + 1) * LANES] + v["k_rope"]).astype(BF16)

        def values():
            vT = lax.dot_general(w_vT_ref[...], v["ckvn"], _NT, preferred_element_type=F32)
            for h in range(N_TOK_HEADS):
                vT_ref[0, h, 0, :, rows] = vT[h * HEAD_DIM:(h + 1) * HEAD_DIM, :].astype(BF16)

        def gates():
            g = _silu(seg(_A_GATE))
            g_ref[0, rows, :] = g[:, :TOK_WIDTH].astype(g_ref.dtype)
            mem_out = _memory_attention(seg(_A_QM), kbd_ref.at[0, 0], vbd_ref.at[0, 0])
            mg_ref[0, rows, :] = (mem_out * g[:, TOK_WIDTH:]).astype(mg_ref.dtype)

        n_tiles = N_TOK_HEADS // heads_per_tile
        matmul_heavy = [load] + [functools.partial(project, c) for c in range(_A_WIDTH // MXU_N)]
        vector_heavy = ([q_norm] + [functools.partial(q_heads, t) for t in range(n_tiles)] + [kv_norm]
                        + [functools.partial(k_heads, t) for t in range(n_tiles)] + [values, gates])
        return matmul_heavy, vector_heavy

    pieces = [sub_tile_pieces(r) for r in range(n_sub)]
    _emit_round_robin(pieces[0][0])
    for r in range(n_sub):
        _emit_round_robin(pieces[r + 1][0] if r + 1 < n_sub else [], pieces[r][1])


def _mla_proj(x, pos, w_in_a, qn, kvn, w_uq, w_k, w_vT, rope_c, kbd, vbd):
    b, s, _ = x.shape
    tm = TM_PROJ
    nt = s // tm
    return pl.pallas_call(
        _mla_proj_kernel,
        grid=(b, nt),
        in_specs=[
            pl.BlockSpec((1, tm, D_MODEL), lambda i, j: (i, j, 0)),
            pl.BlockSpec((1, 1, tm), lambda i, j: (i, 0, j)),
            _const_spec((D_MODEL, _A_WIDTH)),
            _const_spec((1, Q_LORA)),
            _const_spec((1, KV_LORA)),
            _const_spec((Q_LORA, N_TOK_HEADS * LANES)),
            _const_spec((KV_LORA, N_TOK_HEADS * LANES)),
            _const_spec((TOK_WIDTH, KV_LORA)),
            _const_spec((HALF_ROPE, tm)),
            pl.BlockSpec((1, 1, MEM_WIDTH, N_MEM_HEADS * MEM_LEN), lambda i, j: (0, i, 0, 0)),
            pl.BlockSpec((1, 1, N_MEM_HEADS * MEM_LEN, MEM_WIDTH), lambda i, j: (0, i, 0, 0)),
        ],
        out_specs=[
            pl.BlockSpec((1, N_TOK_HEADS, tm, LANES), lambda i, j: (i, 0, j, 0)),
            pl.BlockSpec((1, N_TOK_HEADS, tm, LANES), lambda i, j: (i, 0, j, 0)),
            pl.BlockSpec((1, N_TOK_HEADS, 1, HEAD_DIM, tm), lambda i, j: (i, 0, j, 0, 0)),
            pl.BlockSpec((1, tm, TOK_WIDTH), lambda i, j: (i, j, 0)),
            pl.BlockSpec((1, tm, MEM_WIDTH), lambda i, j: (i, j, 0)),
        ],
        out_shape=[
            jax.ShapeDtypeStruct((b, N_TOK_HEADS, s, LANES), BF16),
            jax.ShapeDtypeStruct((b, N_TOK_HEADS, s, LANES), BF16),
            jax.ShapeDtypeStruct((b, N_TOK_HEADS, nt, HEAD_DIM, tm), BF16),
            jax.ShapeDtypeStruct((b, s, TOK_WIDTH), BF16),
            jax.ShapeDtypeStruct((b, s, MEM_WIDTH), BF16),
        ],
        compiler_params=pltpu.CompilerParams(
            dimension_semantics=("arbitrary", "arbitrary"), vmem_limit_bytes=VMEM_LIMIT),
        name="mla_proj",
    )(x, pos, w_in_a, qn, kvn, w_uq, w_k, w_vT, rope_c, kbd, vbd)


def _attn_kernel(q_ref, k_ref, vT_ref, o_ref, s_ref, bm_ref, m_ref, l_ref, acc_ref):
    tq, tk, gw, tv = TQ_ATTN, TK_ATTN, GW_ATTN, TM_PROJ
    groups = range(tq // gw)
    qi = pl.program_id(2)
    m_ref[...] = jnp.full((1, tq), -jnp.inf, F32)
    l_ref[...] = jnp.zeros((1, tq), F32)
    acc_ref[...] = jnp.zeros((HEAD_DIM, tq), F32)

    def visible_rows(g, key_off):
        return tk if key_off is None else max(0, min(tk, (g + 1) * gw - key_off))

    def scores(chunk, g, slot, key_off=None):
        c0, rows = g * gw, visible_rows(g, key_off)
        if rows == 0:
            return
        start = pl.multiple_of(chunk * tk, tk)
        kt = k_ref[0, 0, pl.ds(start, rows), :]
        st = lax.dot_general(kt, q_ref[0, 0, c0:c0 + gw, :], _NT, preferred_element_type=F32)
        if key_off is not None and key_off + rows - 1 > c0:
            key = lax.broadcasted_iota(jnp.int32, st.shape, 0) + key_off
            qry = lax.broadcasted_iota(jnp.int32, st.shape, 1) + c0
            st = jnp.where(key <= qry, st, -jnp.inf)
        s_ref[slot, :rows, c0:c0 + gw] = st
        bm_ref[slot, :, c0:c0 + gw] = jnp.max(st, axis=0, keepdims=True)

    def update(chunk, g, slot, key_off=None):
        c0, c1, rows = g * gw, (g + 1) * gw, visible_rows(g, key_off)
        if rows == 0:
            return
        m_old = m_ref[:, c0:c1]
        m_new = jnp.maximum(m_old, bm_ref[slot, :, c0:c1])
        alpha = jnp.exp2(m_old - m_new)
        p = jnp.exp2(s_ref[slot, :rows, c0:c1] - m_new)
        l_ref[:, c0:c1] = alpha * l_ref[:, c0:c1] + jnp.sum(p, axis=0, keepdims=True)
        m_ref[:, c0:c1] = m_new
        pb = p.astype(BF16)
        pv = None
        for r0 in range(0, rows, tv):
            r1 = min(rows, r0 + tv)
            part = jnp.dot(vT_ref[0, 0, chunk * (tk // tv) + r0 // tv, :, :r1 - r0], pb[r0:r1, :],
                           preferred_element_type=F32)
            pv = part if pv is None else pv + part
        acc_ref[:, c0:c1] = alpha * acc_ref[:, c0:c1] + pv

    def stage(nxt_chunk, nxt_key_off, cur_chunk, cur_slot, cur_key_off=None):
        for g in groups:
            if nxt_chunk is not None:
                scores(nxt_chunk, g, 1 - cur_slot, nxt_key_off)
            update(cur_chunk, g, cur_slot, cur_key_off)

    d = 2 * qi

    def diagonal_tail():
        stage(d + 1, tk, d, 0, 0)
        stage(None, None, d + 1, 1, tk)
        o_ref[0] = (acc_ref[...] * (1.0 / l_ref[...])).astype(o_ref.dtype)

    @pl.when(qi > 0)
    def _():
        for g in groups:
            scores(0, g, 0)

        def pair(p, _):
            c = 2 * p
            stage(c + 1, None, c, 0)
            stage(c + 2, None, c + 1, 1)
            return 0

        lax.fori_loop(0, qi - 1, pair, 0)
        stage(d - 1, None, d - 2, 0)
        stage(d, 0, d - 1, 1)
        diagonal_tail()

    @pl.when(qi == 0)
    def _():
        for g in groups:
            scores(d, g, 0, 0)
        diagonal_tail()


def _attention(q, k, vT):
    b, h, s, _ = q.shape
    tq, tk = TQ_ATTN, TK_ATTN
    nkv = vT.shape[2]
    return pl.pallas_call(
        _attn_kernel,
        grid=(b, h, s // tq),
        in_specs=[
            pl.BlockSpec((1, 1, tq, LANES), lambda i, j, t: (i, j, t, 0)),
            pl.BlockSpec((1, 1, s, LANES), lambda i, j, t: (i, j, 0, 0)),
            pl.BlockSpec((1, 1, nkv, HEAD_DIM, TM_PROJ), lambda i, j, t: (i, j, 0, 0, 0)),
        ],
        out_specs=pl.BlockSpec((1, HEAD_DIM, tq), lambda i, j, t: (i, j, t)),
        out_shape=jax.ShapeDtypeStruct((b, h * HEAD_DIM, s), BF16),
        scratch_shapes=[
            pltpu.VMEM((2, tk, tq), F32),
            pltpu.VMEM((2, 1, tq), F32),
            pltpu.VMEM((1, tq), F32),
            pltpu.VMEM((1, tq), F32),
            pltpu.VMEM((HEAD_DIM, tq), F32),
        ],
        compiler_params=pltpu.CompilerParams(
            dimension_semantics=("arbitrary", "arbitrary", "arbitrary"), vmem_limit_bytes=VMEM_LIMIT),
        name="mla_attn",
    )(q, k, vT)


def _out_ln_kernel(tokT_ref, g_ref, mg_ref, h_ref, w_out_ref, lng_ref, lnb_ref, o_ref):
    sub = SUB_OUT
    n_sub = o_ref.shape[1] // sub
    v = {}

    def gate(r):
        rows = slice(r * sub, (r + 1) * sub)
        tok = tokT_ref[0, :, rows].astype(F32).T
        v["y", r] = (tok * g_ref[0, rows, :].astype(F32)).astype(BF16)

    def project(r, c):
        cols = slice(c * MXU_N, (c + 1) * MXU_N)
        v["o", r, c] = (
            jnp.dot(v["y", r], w_out_ref[:TOK_WIDTH, cols], preferred_element_type=F32)
            + jnp.dot(mg_ref[0, r * sub:(r + 1) * sub, :], w_out_ref[TOK_WIDTH:, cols], preferred_element_type=F32))

    def finish(r):
        rows = slice(r * sub, (r + 1) * sub)
        o = jnp.concatenate([v["o", r, c] for c in range(D_MODEL // MXU_N)], axis=1)
        o_ref[0, rows, :] = _layer_norm(ALPHA * h_ref[0, rows, :] + o, lng_ref[...], lnb_ref[...])

    def matmul_pieces(r):
        return [functools.partial(gate, r)] + [functools.partial(project, r, c) for c in range(D_MODEL // MXU_N)]

    _emit_round_robin(matmul_pieces(0))
    for r in range(n_sub):
        _emit_round_robin(matmul_pieces(r + 1) if r + 1 < n_sub else [], [functools.partial(finish, r)])


def _out_ln(tokT, g, mg, h, w_out, ln_g, ln_b):
    b, s, _ = h.shape
    tm = TM_OUT
    return pl.pallas_call(
        _out_ln_kernel,
        grid=(b, s // tm),
        in_specs=[
            pl.BlockSpec((1, TOK_WIDTH, tm), lambda i, j: (i, 0, j)),
            pl.BlockSpec((1, tm, TOK_WIDTH), lambda i, j: (i, j, 0)),
            pl.BlockSpec((1, tm, MEM_WIDTH), lambda i, j: (i, j, 0)),
            pl.BlockSpec((1, tm, D_MODEL), lambda i, j: (i, j, 0)),
            _const_spec((MIX_WIDTH, D_MODEL)),
            _const_spec((1, D_MODEL)),
            _const_spec((1, D_MODEL)),
        ],
        out_specs=pl.BlockSpec((1, tm, D_MODEL), lambda i, j: (i, j, 0)),
        out_shape=jax.ShapeDtypeStruct((b, s, D_MODEL), F32),
        compiler_params=pltpu.CompilerParams(
            dimension_semantics=("arbitrary", "arbitrary"), vmem_limit_bytes=VMEM_LIMIT),
        name="mla_out_ln",
    )(tokT, g, mg, h, w_out, ln_g, ln_b)


_B_U = (0, TOK_WIDTH)
_B_GATE = (_B_U[1], _B_U[1] + MIX_WIDTH)
_B_QM = (_B_GATE[1], _B_GATE[1] + MEM_WIDTH)
_B_WIDTH = _B_QM[1]
GATE_TILE = 256


def _linear_scan(a, b):
    n = a.shape[0]
    row = lax.broadcasted_iota(jnp.int32, a.shape, 0)
    d = 1
    while d < n:
        live = row >= d
        b = a * jnp.where(live, pltpu.roll(b, d, 0), 0.0) + b
        if 2 * d < n:
            a = a * jnp.where(live, pltpu.roll(a, d, 0), 1.0)
        d *= 2
    return b


def _emit_round_robin(*piece_lists):
    lists = [list(p) for p in piece_lists if p]
    longest = max((len(p) for p in lists), default=0)
    for k in range(longest):
        for p in lists:
            lo, hi = (k * len(p)) // longest, ((k + 1) * len(p)) // longest
            for piece in p[lo:hi]:
                piece()


def _lru_layer_kernel(h_ref, w_in_ref, conv_w_ref, conv_b_ref, w_r_ref, b_r_ref, w_i_ref, b_i_ref, lam_ref,
                      kbd_ref, vbd_ref, w_out_ref, lng_ref, lnb_ref, o_ref,
                      ebuf_ref, tail_ref, carry_ref, perm_in_ref, perm_out_ref):
    tm, sl, ct = SUB_LRU, SUBLANES, GATE_TILE
    n = tm // sl
    halo = (CONV_W - 1) * sl
    lane_tiles = D_MODEL // LANES
    n_sub = TM_LRU // SUB_LRU
    step = pl.program_id(1)

    @pl.when(step == 0)
    def _():
        tail_ref[...] = jnp.zeros((halo, TOK_WIDTH), F32)
        carry_ref[...] = jnp.zeros((sl, TOK_WIDTH), F32)

    sub8 = lax.broadcasted_iota(jnp.int32, (sl, ct), 0)

    def segment_rows(j):
        s, i0 = divmod(sl * j, n)
        return pl.ds(sl * i0 + s, sl, stride=sl)

    def from_previous_segment(prev_tile, cur_tile):
        return jnp.where(sub8 == 0, pltpu.roll(prev_tile, 1, 0), pltpu.roll(cur_tile, 1, 0))

    tails = [tail_ref[:, c * ct:(c + 1) * ct] for c in range(TOK_WIDTH // ct)]
    carries = [carry_ref[:, c * ct:(c + 1) * ct] for c in range(TOK_WIDTH // ct)]

    def sub_tile_phases(r):
        row0 = r * tm
        v = {}

        def load():
            for j in range(n):
                rows = h_ref[0, row0 + j * sl:row0 + (j + 1) * sl, :]
                for c in range(lane_tiles):
                    perm_in_ref[r, c, segment_rows(j), :] = rows[:, c * LANES:(c + 1) * LANES]
            v["hin"] = jnp.concatenate([perm_in_ref[r, c] for c in range(lane_tiles)], axis=1)
            v["hb"] = v["hin"].astype(BF16)

        def project(k):
            z = jnp.dot(v["hb"], w_in_ref[:, k * LRU_DOT_N:(k + 1) * LRU_DOT_N], preferred_element_type=F32)
            for c in range(LRU_DOT_N // MXU_N):
                v["z", k * (LRU_DOT_N // MXU_N) + c] = z[:, c * MXU_N:(c + 1) * MXU_N]

        def recurrence(c):
            cols = slice(c * ct, (c + 1) * ct)
            u = v["z", _B_U[0] // MXU_N + c]
            for k in range(CONV_W - 1):
                cur = u[tm - halo + k * sl:tm - halo + (k + 1) * sl, :]
                ebuf_ref[r, k * sl:(k + 1) * sl, cols] = from_previous_segment(tails[c][k * sl:(k + 1) * sl, :], cur)
            ebuf_ref[r, halo:halo + tm, cols] = u
            tails[c] = u[tm - halo:, :]
            xc = conv_b_ref[:, cols] + u * conv_w_ref[CONV_W - 1:CONV_W, cols]
            for back in range(1, CONV_W):
                tap = CONV_W - 1 - back
                xc = xc + ebuf_ref[r, halo - back * sl:halo - back * sl + tm, cols] * conv_w_ref[tap:tap + 1, cols]

            xcb = xc.astype(BF16)
            gate_r = _sigmoid(jnp.dot(xcb, w_r_ref[c], preferred_element_type=F32) + b_r_ref[:, cols])
            gate_i = _sigmoid(jnp.dot(xcb, w_i_ref[c], preferred_element_type=F32) + b_i_ref[:, cols])

            neg_lam = -lam_ref[:, cols]
            softplus = jnp.maximum(neg_lam, 0.0) + jnp.log1p(jnp.exp(-jnp.abs(neg_lam)))
            log_a = (-LRU_C * softplus) * gate_r
            a = jnp.exp(log_a)
            one_minus_a2 = -jnp.tanh(log_a) * (a * a + 1.0)
            root = jnp.where(one_minus_a2 > 0.0, one_minus_a2 * lax.rsqrt(one_minus_a2), 0.0)
            b = root * (gate_i * xc)

            h_loc, a_run = [b[0:sl, :]], [a[0:sl, :]]
            for g in range(1, n):
                ag = a[g * sl:(g + 1) * sl, :]
                h_loc.append(ag * h_loc[-1] + b[g * sl:(g + 1) * sl, :])
                a_run.append(ag * a_run[-1])
            state_in = pltpu.roll(carries[c], 1, 0)
            seg_b = jnp.where(sub8 == 0, a_run[-1] * state_in + h_loc[-1], h_loc[-1])
            seg_end = _linear_scan(a_run[-1], seg_b)
            start = from_previous_segment(carries[c], seg_end)
            carries[c] = seg_end
            v["hs", c] = jnp.concatenate([h_loc[g] + a_run[g] * start for g in range(n)], axis=0)

        def gate(c):
            g = _silu(v["z", _B_GATE[0] // MXU_N + c])
            if c < TOK_WIDTH // ct:
                v["y", c] = (v["hs", c] * g).astype(BF16)
            else:
                mem_out = _memory_attention(v["z", _B_QM[0] // MXU_N], kbd_ref.at[0, 0], vbd_ref.at[0, 0])
                v["y", c] = (mem_out * g).astype(BF16)

        def out_project(k):
            if k == 0:
                v["yb"] = jnp.concatenate([v["y", c] for c in range(MIX_WIDTH // ct)], axis=1)
            v["o", k] = jnp.dot(v["yb"], w_out_ref[:, k * LRU_DOT_N:(k + 1) * LRU_DOT_N], preferred_element_type=F32)

        def finish():
            o = jnp.concatenate([v["o", k] for k in range(D_MODEL // LRU_DOT_N)], axis=1)
            out = _layer_norm(ALPHA * v["hin"] + o, lng_ref[...], lnb_ref[...])
            for c in range(lane_tiles):
                perm_out_ref[r, c] = out[:, c * LANES:(c + 1) * LANES]
            for j in range(n):
                o_ref[0, row0 + j * sl:row0 + (j + 1) * sl, :] = jnp.concatenate(
                    [perm_out_ref[r, c, segment_rows(j), :] for c in range(lane_tiles)], axis=1)

        n_tok, n_mix = TOK_WIDTH // ct, MIX_WIDTH // ct
        phase_a = [load] + [functools.partial(project, k) for k in range(_B_WIDTH // LRU_DOT_N)]
        phase_b = []
        for c in range(n_mix):
            if c < n_tok:
                phase_b.append(functools.partial(recurrence, c))
            phase_b.append(functools.partial(gate, c))
        phase_cd = [functools.partial(out_project, k) for k in range(D_MODEL // LRU_DOT_N)] + [finish]
        return phase_a, phase_b, phase_cd

    phases = [sub_tile_phases(r) for r in range(n_sub)]
    _emit_round_robin(phases[0][0])
    for r in range(n_sub + 1):
        vector_heavy = phases[r][1] if r < n_sub else []
        matmul_heavy = (phases[r + 1][0] if r + 1 < n_sub else []) + (phases[r - 1][2] if r >= 1 else [])
        _emit_round_robin(matmul_heavy, vector_heavy)

    for c in range(TOK_WIDTH // ct):
        tail_ref[:, c * ct:(c + 1) * ct] = tails[c]
        carry_ref[:, c * ct:(c + 1) * ct] = carries[c]


def _lru_layer(h, w_in, conv_w, conv_b, w_r, b_r, w_i, b_i, lam, kbd, vbd, w_out, ln_g, ln_b):
    b, s, _ = h.shape
    tm = TM_LRU
    n_gate_tiles = TOK_WIDTH // GATE_TILE
    n_sub = TM_LRU // SUB_LRU
    return pl.pallas_call(
        _lru_layer_kernel,
        grid=(b, s // tm),
        in_specs=[
            pl.BlockSpec((1, tm, D_MODEL), lambda i, j: (i, j, 0)),
            _const_spec((D_MODEL, _B_WIDTH)),
            _const_spec((CONV_W, TOK_WIDTH)),
            _const_spec((1, TOK_WIDTH)),
            _const_spec((n_gate_tiles, GATE_TILE, GATE_TILE)),
            _const_spec((1, TOK_WIDTH)),
            _const_spec((n_gate_tiles, GATE_TILE, GATE_TILE)),
            _const_spec((1, TOK_WIDTH)),
            _const_spec((1, TOK_WIDTH)),
            pl.BlockSpec((1, 1, MEM_WIDTH, N_MEM_HEADS * MEM_LEN), lambda i, j: (1, i, 0, 0)),
            pl.BlockSpec((1, 1, N_MEM_HEADS * MEM_LEN, MEM_WIDTH), lambda i, j: (1, i, 0, 0)),
            _const_spec((MIX_WIDTH, D_MODEL)),
            _const_spec((1, D_MODEL)),
            _const_spec((1, D_MODEL)),
        ],
        out_specs=pl.BlockSpec((1, tm, D_MODEL), lambda i, j: (i, j, 0)),
        out_shape=jax.ShapeDtypeStruct((b, s, D_MODEL), F32),
        scratch_shapes=[
            pltpu.VMEM((n_sub, SUB_LRU + (CONV_W - 1) * SUBLANES, TOK_WIDTH), F32),
            pltpu.VMEM(((CONV_W - 1) * SUBLANES, TOK_WIDTH), F32),
            pltpu.VMEM((SUBLANES, TOK_WIDTH), F32),
            pltpu.VMEM((n_sub, D_MODEL // LANES, SUB_LRU, LANES), F32),
            pltpu.VMEM((n_sub, D_MODEL // LANES, SUB_LRU, LANES), F32),
        ],
        compiler_params=pltpu.CompilerParams(
            dimension_semantics=("arbitrary", "arbitrary"), vmem_limit_bytes=VMEM_LIMIT),
        name="lru_layer",
    )(h, w_in, conv_w, conv_b, w_r, b_r, w_i, b_i, lam, kbd, vbd, w_out, ln_g, ln_b)


def _block_diag_tiles(w):
    per_tile = GATE_TILE // HEAD_DIM
    w = w.reshape(TOK_WIDTH // GATE_TILE, per_tile, HEAD_DIM, HEAD_DIM)
    eye = jnp.eye(per_tile, dtype=w.dtype)
    t = w[:, :, :, None, :] * eye[None, :, None, :, None]
    return t.reshape(TOK_WIDTH // GATE_TILE, GATE_TILE, GATE_TILE)


def _rope_frequencies(tm):
    inv_freq = ROPE_THETA ** (-np.arange(HALF_ROPE, dtype=np.float32) / np.float32(HALF_ROPE))
    return np.ascontiguousarray(np.broadcast_to(inv_freq[:, None], (HALF_ROPE, tm)))


def _head_lanes(w, nope, t1, t2):
    zero = w.shape[-1]
    idx = np.full((LANES,), zero, np.int32)
    head = LANES // 2 - HALF_ROPE
    if t1 is not None:
        idx[:HALF_ROPE] = t1 + np.arange(HALF_ROPE)
        idx[LANES // 2:LANES // 2 + HALF_ROPE] = t2 + np.arange(HALF_ROPE)
    if nope is not None:
        idx[HALF_ROPE:LANES // 2] = nope + np.arange(head)
        idx[LANES // 2 + HALF_ROPE:LANES // 2 + HALF_ROPE + QK_NOPE - head] = nope + head + np.arange(QK_NOPE - head)
    w_ext = jnp.concatenate([w, jnp.zeros(w.shape[:-1] + (1,), w.dtype)], axis=-1)
    return jnp.take(w_ext, jnp.asarray(idx), axis=-1)


def kernel(x, mem, positions, mla_w_in, mla_q_norm, mla_w_uq, mla_kv_norm, mla_w_ukv, lru_w_in, lru_conv_w,
           lru_conv_b, lru_w_rgate, lru_b_rgate, lru_w_igate, lru_b_igate, lru_lambda, w_mem_kv, w_out, ln_g, ln_b):
    b, s, _ = x.shape
    assert s % TQ_ATTN == 0 and s % TM_LRU == 0 and s % TM_OUT == 0 and s % TM_PROJ == 0 and TK_ATTN % TM_PROJ == 0

    kbd, vbd = _mem_kv(mem, w_mem_kv)

    w_in = mla_w_in[0]
    o_q, o_kv, o_kr, o_gate = Q_LORA, Q_LORA + KV_LORA, Q_LORA + KV_LORA + QK_ROPE, Q_LORA + KV_LORA + QK_ROPE + MIX_WIDTH
    w_kr = _head_lanes(w_in[:, o_kv:o_kr], None, 0, HALF_ROPE)
    w_in_a = jnp.concatenate(
        [w_in[:, :o_q], w_in[:, o_q:o_kv], w_kr, w_in[:, o_kr:o_gate], w_in[:, o_gate:]], axis=1).astype(BF16)
    w_uq = _head_lanes(mla_w_uq[0].reshape(Q_LORA, N_TOK_HEADS, QK_DIM), 0, QK_NOPE, QK_NOPE + HALF_ROPE)
    w_uq = w_uq.reshape(Q_LORA, N_TOK_HEADS * LANES).astype(BF16)
    w_ukv = mla_w_ukv[0].reshape(KV_LORA, N_TOK_HEADS, QK_NOPE + HEAD_DIM)
    w_k = _head_lanes(w_ukv[:, :, :QK_NOPE], 0, None, None)
    w_k = w_k.reshape(KV_LORA, N_TOK_HEADS * LANES).astype(BF16)
    w_vT = w_ukv[:, :, QK_NOPE:].reshape(KV_LORA, TOK_WIDTH).T.astype(BF16)
    pos = positions.astype(F32)[:, None, :]
    w_out_b = w_out.astype(BF16)

    q, k, vT, g, mg = _mla_proj(x, pos, w_in_a, mla_q_norm[0][None], mla_kv_norm[0][None], w_uq, w_k, w_vT,
                                jnp.asarray(_rope_frequencies(TM_PROJ)), kbd, vbd)
    tokT = _attention(q, k, vT)
    h1 = _out_ln(tokT, g, mg, x, w_out_b[0], ln_g[0][None], ln_b[0][None])

    return _lru_layer(h1, lru_w_in[0].astype(BF16), lru_conv_w[0], lru_conv_b[0][None],
                      _block_diag_tiles(lru_w_rgate[0]).astype(BF16), lru_b_rgate[0][None],
                      _block_diag_tiles(lru_w_igate[0]).astype(BF16), lru_b_igate[0][None],
                      lru_lambda[0][None], kbd, vbd, w_out_b[1], ln_g[1][None], ln_b[1][None])
```

```python
import functools
import math

import jax
import jax.numpy as jnp
import numpy as np
from jax import lax
from jax.experimental import pallas as pl
from jax.experimental.pallas import tpu as pltpu

F32 = jnp.float32
BF16 = jnp.bfloat16

D_MODEL = 1024
DEPTH = 2
MEM_LEN = 256
HEAD_DIM = 64
N_MEM_HEADS = 4
N_TOK_HEADS = 12
TOK_WIDTH = N_TOK_HEADS * HEAD_DIM
MEM_WIDTH = N_MEM_HEADS * HEAD_DIM
MIX_WIDTH = TOK_WIDTH + MEM_WIDTH
Q_LORA = 384
KV_LORA = 256
QK_NOPE = 64
QK_ROPE = 32
QK_DIM = QK_NOPE + QK_ROPE
ROPE_THETA = 10000.0
CONV_W = 4
LRU_C = 8.0
ALPHA = (2.0 * DEPTH) ** 0.25
NORM_EPS = 1e-6

MXU_N = 256
LANES = 128
SUBLANES = 8
HALF_ROPE = QK_ROPE // 2
LOG2E = math.log2(math.e)
Q_SCALE = LOG2E / math.sqrt(QK_DIM)

TM_PROJ = 1024
SUB_PROJ = 256
TK_ATTN = 1024
TQ_ATTN = 2 * TK_ATTN
GW_ATTN = 256
TM_OUT = 1024
SUB_OUT = 256
TM_LRU = 1024
SUB_LRU = 256
LRU_DOT_N = MXU_N
VMEM_LIMIT = 48 * 1024 * 1024

_NT = (((1,), (1,)), ((), ()))


def _const_spec(shape):
    nd = len(shape)
    return pl.BlockSpec(shape, lambda *_: (0,) * nd, pipeline_mode=pl.Buffered(1))


def _dot_cols(x, w_ref, lo, hi, row_lo=None, row_hi=None):
    parts = [jnp.dot(x, w_ref[row_lo:row_hi, c:min(c + MXU_N, hi)], preferred_element_type=F32)
             for c in range(lo, hi, MXU_N)]
    return parts[0] if len(parts) == 1 else jnp.concatenate(parts, axis=-1)


def _sigmoid(t):
    return 0.5 * jnp.tanh(0.5 * t) + 0.5


def _silu(t):
    h = 0.5 * t
    return h * jnp.tanh(h) + h


def _rms_norm(t, g):
    return t * lax.rsqrt(jnp.mean(t * t, axis=-1, keepdims=True) + NORM_EPS) * g


def _layer_norm(t, g, b):
    mu = jnp.mean(t, axis=-1, keepdims=True)
    c = t - mu
    var = jnp.mean(c * c, axis=-1, keepdims=True)
    return c * lax.rsqrt(var + NORM_EPS) * g + b


def _memory_attention(qm, kbd_ref, vbd_ref):
    s = _dot_cols(qm.astype(BF16), kbd_ref, 0, N_MEM_HEADS * MEM_LEN)
    ps = []
    for h in range(N_MEM_HEADS):
        sh = s[:, h * MEM_LEN:(h + 1) * MEM_LEN]
        e = jnp.exp(sh - jnp.max(sh, axis=-1, keepdims=True))
        ps.append((e * (1.0 / jnp.sum(e, axis=-1, keepdims=True))).astype(BF16))
    p = jnp.concatenate(ps, axis=-1)
    return jnp.dot(p, vbd_ref[...], preferred_element_type=F32)


def _mem_kv_kernel(mem_ref, wkT_ref, wv_ref, kbd_ref, vbd_ref):
    mb = mem_ref[0].astype(BF16)
    kT = lax.dot_general(wkT_ref[0].astype(BF16), mb, _NT, preferred_element_type=F32)
    kT = (kT * (1.0 / math.sqrt(HEAD_DIM))).astype(BF16)
    v = jnp.dot(mb, wv_ref[0].astype(BF16), preferred_element_type=F32).astype(BF16)
    row = lax.broadcasted_iota(jnp.int32, (MEM_WIDTH, MEM_LEN), 0)
    col = lax.broadcasted_iota(jnp.int32, (MEM_LEN, MEM_WIDTH), 1)
    zero = jnp.zeros((), BF16)
    for h in range(N_MEM_HEADS):
        lo, hi = h * HEAD_DIM, (h + 1) * HEAD_DIM
        kbd_ref[0, 0, :, h * MEM_LEN:(h + 1) * MEM_LEN] = jnp.where((row >= lo) & (row < hi), kT, zero)
        vbd_ref[0, 0, h * MEM_LEN:(h + 1) * MEM_LEN, :] = jnp.where((col >= lo) & (col < hi), v, zero)


def _mem_kv(mem, w_mem_kv):
    b = mem.shape[0]
    wkT = jnp.swapaxes(w_mem_kv[:, :, :MEM_WIDTH], 1, 2)
    wv = w_mem_kv[:, :, MEM_WIDTH:]
    return pl.pallas_call(
        _mem_kv_kernel,
        grid=(DEPTH, b),
        in_specs=[
            pl.BlockSpec((1, MEM_LEN, D_MODEL), lambda l, i: (i, 0, 0)),
            pl.BlockSpec((1, MEM_WIDTH, D_MODEL), lambda l, i: (l, 0, 0)),
            pl.BlockSpec((1, D_MODEL, MEM_WIDTH), lambda l, i: (l, 0, 0)),
        ],
        out_specs=[
            pl.BlockSpec((1, 1, MEM_WIDTH, N_MEM_HEADS * MEM_LEN), lambda l, i: (l, i, 0, 0)),
            pl.BlockSpec((1, 1, N_MEM_HEADS * MEM_LEN, MEM_WIDTH), lambda l, i: (l, i, 0, 0)),
        ],
        out_shape=[
            jax.ShapeDtypeStruct((DEPTH, b, MEM_WIDTH, N_MEM_HEADS * MEM_LEN), BF16),
            jax.ShapeDtypeStruct((DEPTH, b, N_MEM_HEADS * MEM_LEN, MEM_WIDTH), BF16),
        ],
        name="mem_kv",
    )(mem, wkT, wv)


_A_CQ = (0, Q_LORA)
_A_CKV = (_A_CQ[1], _A_CQ[1] + KV_LORA)
_A_KR = (_A_CKV[1], _A_CKV[1] + LANES)
_A_GATE = (_A_KR[1], _A_KR[1] + MIX_WIDTH)
_A_QM = (_A_GATE[1], _A_GATE[1] + MEM_WIDTH)
_A_WIDTH = _A_QM[1]


def _mla_proj_kernel(x_ref, pos_ref, w_in_ref, qn_ref, kvn_ref, w_uq_ref, w_k_ref, w_vT_ref, rope_ref,
                     kbd_ref, vbd_ref, q_ref, k_ref, vT_ref, g_ref, mg_ref):
    sub = SUB_PROJ
    n_sub = x_ref.shape[1] // sub
    heads_per_tile = MXU_N // LANES

    ang = rope_ref[...] * pos_ref[0]
    cos_t, sin_t = jnp.cos(ang), jnp.sin(ang)
    gap = LANES // 2 - HALF_ROPE
    ones, zeros = jnp.ones((gap, ang.shape[1]), F32), jnp.zeros((gap, ang.shape[1]), F32)
    cos_all = jnp.concatenate([cos_t, ones, cos_t, ones], axis=0).T
    sin_all = jnp.concatenate([-sin_t, zeros, sin_t, zeros], axis=0).T

    def sub_tile_pieces(r):
        rows = slice(r * sub, (r + 1) * sub)
        cos, sin = cos_all[rows, :], sin_all[rows, :]
        v = {}

        def rope(t):
            return t * cos + pltpu.roll(t, LANES // 2, 1) * sin

        def load():
            v["xb"] = x_ref[0, rows, :].astype(BF16)

        def project(c):
            v["z", c] = jnp.dot(v["xb"], w_in_ref[:, c * MXU_N:(c + 1) * MXU_N], preferred_element_type=F32)

        def seg(s):
            z = jnp.concatenate([v["z", c] for c in range(s[0] // MXU_N, (s[1] - 1) // MXU_N + 1)], axis=1)
            lo = s[0] - (s[0] // MXU_N) * MXU_N
            return z[:, lo:lo + s[1] - s[0]]

        def q_norm():
            v["cqn"] = _rms_norm(seg(_A_CQ), qn_ref[...]).astype(BF16)

        def q_heads(t):
            q = jnp.dot(v["cqn"], w_uq_ref[:, t * MXU_N:(t + 1) * MXU_N], preferred_element_type=F32)
            for i in range(heads_per_tile):
                q_ref[0, t * heads_per_tile + i, rows, :] = (
                    rope(q[:, i * LANES:(i + 1) * LANES]) * Q_SCALE).astype(BF16)

        def kv_norm():
            v["ckvn"] = _rms_norm(seg(_A_CKV), kvn_ref[...]).astype(BF16)
            v["k_rope"] = rope(seg(_A_KR))

        def k_heads(t):
            k = jnp.dot(v["ckvn"], w_k_ref[:, t * MXU_N:(t + 1) * MXU_N], preferred_element_type=F32)
            for i in range(heads_per_tile):
                k_ref[0, t * heads_per_tile + i, rows, :] = (k[:, i * LANES:(i + 1) * LANES] + v["k_rope"]).astype(BF16)

        def values():
            vT = lax.dot_general(w_vT_ref[...], v["ckvn"], _NT, preferred_element_type=F32)
            for h in range(N_TOK_HEADS):
                vT_ref[0, h, 0, :, rows] = vT[h * HEAD_DIM:(h + 1) * HEAD_DIM, :].astype(BF16)

        def gates():
            g = _silu(seg(_A_GATE))
            g_ref[0, rows, :] = g[:, :TOK_WIDTH].astype(g_ref.dtype)
            mem_out = _memory_attention(seg(_A_QM), kbd_ref.at[0, 0], vbd_ref.at[0, 0])
            mg_ref[0, rows, :] = (mem_out * g[:, TOK_WIDTH:]).astype(mg_ref.dtype)

        n_tiles = N_TOK_HEADS // heads_per_tile
        matmul_heavy = [load] + [functools.partial(project, c) for c in range(_A_WIDTH // MXU_N)]
        vector_heavy = ([q_norm] + [functools.partial(q_heads, t) for t in range(n_tiles)] + [kv_norm]
                        + [functools.partial(k_heads, t) for t in range(n_tiles)] + [values, gates])
        return matmul_heavy, vector_heavy

    pieces = [sub_tile_pieces(r) for r in range(n_sub)]
    _emit_round_robin(pieces[0][0])
    for r in range(n_sub):
        _emit_round_robin(pieces[r + 1][0] if r + 1 < n_sub else [], pieces[r][1])


def _mla_proj(x, pos, w_in_a, qn, kvn, w_uq, w_k, w_vT, rope_c, kbd, vbd):
    b, s, _ = x.shape
    tm = TM_PROJ
    nt = s // tm
    return pl.pallas_call(
        _mla_proj_kernel,
        grid=(b, nt),
        in_specs=[
            pl.BlockSpec((1, tm, D_MODEL), lambda i, j: (i, j, 0)),
            pl.BlockSpec((1, 1, tm), lambda i, j: (i, 0, j)),
            _const_spec((D_MODEL, _A_WIDTH)),
            _const_spec((1, Q_LORA)),
            _const_spec((1, KV_LORA)),
            _const_spec((Q_LORA, N_TOK_HEADS * LANES)),
            _const_spec((KV_LORA, N_TOK_HEADS * LANES)),
            _const_spec((TOK_WIDTH, KV_LORA)),
            _const_spec((HALF_ROPE, tm)),
            pl.BlockSpec((1, 1, MEM_WIDTH, N_MEM_HEADS * MEM_LEN), lambda i, j: (0, i, 0, 0)),
            pl.BlockSpec((1, 1, N_MEM_HEADS * MEM_LEN, MEM_WIDTH), lambda i, j: (0, i, 0, 0)),
        ],
        out_specs=[
            pl.BlockSpec((1, N_TOK_HEADS, tm, LANES), lambda i, j: (i, 0, j, 0)),
            pl.BlockSpec((1, N_TOK_HEADS, tm, LANES), lambda i, j: (i, 0, j, 0)),
            pl.BlockSpec((1, N_TOK_HEADS, 1, HEAD_DIM, tm), lambda i, j: (i, 0, j, 0, 0)),
            pl.BlockSpec((1, tm, TOK_WIDTH), lambda i, j: (i, j, 0)),
            pl.BlockSpec((1, tm, MEM_WIDTH), lambda i, j: (i, j, 0)),
        ],
        out_shape=[
            jax.ShapeDtypeStruct((b, N_TOK_HEADS, s, LANES), BF16),
            jax.ShapeDtypeStruct((b, N_TOK_HEADS, s, LANES), BF16),
            jax.ShapeDtypeStruct((b, N_TOK_HEADS, nt, HEAD_DIM, tm), BF16),
            jax.ShapeDtypeStruct((b, s, TOK_WIDTH), BF16),
            jax.ShapeDtypeStruct((b, s, MEM_WIDTH), BF16),
        ],
        compiler_params=pltpu.CompilerParams(
            dimension_semantics=("arbitrary", "arbitrary"), vmem_limit_bytes=VMEM_LIMIT),
        name="mla_proj",
    )(x, pos, w_in_a, qn, kvn, w_uq, w_k, w_vT, rope_c, kbd, vbd)


def _attn_kernel(q_ref, q_next_ref, k_ref, vT_ref, o_ref, s_ref, bm_ref, m_ref, l_ref, acc_ref):
    tq, tk, gw, tv = TQ_ATTN, TK_ATTN, GW_ATTN, TM_PROJ
    groups = range(tq // gw)
    qi = pl.program_id(2)
    last_tile = pl.num_programs(2) - 1
    m_ref[...] = jnp.full((1, tq), -jnp.inf, F32)
    l_ref[...] = jnp.zeros((1, tq), F32)
    acc_ref[...] = jnp.zeros((HEAD_DIM, tq), F32)

    def visible_rows(g, key_off):
        return tk if key_off is None else max(0, min(tk, (g + 1) * gw - key_off))

    def scores(chunk, g, slot, key_off=None, queries=q_ref):
        c0, rows = g * gw, visible_rows(g, key_off)
        if rows == 0:
            return
        start = pl.multiple_of(chunk * tk, tk)
        kt = k_ref[0, 0, pl.ds(start, rows), :]
        st = lax.dot_general(kt, queries[0, 0, c0:c0 + gw, :], _NT, preferred_element_type=F32)
        if key_off is not None and key_off + rows - 1 > c0:
            key = lax.broadcasted_iota(jnp.int32, st.shape, 0) + key_off
            qry = lax.broadcasted_iota(jnp.int32, st.shape, 1) + c0
            st = jnp.where(key <= qry, st, -jnp.inf)
        s_ref[slot, :rows, c0:c0 + gw] = st
        bm_ref[slot, :, c0:c0 + gw] = jnp.max(st, axis=0, keepdims=True)

    def update(chunk, g, slot, key_off=None):
        c0, c1, rows = g * gw, (g + 1) * gw, visible_rows(g, key_off)
        if rows == 0:
            return
        m_old = m_ref[:, c0:c1]
        m_new = jnp.maximum(m_old, bm_ref[slot, :, c0:c1])
        alpha = jnp.exp2(m_old - m_new)
        p = jnp.exp2(s_ref[slot, :rows, c0:c1] - m_new)
        l_ref[:, c0:c1] = alpha * l_ref[:, c0:c1] + jnp.sum(p, axis=0, keepdims=True)
        m_ref[:, c0:c1] = m_new
        pb = p.astype(BF16)
        pv = None
        for r0 in range(0, rows, tv):
            r1 = min(rows, r0 + tv)
            part = jnp.dot(vT_ref[0, 0, chunk * (tk // tv) + r0 // tv, :, :r1 - r0], pb[r0:r1, :],
                           preferred_element_type=F32)
            pv = part if pv is None else pv + part
        acc_ref[:, c0:c1] = alpha * acc_ref[:, c0:c1] + pv

    def stage(nxt_chunk, nxt_key_off, cur_chunk, cur_slot, cur_key_off=None, after=None):
        for g in groups:
            if nxt_chunk is not None:
                scores(nxt_chunk, g, 1 - cur_slot, nxt_key_off)
            update(cur_chunk, g, cur_slot, cur_key_off)
            if after is not None:
                after(g)

    def next_tile_first_chunk(g):
        scores(0, g, 0, None, q_next_ref)

    d = 2 * qi

    def diagonal_tail(look_ahead):
        stage(d + 1, tk, d, 0, 0, next_tile_first_chunk if look_ahead else None)
        stage(None, None, d + 1, 1, tk)
        o_ref[0] = (acc_ref[...] * (1.0 / l_ref[...])).astype(o_ref.dtype)

    @pl.when(qi > 0)
    def _():
        def pair(p, _):
            c = 2 * p
            stage(c + 1, None, c, 0)
            stage(c + 2, None, c + 1, 1)
            return 0

        lax.fori_loop(0, qi - 1, pair, 0)
        stage(d - 1, None, d - 2, 0)
        stage(d, 0, d - 1, 1)

        @pl.when(qi < last_tile)
        def _():
            diagonal_tail(True)

        @pl.when(qi == last_tile)
        def _():
            diagonal_tail(False)

    @pl.when(qi == 0)
    def _():
        for g in groups:
            scores(d, g, 0, 0)
        diagonal_tail(True)


def _attention(q, k, vT):
    b, h, s, _ = q.shape
    tq, tk = TQ_ATTN, TK_ATTN
    nkv = vT.shape[2]
    return pl.pallas_call(
        _attn_kernel,
        grid=(b, h, s // tq),
        in_specs=[
            pl.BlockSpec((1, 1, tq, LANES), lambda i, j, t: (i, j, t, 0)),
            pl.BlockSpec((1, 1, tq, LANES), lambda i, j, t: (i, j, jnp.minimum(t + 1, s // tq - 1), 0)),
            pl.BlockSpec((1, 1, s, LANES), lambda i, j, t: (i, j, 0, 0)),
            pl.BlockSpec((1, 1, nkv, HEAD_DIM, TM_PROJ), lambda i, j, t: (i, j, 0, 0, 0)),
        ],
        out_specs=pl.BlockSpec((1, HEAD_DIM, tq), lambda i, j, t: (i, j, t)),
        out_shape=jax.ShapeDtypeStruct((b, h * HEAD_DIM, s), BF16),
        scratch_shapes=[
            pltpu.VMEM((2, tk, tq), F32),
            pltpu.VMEM((2, 1, tq), F32),
            pltpu.VMEM((1, tq), F32),
            pltpu.VMEM((1, tq), F32),
            pltpu.VMEM((HEAD_DIM, tq), F32),
        ],
        compiler_params=pltpu.CompilerParams(
            dimension_semantics=("arbitrary", "arbitrary", "arbitrary"), vmem_limit_bytes=VMEM_LIMIT),
        name="mla_attn",
    )(q, q, k, vT)


def _out_ln_kernel(tokT_ref, g_ref, mg_ref, h_ref, w_out_ref, lng_ref, lnb_ref, o_ref):
    sub = SUB_OUT
    n_sub = o_ref.shape[1] // sub
    v = {}

    def gate(r):
        rows = slice(r * sub, (r + 1) * sub)
        tok = tokT_ref[0, :, rows].astype(F32).T
        v["y", r] = (tok * g_ref[0, rows, :].astype(F32)).astype(BF16)

    def project(r, c):
        cols = slice(c * MXU_N, (c + 1) * MXU_N)
        v["o", r, c] = (
            jnp.dot(v["y", r], w_out_ref[:TOK_WIDTH, cols], preferred_element_type=F32)
            + jnp.dot(mg_ref[0, r * sub:(r + 1) * sub, :], w_out_ref[TOK_WIDTH:, cols], preferred_element_type=F32))

    def finish(r):
        rows = slice(r * sub, (r + 1) * sub)
        o = jnp.concatenate([v["o", r, c] for c in range(D_MODEL // MXU_N)], axis=1)
        o_ref[0, rows, :] = _layer_norm(ALPHA * h_ref[0, rows, :] + o, lng_ref[...], lnb_ref[...])

    def matmul_pieces(r):
        return [functools.partial(gate, r)] + [functools.partial(project, r, c) for c in range(D_MODEL // MXU_N)]

    _emit_round_robin(matmul_pieces(0))
    for r in range(n_sub):
        _emit_round_robin(matmul_pieces(r + 1) if r + 1 < n_sub else [], [functools.partial(finish, r)])


def _out_ln(tokT, g, mg, h, w_out, ln_g, ln_b):
    b, s, _ = h.shape
    tm = TM_OUT
    return pl.pallas_call(
        _out_ln_kernel,
        grid=(b, s // tm),
        in_specs=[
            pl.BlockSpec((1, TOK_WIDTH, tm), lambda i, j: (i, 0, j)),
            pl.BlockSpec((1, tm, TOK_WIDTH), lambda i, j: (i, j, 0)),
            pl.BlockSpec((1, tm, MEM_WIDTH), lambda i, j: (i, j, 0)),
            pl.BlockSpec((1, tm, D_MODEL), lambda i, j: (i, j, 0)),
            _const_spec((MIX_WIDTH, D_MODEL)),
            _const_spec((1, D_MODEL)),
            _const_spec((1, D_MODEL)),
        ],
        out_specs=pl.BlockSpec((1, tm, D_MODEL), lambda i, j: (i, j, 0)),
        out_shape=jax.ShapeDtypeStruct((b, s, D_MODEL), F32),
        compiler_params=pltpu.CompilerParams(
            dimension_semantics=("arbitrary", "arbitrary"), vmem_limit_bytes=VMEM_LIMIT),
        name="mla_out_ln",
    )(tokT, g, mg, h, w_out, ln_g, ln_b)


_B_U = (0, TOK_WIDTH)
_B_GATE = (_B_U[1], _B_U[1] + MIX_WIDTH)
_B_QM = (_B_GATE[1], _B_GATE[1] + MEM_WIDTH)
_B_WIDTH = _B_QM[1]
GATE_TILE = 256


def _linear_scan(a, b):
    n = a.shape[0]
    row = lax.broadcasted_iota(jnp.int32, a.shape, 0)
    d = 1
    while d < n:
        live = row >= d
        b = a * jnp.where(live, pltpu.roll(b, d, 0), 0.0) + b
        if 2 * d < n:
            a = a * jnp.where(live, pltpu.roll(a, d, 0), 1.0)
        d *= 2
    return b


def _emit_round_robin(*piece_lists):
    lists = [list(p) for p in piece_lists if p]
    longest = max((len(p) for p in lists), default=0)
    for k in range(longest):
        for p in lists:
            lo, hi = (k * len(p)) // longest, ((k + 1) * len(p)) // longest
            for piece in p[lo:hi]:
                piece()


def _lru_layer_kernel(h_ref, w_in_ref, conv_w_ref, conv_b_ref, w_r_ref, b_r_ref, w_i_ref, b_i_ref, lam_ref,
                      kbd_ref, vbd_ref, w_out_ref, lng_ref, lnb_ref, o_ref,
                      ebuf_ref, tail_ref, carry_ref, perm_in_ref, perm_out_ref):
    tm, sl, ct = SUB_LRU, SUBLANES, GATE_TILE
    n = tm // sl
    halo = (CONV_W - 1) * sl
    lane_tiles = D_MODEL // LANES
    n_sub = TM_LRU // SUB_LRU
    step = pl.program_id(1)

    @pl.when(step == 0)
    def _():
        tail_ref[...] = jnp.zeros((halo, TOK_WIDTH), F32)
        carry_ref[...] = jnp.zeros((sl, TOK_WIDTH), F32)

    sub8 = lax.broadcasted_iota(jnp.int32, (sl, ct), 0)

    def segment_rows(j):
        s, i0 = divmod(sl * j, n)
        return pl.ds(sl * i0 + s, sl, stride=sl)

    def from_previous_segment(prev_tile, cur_tile):
        return jnp.where(sub8 == 0, pltpu.roll(prev_tile, 1, 0), pltpu.roll(cur_tile, 1, 0))

    tails = [tail_ref[:, c * ct:(c + 1) * ct] for c in range(TOK_WIDTH // ct)]
    carries = [carry_ref[:, c * ct:(c + 1) * ct] for c in range(TOK_WIDTH // ct)]

    def sub_tile_phases(r):
        row0 = r * tm
        v = {}

        def load():
            for j in range(n):
                rows = h_ref[0, row0 + j * sl:row0 + (j + 1) * sl, :]
                for c in range(lane_tiles):
                    perm_in_ref[r, c, segment_rows(j), :] = rows[:, c * LANES:(c + 1) * LANES]
            v["hin"] = jnp.concatenate([perm_in_ref[r, c] for c in range(lane_tiles)], axis=1)
            v["hb"] = v["hin"].astype(BF16)

        def project(k):
            z = jnp.dot(v["hb"], w_in_ref[:, k * LRU_DOT_N:(k + 1) * LRU_DOT_N], preferred_element_type=F32)
            for c in range(LRU_DOT_N // MXU_N):
                v["z", k * (LRU_DOT_N // MXU_N) + c] = z[:, c * MXU_N:(c + 1) * MXU_N]

        def recurrence(c):
            cols = slice(c * ct, (c + 1) * ct)
            u = v["z", _B_U[0] // MXU_N + c]
            for k in range(CONV_W - 1):
                cur = u[tm - halo + k * sl:tm - halo + (k + 1) * sl, :]
                ebuf_ref[r, k * sl:(k + 1) * sl, cols] = from_previous_segment(tails[c][k * sl:(k + 1) * sl, :], cur)
            ebuf_ref[r, halo:halo + tm, cols] = u
            tails[c] = u[tm - halo:, :]
            xc = conv_b_ref[:, cols] + u * conv_w_ref[CONV_W - 1:CONV_W, cols]
            for back in range(1, CONV_W):
                tap = CONV_W - 1 - back
                xc = xc + ebuf_ref[r, halo - back * sl:halo - back * sl + tm, cols] * conv_w_ref[tap:tap + 1, cols]

            xcb = xc.astype(BF16)
            gate_r = _sigmoid(jnp.dot(xcb, w_r_ref[c], preferred_element_type=F32) + b_r_ref[:, cols])
            gate_i = _sigmoid(jnp.dot(xcb, w_i_ref[c], preferred_element_type=F32) + b_i_ref[:, cols])

            neg_lam = -lam_ref[:, cols]
            softplus = jnp.maximum(neg_lam, 0.0) + jnp.log1p(jnp.exp(-jnp.abs(neg_lam)))
            log_a = (-LRU_C * softplus) * gate_r
            a = jnp.exp(log_a)
            one_minus_a2 = -jnp.tanh(log_a) * (a * a + 1.0)
            root = jnp.where(one_minus_a2 > 0.0, one_minus_a2 * lax.rsqrt(one_minus_a2), 0.0)
            b = root * (gate_i * xc)

            h_loc, a_run = [b[0:sl, :]], [a[0:sl, :]]
            for g in range(1, n):
                ag = a[g * sl:(g + 1) * sl, :]
                h_loc.append(ag * h_loc[-1] + b[g * sl:(g + 1) * sl, :])
                a_run.append(ag * a_run[-1])
            state_in = pltpu.roll(carries[c], 1, 0)
            seg_b = jnp.where(sub8 == 0, a_run[-1] * state_in + h_loc[-1], h_loc[-1])
            seg_end = _linear_scan(a_run[-1], seg_b)
            start = from_previous_segment(carries[c], seg_end)
            carries[c] = seg_end
            v["hs", c] = jnp.concatenate([h_loc[g] + a_run[g] * start for g in range(n)], axis=0)

        def gate(c):
            g = _silu(v["z", _B_GATE[0] // MXU_N + c])
            if c < TOK_WIDTH // ct:
                v["y", c] = (v["hs", c] * g).astype(BF16)
            else:
                mem_out = _memory_attention(v["z", _B_QM[0] // MXU_N], kbd_ref.at[0, 0], vbd_ref.at[0, 0])
                v["y", c] = (mem_out * g).astype(BF16)

        def out_project(k):
            if k == 0:
                v["yb"] = jnp.concatenate([v["y", c] for c in range(MIX_WIDTH // ct)], axis=1)
            v["o", k] = jnp.dot(v["yb"], w_out_ref[:, k * LRU_DOT_N:(k + 1) * LRU_DOT_N], preferred_element_type=F32)

        def finish():
            o = jnp.concatenate([v["o", k] for k in range(D_MODEL // LRU_DOT_N)], axis=1)
            out = _layer_norm(ALPHA * v["hin"] + o, lng_ref[...], lnb_ref[...])
            for c in range(lane_tiles):
                perm_out_ref[r, c] = out[:, c * LANES:(c + 1) * LANES]
            for j in range(n):
                o_ref[0, row0 + j * sl:row0 + (j + 1) * sl, :] = jnp.concatenate(
                    [perm_out_ref[r, c, segment_rows(j), :] for c in range(lane_tiles)], axis=1)

        n_tok, n_mix = TOK_WIDTH // ct, MIX_WIDTH // ct
        phase_a = [load] + [functools.partial(project, k) for k in range(_B_WIDTH // LRU_DOT_N)]
        phase_b = []
        for c in range(n_mix):
            if c < n_tok:
                phase_b.append(functools.partial(recurrence, c))
            phase_b.append(functools.partial(gate, c))
        phase_cd = [functools.partial(out_project, k) for k in range(D_MODEL // LRU_DOT_N)] + [finish]
        return phase_a, phase_b, phase_cd

    phases = [sub_tile_phases(r) for r in range(n_sub)]
    _emit_round_robin(phases[0][0])
    for r in range(n_sub + 1):
        vector_heavy = phases[r][1] if r < n_sub else []
        matmul_heavy = (phases[r + 1][0] if r + 1 < n_sub else []) + (phases[r - 1][2] if r >= 1 else [])
        _emit_round_robin(matmul_heavy, vector_heavy)

    for c in range(TOK_WIDTH // ct):
        tail_ref[:, c * ct:(c + 1) * ct] = tails[c]
        carry_ref[:, c * ct:(c + 1) * ct] = carries[c]


def _lru_layer(h, w_in, conv_w, conv_b, w_r, b_r, w_i, b_i, lam, kbd, vbd, w_out, ln_g, ln_b):
    b, s, _ = h.shape
    tm = TM_LRU
    n_gate_tiles = TOK_WIDTH // GATE_TILE
    n_sub = TM_LRU // SUB_LRU
    return pl.pallas_call(
        _lru_layer_kernel,
        grid=(b, s // tm),
        in_specs=[
            pl.BlockSpec((1, tm, D_MODEL), lambda i, j: (i, j, 0)),
            _const_spec((D_MODEL, _B_WIDTH)),
            _const_spec((CONV_W, TOK_WIDTH)),
            _const_spec((1, TOK_WIDTH)),
            _const_spec((n_gate_tiles, GATE_TILE, GATE_TILE)),
            _const_spec((1, TOK_WIDTH)),
            _const_spec((n_gate_tiles, GATE_TILE, GATE_TILE)),
            _const_spec((1, TOK_WIDTH)),
            _const_spec((1, TOK_WIDTH)),
            pl.BlockSpec((1, 1, MEM_WIDTH, N_MEM_HEADS * MEM_LEN), lambda i, j: (1, i, 0, 0)),
            pl.BlockSpec((1, 1, N_MEM_HEADS * MEM_LEN, MEM_WIDTH), lambda i, j: (1, i, 0, 0)),
            _const_spec((MIX_WIDTH, D_MODEL)),
            _const_spec((1, D_MODEL)),
            _const_spec((1, D_MODEL)),
        ],
        out_specs=pl.BlockSpec((1, tm, D_MODEL), lambda i, j: (i, j, 0)),
        out_shape=jax.ShapeDtypeStruct((b, s, D_MODEL), F32),
        scratch_shapes=[
            pltpu.VMEM((n_sub, SUB_LRU + (CONV_W - 1) * SUBLANES, TOK_WIDTH), F32),
            pltpu.VMEM(((CONV_W - 1) * SUBLANES, TOK_WIDTH), F32),
            pltpu.VMEM((SUBLANES, TOK_WIDTH), F32),
            pltpu.VMEM((n_sub, D_MODEL // LANES, SUB_LRU, LANES), F32),
            pltpu.VMEM((n_sub, D_MODEL // LANES, SUB_LRU, LANES), F32),
        ],
        compiler_params=pltpu.CompilerParams(
            dimension_semantics=("arbitrary", "arbitrary"), vmem_limit_bytes=VMEM_LIMIT),
        name="lru_layer",
    )(h, w_in, conv_w, conv_b, w_r, b_r, w_i, b_i, lam, kbd, vbd, w_out, ln_g, ln_b)


def _block_diag_tiles(w):
    per_tile = GATE_TILE // HEAD_DIM
    w = w.reshape(TOK_WIDTH // GATE_TILE, per_tile, HEAD_DIM, HEAD_DIM)
    eye = jnp.eye(per_tile, dtype=w.dtype)
    t = w[:, :, :, None, :] * eye[None, :, None, :, None]
    return t.reshape(TOK_WIDTH // GATE_TILE, GATE_TILE, GATE_TILE)


def _rope_frequencies(tm):
    inv_freq = ROPE_THETA ** (-np.arange(HALF_ROPE, dtype=np.float32) / np.float32(HALF_ROPE))
    return np.ascontiguousarray(np.broadcast_to(inv_freq[:, None], (HALF_ROPE, tm)))


def _head_lanes(w, nope, t1, t2):
    zero = w.shape[-1]
    idx = np.full((LANES,), zero, np.int32)
    head = LANES // 2 - HALF_ROPE
    if t1 is not None:
        idx[:HALF_ROPE] = t1 + np.arange(HALF_ROPE)
        idx[LANES // 2:LANES // 2 + HALF_ROPE] = t2 + np.arange(HALF_ROPE)
    if nope is not None:
        idx[HALF_ROPE:LANES // 2] = nope + np.arange(head)
        idx[LANES // 2 + HALF_ROPE:LANES // 2 + HALF_ROPE + QK_NOPE - head] = nope + head + np.arange(QK_NOPE - head)
    w_ext = jnp.concatenate([w, jnp.zeros(w.shape[:-1] + (1,), w.dtype)], axis=-1)
    return jnp.take(w_ext, jnp.asarray(idx), axis=-1)


def kernel(x, mem, positions, mla_w_in, mla_q_norm, mla_w_uq, mla_kv_norm, mla_w_ukv, lru_w_in, lru_conv_w,
           lru_conv_b, lru_w_rgate, lru_b_rgate, lru_w_igate, lru_b_igate, lru_lambda, w_mem_kv, w_out, ln_g, ln_b):
    b, s, _ = x.shape
    assert s % TQ_ATTN == 0 and s % TM_LRU == 0 and s % TM_OUT == 0 and s % TM_PROJ == 0 and TK_ATTN % TM_PROJ == 0

    kbd, vbd = _mem_kv(mem, w_mem_kv)

    w_in = mla_w_in[0]
    o_q, o_kv, o_kr, o_gate = Q_LORA, Q_LORA + KV_LORA, Q_LORA + KV_LORA + QK_ROPE, Q_LORA + KV_LORA + QK_ROPE + MIX_WIDTH
    w_kr = _head_lanes(w_in[:, o_kv:o_kr], None, 0, HALF_ROPE)
    w_in_a = jnp.concatenate(
        [w_in[:, :o_q], w_in[:, o_q:o_kv], w_kr, w_in[:, o_kr:o_gate], w_in[:, o_gate:]], axis=1).astype(BF16)
    w_uq = _head_lanes(mla_w_uq[0].reshape(Q_LORA, N_TOK_HEADS, QK_DIM), 0, QK_NOPE, QK_NOPE + HALF_ROPE)
    w_uq = w_uq.reshape(Q_LORA, N_TOK_HEADS * LANES).astype(BF16)
    w_ukv = mla_w_ukv[0].reshape(KV_LORA, N_TOK_HEADS, QK_NOPE + HEAD_DIM)
    w_k = _head_lanes(w_ukv[:, :, :QK_NOPE], 0, None, None)
    w_k = w_k.reshape(KV_LORA, N_TOK_HEADS * LANES).astype(BF16)
    w_vT = w_ukv[:, :, QK_NOPE:].reshape(KV_LORA, TOK_WIDTH).T.astype(BF16)
    pos = positions.astype(F32)[:, None, :]
    w_out_b = w_out.astype(BF16)

    q, k, vT, g, mg = _mla_proj(x, pos, w_in_a, mla_q_norm[0][None], mla_kv_norm[0][None], w_uq, w_k, w_vT,
                                jnp.asarray(_rope_frequencies(TM_PROJ)), kbd, vbd)
    tokT = _attention(q, k, vT)
    h1 = _out_ln(tokT, g, mg, x, w_out_b[0], ln_g[0][None], ln_b[0][None])

    return _lru_layer(h1, lru_w_in[0].astype(BF16), lru_conv_w[0], lru_conv_b[0][None],
                      _block_diag_tiles(lru_w_rgate[0]).astype(BF16), lru_b_rgate[0][None],
                      _block_diag_tiles(lru_w_igate[0]).astype(BF16), lru_b_igate[0][None],
                      lru_lambda[0][None], kbd, vbd, w_out_b[1], ln_g[1][None], ln_b[1][None])
```

```python
import functools
import math

import jax
import jax.numpy as jnp
import numpy as np
from jax import lax
from jax.experimental import pallas as pl
from jax.experimental.pallas import tpu as pltpu

F32 = jnp.float32
BF16 = jnp.bfloat16

D_MODEL = 1024
DEPTH = 2
MEM_LEN = 256
HEAD_DIM = 64
N_MEM_HEADS = 4
N_TOK_HEADS = 12
TOK_WIDTH = N_TOK_HEADS * HEAD_DIM
MEM_WIDTH = N_MEM_HEADS * HEAD_DIM
MIX_WIDTH = TOK_WIDTH + MEM_WIDTH
Q_LORA = 384
KV_LORA = 256
QK_NOPE = 64
QK_ROPE = 32
QK_DIM = QK_NOPE + QK_ROPE
ROPE_THETA = 10000.0
CONV_W = 4
LRU_C = 8.0
ALPHA = (2.0 * DEPTH) ** 0.25
NORM_EPS = 1e-6

MXU_N = 256
LANES = 128
SUBLANES = 8
HALF_ROPE = QK_ROPE // 2
LOG2E = math.log2(math.e)
Q_SCALE = LOG2E / math.sqrt(QK_DIM)

TM_PROJ = 1024
SUB_PROJ = 256
TK_ATTN = 1024
TQ_ATTN = 2 * TK_ATTN
GW_ATTN = 256
TM_OUT = 1024
SUB_OUT = 256
TM_LRU = 1024
SUB_LRU = 256
LRU_DOT_N = MXU_N
VMEM_LIMIT = 48 * 1024 * 1024

_NT = (((1,), (1,)), ((), ()))


def _const_spec(shape):
    nd = len(shape)
    return pl.BlockSpec(shape, lambda *_: (0,) * nd, pipeline_mode=pl.Buffered(1))


def _dot_cols(x, w_ref, lo, hi, row_lo=None, row_hi=None):
    parts = [jnp.dot(x, w_ref[row_lo:row_hi, c:min(c + MXU_N, hi)], preferred_element_type=F32)
             for c in range(lo, hi, MXU_N)]
    return parts[0] if len(parts) == 1 else jnp.concatenate(parts, axis=-1)


def _sigmoid(t):
    return 0.5 * jnp.tanh(0.5 * t) + 0.5


def _silu(t):
    h = 0.5 * t
    return h * jnp.tanh(h) + h


def _rms_norm(t, g):
    return t * lax.rsqrt(jnp.mean(t * t, axis=-1, keepdims=True) + NORM_EPS) * g


def _layer_norm(t, g, b):
    mu = jnp.mean(t, axis=-1, keepdims=True)
    c = t - mu
    var = jnp.mean(c * c, axis=-1, keepdims=True)
    return c * lax.rsqrt(var + NORM_EPS) * g + b


def _memory_attention(qm, kbd_ref, vbd_ref):
    s = _dot_cols(qm.astype(BF16), kbd_ref, 0, N_MEM_HEADS * MEM_LEN)
    ps = []
    for h in range(N_MEM_HEADS):
        sh = s[:, h * MEM_LEN:(h + 1) * MEM_LEN]
        e = jnp.exp2(sh - jnp.max(sh, axis=-1, keepdims=True))
        ps.append((e * (1.0 / jnp.sum(e, axis=-1, keepdims=True))).astype(BF16))
    p = jnp.concatenate(ps, axis=-1)
    return jnp.dot(p, vbd_ref[...], preferred_element_type=F32)


def _mem_kv_kernel(mem_ref, wkT_ref, wv_ref, kbd_ref, vbd_ref):
    mb = mem_ref[0].astype(BF16)
    kT = lax.dot_general(wkT_ref[0].astype(BF16), mb, _NT, preferred_element_type=F32)
    kT = (kT * (LOG2E / math.sqrt(HEAD_DIM))).astype(BF16)
    v = jnp.dot(mb, wv_ref[0].astype(BF16), preferred_element_type=F32).astype(BF16)
    row = lax.broadcasted_iota(jnp.int32, (MEM_WIDTH, MEM_LEN), 0)
    col = lax.broadcasted_iota(jnp.int32, (MEM_LEN, MEM_WIDTH), 1)
    zero = jnp.zeros((), BF16)
    for h in range(N_MEM_HEADS):
        lo, hi = h * HEAD_DIM, (h + 1) * HEAD_DIM
        kbd_ref[0, 0, :, h * MEM_LEN:(h + 1) * MEM_LEN] = jnp.where((row >= lo) & (row < hi), kT, zero)
        vbd_ref[0, 0, h * MEM_LEN:(h + 1) * MEM_LEN, :] = jnp.where((col >= lo) & (col < hi), v, zero)


def _mem_kv(mem, w_mem_kv):
    b = mem.shape[0]
    wkT = jnp.swapaxes(w_mem_kv[:, :, :MEM_WIDTH], 1, 2)
    wv = w_mem_kv[:, :, MEM_WIDTH:]
    return pl.pallas_call(
        _mem_kv_kernel,
        grid=(DEPTH, b),
        in_specs=[
            pl.BlockSpec((1, MEM_LEN, D_MODEL), lambda l, i: (i, 0, 0)),
            pl.BlockSpec((1, MEM_WIDTH, D_MODEL), lambda l, i: (l, 0, 0)),
            pl.BlockSpec((1, D_MODEL, MEM_WIDTH), lambda l, i: (l, 0, 0)),
        ],
        out_specs=[
            pl.BlockSpec((1, 1, MEM_WIDTH, N_MEM_HEADS * MEM_LEN), lambda l, i: (l, i, 0, 0)),
            pl.BlockSpec((1, 1, N_MEM_HEADS * MEM_LEN, MEM_WIDTH), lambda l, i: (l, i, 0, 0)),
        ],
        out_shape=[
            jax.ShapeDtypeStruct((DEPTH, b, MEM_WIDTH, N_MEM_HEADS * MEM_LEN), BF16),
            jax.ShapeDtypeStruct((DEPTH, b, N_MEM_HEADS * MEM_LEN, MEM_WIDTH), BF16),
        ],
        name="mem_kv",
    )(mem, wkT, wv)


_A_CQ = (0, Q_LORA)
_A_CKV = (_A_CQ[1], _A_CQ[1] + KV_LORA)
_A_KR = (_A_CKV[1], _A_CKV[1] + LANES)
_A_GATE = (_A_KR[1], _A_KR[1] + MIX_WIDTH)
_A_QM = (_A_GATE[1], _A_GATE[1] + MEM_WIDTH)
_A_WIDTH = _A_QM[1]


def _mla_proj_kernel(x_ref, pos_ref, w_in_ref, qn_ref, kvn_ref, w_uq_ref, w_k_ref, w_vT_ref, rope_ref,
                     kbd_ref, vbd_ref, q_ref, k_ref, vT_ref, g_ref, mg_ref):
    sub = SUB_PROJ
    n_sub = x_ref.shape[1] // sub
    heads_per_tile = MXU_N // LANES

    ang = rope_ref[...] * pos_ref[0]
    cos_t, sin_t = jnp.cos(ang), jnp.sin(ang)
    gap = LANES // 2 - HALF_ROPE
    ones, zeros = jnp.ones((gap, ang.shape[1]), F32), jnp.zeros((gap, ang.shape[1]), F32)
    cos_all = jnp.concatenate([cos_t, ones, cos_t, ones], axis=0).T
    sin_all = jnp.concatenate([-sin_t, zeros, sin_t, zeros], axis=0).T

    def sub_tile_pieces(r):
        rows = slice(r * sub, (r + 1) * sub)
        cos, sin = cos_all[rows, :], sin_all[rows, :]
        v = {}

        def rope(t):
            return t * cos + pltpu.roll(t, LANES // 2, 1) * sin

        def load():
            v["xb"] = x_ref[0, rows, :].astype(BF16)

        def project(c):
            v["z", c] = jnp.dot(v["xb"], w_in_ref[:, c * MXU_N:(c + 1) * MXU_N], preferred_element_type=F32)

        def seg(s):
            z = jnp.concatenate([v["z", c] for c in range(s[0] // MXU_N, (s[1] - 1) // MXU_N + 1)], axis=1)
            lo = s[0] - (s[0] // MXU_N) * MXU_N
            return z[:, lo:lo + s[1] - s[0]]

        def q_norm():
            v["cqn"] = _rms_norm(seg(_A_CQ), qn_ref[...]).astype(BF16)

        def q_heads(t):
            q = jnp.dot(v["cqn"], w_uq_ref[:, t * MXU_N:(t + 1) * MXU_N], preferred_element_type=F32)
            for i in range(heads_per_tile):
                q_ref[0, t * heads_per_tile + i, rows, :] = (
                    rope(q[:, i * LANES:(i + 1) * LANES]) * Q_SCALE).astype(BF16)

        def kv_norm():
            v["ckvn"] = _rms_norm(seg(_A_CKV), kvn_ref[...]).astype(BF16)
            v["k_rope"] = rope(seg(_A_KR))

        def k_heads(t):
            k = jnp.dot(v["ckvn"], w_k_ref[:, t * MXU_N:(t + 1) * MXU_N], preferred_element_type=F32)
            for i in range(heads_per_tile):
                k_ref[0, t * heads_per_tile + i, rows, :] = (k[:, i * LANES:(i + 1) * LANES] + v["k_rope"]).astype(BF16)

        def values():
            vT = lax.dot_general(w_vT_ref[...], v["ckvn"], _NT, preferred_element_type=F32)
            for h in range(N_TOK_HEADS):
                vT_ref[0, h, 0, :, rows] = vT[h * HEAD_DIM:(h + 1) * HEAD_DIM, :].astype(BF16)

        def gates():
            g = _silu(seg(_A_GATE))
            g_ref[0, rows, :] = g[:, :TOK_WIDTH].astype(g_ref.dtype)
            mem_out = _memory_attention(seg(_A_QM), kbd_ref.at[0, 0], vbd_ref.at[0, 0])
            mg_ref[0, rows, :] = (mem_out * g[:, TOK_WIDTH:]).astype(mg_ref.dtype)

        n_tiles = N_TOK_HEADS // heads_per_tile
        matmul_heavy = [load] + [functools.partial(project, c) for c in range(_A_WIDTH // MXU_N)]
        vector_heavy = ([q_norm] + [functools.partial(q_heads, t) for t in range(n_tiles)] + [kv_norm]
                        + [functools.partial(k_heads, t) for t in range(n_tiles)] + [values, gates])
        return matmul_heavy, vector_heavy

    pieces = [sub_tile_pieces(r) for r in range(n_sub)]
    _emit_round_robin(pieces[0][0])
    for r in range(n_sub):
        _emit_round_robin(pieces[r + 1][0] if r + 1 < n_sub else [], pieces[r][1])


def _mla_proj(x, pos, w_in_a, qn, kvn, w_uq, w_k, w_vT, rope_c, kbd, vbd):
    b, s, _ = x.shape
    tm = TM_PROJ
    nt = s // tm
    return pl.pallas_call(
        _mla_proj_kernel,
        grid=(b, nt),
        in_specs=[
            pl.BlockSpec((1, tm, D_MODEL), lambda i, j: (i, j, 0)),
            pl.BlockSpec((1, 1, tm), lambda i, j: (i, 0, j)),
            _const_spec((D_MODEL, _A_WIDTH)),
            _const_spec((1, Q_LORA)),
            _const_spec((1, KV_LORA)),
            _const_spec((Q_LORA, N_TOK_HEADS * LANES)),
            _const_spec((KV_LORA, N_TOK_HEADS * LANES)),
            _const_spec((TOK_WIDTH, KV_LORA)),
            _const_spec((HALF_ROPE, tm)),
            pl.BlockSpec((1, 1, MEM_WIDTH, N_MEM_HEADS * MEM_LEN), lambda i, j: (0, i, 0, 0)),
            pl.BlockSpec((1, 1, N_MEM_HEADS * MEM_LEN, MEM_WIDTH), lambda i, j: (0, i, 0, 0)),
        ],
        out_specs=[
            pl.BlockSpec((1, N_TOK_HEADS, tm, LANES), lambda i, j: (i, 0, j, 0)),
            pl.BlockSpec((1, N_TOK_HEADS, tm, LANES), lambda i, j: (i, 0, j, 0)),
            pl.BlockSpec((1, N_TOK_HEADS, 1, HEAD_DIM, tm), lambda i, j: (i, 0, j, 0, 0)),
            pl.BlockSpec((1, tm, TOK_WIDTH), lambda i, j: (i, j, 0)),
            pl.BlockSpec((1, tm, MEM_WIDTH), lambda i, j: (i, j, 0)),
        ],
        out_shape=[
            jax.ShapeDtypeStruct((b, N_TOK_HEADS, s, LANES), BF16),
            jax.ShapeDtypeStruct((b, N_TOK_HEADS, s, LANES), BF16),
            jax.ShapeDtypeStruct((b, N_TOK_HEADS, nt, HEAD_DIM, tm), BF16),
            jax.ShapeDtypeStruct((b, s, TOK_WIDTH), BF16),
            jax.ShapeDtypeStruct((b, s, MEM_WIDTH), BF16),
        ],
        compiler_params=pltpu.CompilerParams(
            dimension_semantics=("arbitrary", "arbitrary"), vmem_limit_bytes=VMEM_LIMIT),
        name="mla_proj",
    )(x, pos, w_in_a, qn, kvn, w_uq, w_k, w_vT, rope_c, kbd, vbd)


def _attn_kernel(q_ref, q_next_ref, k_ref, k_next_ref, vT_ref, o_ref, s_ref, bm_ref, m_ref, l_ref, acc_ref):
    tq, tk, gw, tv = TQ_ATTN, TK_ATTN, GW_ATTN, TM_PROJ
    groups = range(tq // gw)
    qi = pl.program_id(2)
    last_tile = pl.num_programs(2) - 1
    m_ref[...] = jnp.full((1, tq), -jnp.inf, F32)
    l_ref[...] = jnp.zeros((1, tq), F32)
    acc_ref[...] = jnp.zeros((HEAD_DIM, tq), F32)

    def visible_rows(g, key_off):
        return tk if key_off is None else max(0, min(tk, (g + 1) * gw - key_off))

    def scores(chunk, g, slot, key_off=None, queries=q_ref, keys=k_ref):
        c0, rows = g * gw, visible_rows(g, key_off)
        if rows == 0:
            return
        start = pl.multiple_of(chunk * tk, tk)
        kt = keys[0, 0, pl.ds(start, rows), :]
        st = lax.dot_general(kt, queries[0, 0, c0:c0 + gw, :], _NT, preferred_element_type=F32)
        if key_off is not None and key_off + rows - 1 > c0:
            key = lax.broadcasted_iota(jnp.int32, st.shape, 0) + key_off
            qry = lax.broadcasted_iota(jnp.int32, st.shape, 1) + c0
            st = jnp.where(key <= qry, st, -jnp.inf)
        s_ref[slot, :rows, c0:c0 + gw] = st
        bm_ref[slot, :, c0:c0 + gw] = jnp.max(st, axis=0, keepdims=True)

    def update(chunk, g, slot, key_off=None):
        c0, c1, rows = g * gw, (g + 1) * gw, visible_rows(g, key_off)
        if rows == 0:
            return
        m_old = m_ref[:, c0:c1]
        m_new = jnp.maximum(m_old, bm_ref[slot, :, c0:c1])
        alpha = jnp.exp2(m_old - m_new)
        p = jnp.exp2(s_ref[slot, :rows, c0:c1] - m_new)
        l_ref[:, c0:c1] = alpha * l_ref[:, c0:c1] + jnp.sum(p, axis=0, keepdims=True)
        m_ref[:, c0:c1] = m_new
        pb = p.astype(BF16)
        pv = None
        for r0 in range(0, rows, tv):
            r1 = min(rows, r0 + tv)
            part = jnp.dot(vT_ref[0, 0, chunk * (tk // tv) + r0 // tv, :, :r1 - r0], pb[r0:r1, :],
                           preferred_element_type=F32)
            pv = part if pv is None else pv + part
        acc_ref[:, c0:c1] = alpha * acc_ref[:, c0:c1] + pv

    def stage(nxt_chunk, nxt_key_off, cur_chunk, cur_slot, cur_key_off=None, after=None):
        for g in groups:
            if nxt_chunk is not None:
                scores(nxt_chunk, g, 1 - cur_slot, nxt_key_off)
            update(cur_chunk, g, cur_slot, cur_key_off)
            if after is not None:
                after(g)

    def next_tile_first_chunk(g):
        scores(0, g, 0, None, q_next_ref)

    def next_head_first_chunk(g):
        scores(0, g, 0, 0, q_next_ref, k_next_ref)

    d = 2 * qi

    def diagonal_tail(look_ahead):
        stage(d + 1, tk, d, 0, 0, look_ahead)
        stage(None, None, d + 1, 1, tk)
        o_ref[0] = (acc_ref[...] * (1.0 / l_ref[...])).astype(o_ref.dtype)

    @pl.when(qi > 0)
    def _():
        def pair(p, _):
            c = 2 * p
            stage(c + 1, None, c, 0)
            stage(c + 2, None, c + 1, 1)
            return 0

        lax.fori_loop(0, qi - 1, pair, 0)
        stage(d - 1, None, d - 2, 0)
        stage(d, 0, d - 1, 1)

        @pl.when(qi < last_tile)
        def _():
            diagonal_tail(next_tile_first_chunk)

        @pl.when(qi == last_tile)
        def _():
            diagonal_tail(next_head_first_chunk)

    @pl.when(qi == 0)
    def _():
        @pl.when((pl.program_id(0) == 0) & (pl.program_id(1) == 0))
        def _():
            for g in groups:
                scores(d, g, 0, 0)

        diagonal_tail(next_tile_first_chunk)


def _attention(q, k, vT):
    b, h, s, _ = q.shape
    tq, tk = TQ_ATTN, TK_ATTN
    nkv = vT.shape[2]
    nq = s // tq

    def next_head(i, j):
        flat = jnp.minimum(i * h + j + 1, b * h - 1)
        return flat // h, flat % h

    def next_queries(i, j, t):
        ni, nj = next_head(i, j)
        wrap = t == nq - 1
        return jnp.where(wrap, ni, i), jnp.where(wrap, nj, j), jnp.where(wrap, 0, t + 1), 0

    return pl.pallas_call(
        _attn_kernel,
        grid=(b, h, nq),
        in_specs=[
            pl.BlockSpec((1, 1, tq, LANES), lambda i, j, t: (i, j, t, 0)),
            pl.BlockSpec((1, 1, tq, LANES), next_queries),
            pl.BlockSpec((1, 1, s, LANES), lambda i, j, t: (i, j, 0, 0)),
            pl.BlockSpec((1, 1, s, LANES), lambda i, j, t: next_head(i, j) + (0, 0)),
            pl.BlockSpec((1, 1, nkv, HEAD_DIM, TM_PROJ), lambda i, j, t: (i, j, 0, 0, 0)),
        ],
        out_specs=pl.BlockSpec((1, HEAD_DIM, tq), lambda i, j, t: (i, j, t)),
        out_shape=jax.ShapeDtypeStruct((b, h * HEAD_DIM, s), BF16),
        scratch_shapes=[
            pltpu.VMEM((2, tk, tq), F32),
            pltpu.VMEM((2, 1, tq), F32),
            pltpu.VMEM((1, tq), F32),
            pltpu.VMEM((1, tq), F32),
            pltpu.VMEM((HEAD_DIM, tq), F32),
        ],
        compiler_params=pltpu.CompilerParams(
            dimension_semantics=("arbitrary", "arbitrary", "arbitrary"), vmem_limit_bytes=VMEM_LIMIT),
        name="mla_attn",
    )(q, q, k, k, vT)


def _out_ln_kernel(tokT_ref, g_ref, mg_ref, h_ref, w_out_ref, lng_ref, lnb_ref, o_ref):
    sub = SUB_OUT
    n_sub = o_ref.shape[1] // sub
    v = {}

    def gate(r):
        rows = slice(r * sub, (r + 1) * sub)
        tok = tokT_ref[0, :, rows].astype(F32).T
        v["y", r] = (tok * g_ref[0, rows, :].astype(F32)).astype(BF16)

    def project(r, c):
        cols = slice(c * MXU_N, (c + 1) * MXU_N)
        v["o", r, c] = (
            jnp.dot(v["y", r], w_out_ref[:TOK_WIDTH, cols], preferred_element_type=F32)
            + jnp.dot(mg_ref[0, r * sub:(r + 1) * sub, :], w_out_ref[TOK_WIDTH:, cols], preferred_element_type=F32))

    def finish(r):
        rows = slice(r * sub, (r + 1) * sub)
        o = jnp.concatenate([v["o", r, c] for c in range(D_MODEL // MXU_N)], axis=1)
        o_ref[0, rows, :] = _layer_norm(ALPHA * h_ref[0, rows, :] + o, lng_ref[...], lnb_ref[...])

    def matmul_pieces(r):
        return [functools.partial(gate, r)] + [functools.partial(project, r, c) for c in range(D_MODEL // MXU_N)]

    _emit_round_robin(matmul_pieces(0))
    for r in range(n_sub):
        _emit_round_robin(matmul_pieces(r + 1) if r + 1 < n_sub else [], [functools.partial(finish, r)])


def _out_ln(tokT, g, mg, h, w_out, ln_g, ln_b):
    b, s, _ = h.shape
    tm = TM_OUT
    return pl.pallas_call(
        _out_ln_kernel,
        grid=(b, s // tm),
        in_specs=[
            pl.BlockSpec((1, TOK_WIDTH, tm), lambda i, j: (i, 0, j)),
            pl.BlockSpec((1, tm, TOK_WIDTH), lambda i, j: (i, j, 0)),
            pl.BlockSpec((1, tm, MEM_WIDTH), lambda i, j: (i, j, 0)),
            pl.BlockSpec((1, tm, D_MODEL), lambda i, j: (i, j, 0)),
            _const_spec((MIX_WIDTH, D_MODEL)),
            _const_spec((1, D_MODEL)),
            _const_spec((1, D_MODEL)),
        ],
        out_specs=pl.BlockSpec((1, tm, D_MODEL), lambda i, j: (i, j, 0)),
        out_shape=jax.ShapeDtypeStruct((b, s, D_MODEL), F32),
        compiler_params=pltpu.CompilerParams(
            dimension_semantics=("arbitrary", "arbitrary"), vmem_limit_bytes=VMEM_LIMIT),
        name="mla_out_ln",
    )(tokT, g, mg, h, w_out, ln_g, ln_b)


_B_U = (0, TOK_WIDTH)
_B_GATE = (_B_U[1], _B_U[1] + MIX_WIDTH)
_B_QM = (_B_GATE[1], _B_GATE[1] + MEM_WIDTH)
_B_WIDTH = _B_QM[1]
GATE_TILE = 256


def _linear_scan(a, b):
    n = a.shape[0]
    row = lax.broadcasted_iota(jnp.int32, a.shape, 0)
    d = 1
    while d < n:
        live = row >= d
        b = a * jnp.where(live, pltpu.roll(b, d, 0), 0.0) + b
        if 2 * d < n:
            a = a * jnp.where(live, pltpu.roll(a, d, 0), 1.0)
        d *= 2
    return b


def _emit_round_robin(*piece_lists):
    lists = [list(p) for p in piece_lists if p]
    longest = max((len(p) for p in lists), default=0)
    for k in range(longest):
        for p in lists:
            lo, hi = (k * len(p)) // longest, ((k + 1) * len(p)) // longest
            for piece in p[lo:hi]:
                piece()


def _lru_layer_kernel(h_ref, w_in_ref, conv_w_ref, conv_b_ref, w_r_ref, b_r_ref, w_i_ref, b_i_ref, lam_ref,
                      kbd_ref, vbd_ref, w_out_ref, lng_ref, lnb_ref, o_ref,
                      ebuf_ref, tail_ref, carry_ref, perm_in_ref, perm_out_ref):
    tm, sl, ct = SUB_LRU, SUBLANES, GATE_TILE
    n = tm // sl
    halo = (CONV_W - 1) * sl
    lane_tiles = D_MODEL // LANES
    n_sub = TM_LRU // SUB_LRU
    step = pl.program_id(1)

    @pl.when(step == 0)
    def _():
        tail_ref[...] = jnp.zeros((halo, TOK_WIDTH), F32)
        carry_ref[...] = jnp.zeros((sl, TOK_WIDTH), F32)

    sub8 = lax.broadcasted_iota(jnp.int32, (sl, ct), 0)

    def segment_rows(j):
        s, i0 = divmod(sl * j, n)
        return pl.ds(sl * i0 + s, sl, stride=sl)

    def from_previous_segment(prev_tile, cur_tile):
        return jnp.where(sub8 == 0, pltpu.roll(prev_tile, 1, 0), pltpu.roll(cur_tile, 1, 0))

    tails = [tail_ref[:, c * ct:(c + 1) * ct] for c in range(TOK_WIDTH // ct)]
    carries = [carry_ref[:, c * ct:(c + 1) * ct] for c in range(TOK_WIDTH // ct)]

    def sub_tile_phases(r):
        row0 = r * tm
        v = {}

        def load():
            for j in range(n):
                rows = h_ref[0, row0 + j * sl:row0 + (j + 1) * sl, :]
                for c in range(lane_tiles):
                    perm_in_ref[r, c, segment_rows(j), :] = rows[:, c * LANES:(c + 1) * LANES]
            v["hin"] = jnp.concatenate([perm_in_ref[r, c] for c in range(lane_tiles)], axis=1)
            v["hb"] = v["hin"].astype(BF16)

        def project(k):
            z = jnp.dot(v["hb"], w_in_ref[:, k * LRU_DOT_N:(k + 1) * LRU_DOT_N], preferred_element_type=F32)
            for c in range(LRU_DOT_N // MXU_N):
                v["z", k * (LRU_DOT_N // MXU_N) + c] = z[:, c * MXU_N:(c + 1) * MXU_N]

        def recurrence(c):
            cols = slice(c * ct, (c + 1) * ct)
            u = v["z", _B_U[0] // MXU_N + c]
            for k in range(CONV_W - 1):
                cur = u[tm - halo + k * sl:tm - halo + (k + 1) * sl, :]
                ebuf_ref[r, k * sl:(k + 1) * sl, cols] = from_previous_segment(tails[c][k * sl:(k + 1) * sl, :], cur)
            ebuf_ref[r, halo:halo + tm, cols] = u
            tails[c] = u[tm - halo:, :]
            xc = conv_b_ref[:, cols] + u * conv_w_ref[CONV_W - 1:CONV_W, cols]
            for back in range(1, CONV_W):
                tap = CONV_W - 1 - back
                xc = xc + ebuf_ref[r, halo - back * sl:halo - back * sl + tm, cols] * conv_w_ref[tap:tap + 1, cols]

            xcb = xc.astype(BF16)
            gate_r = _sigmoid(jnp.dot(xcb, w_r_ref[c], preferred_element_type=F32) + b_r_ref[:, cols])
            gate_i = _sigmoid(jnp.dot(xcb, w_i_ref[c], preferred_element_type=F32) + b_i_ref[:, cols])

            neg_lam = -lam_ref[:, cols]
            softplus = jnp.maximum(neg_lam, 0.0) + jnp.log1p(jnp.exp(-jnp.abs(neg_lam)))
            log_a = (-LRU_C * softplus) * gate_r
            a = jnp.exp(log_a)
            one_minus_a2 = -jnp.tanh(log_a) * (a * a + 1.0)
            root = jnp.where(one_minus_a2 > 0.0, one_minus_a2 * lax.rsqrt(one_minus_a2), 0.0)
            b = root * (gate_i * xc)

            h_loc, a_run = [b[0:sl, :]], [a[0:sl, :]]
            for g in range(1, n):
                ag = a[g * sl:(g + 1) * sl, :]
                h_loc.append(ag * h_loc[-1] + b[g * sl:(g + 1) * sl, :])
                a_run.append(ag * a_run[-1])
            state_in = pltpu.roll(carries[c], 1, 0)
            seg_b = jnp.where(sub8 == 0, a_run[-1] * state_in + h_loc[-1], h_loc[-1])
            seg_end = _linear_scan(a_run[-1], seg_b)
            start = from_previous_segment(carries[c], seg_end)
            carries[c] = seg_end
            v["hs", c] = jnp.concatenate([h_loc[g] + a_run[g] * start for g in range(n)], axis=0)

        def gate(c):
            g = _silu(v["z", _B_GATE[0] // MXU_N + c])
            if c < TOK_WIDTH // ct:
                v["y", c] = (v["hs", c] * g).astype(BF16)
            else:
                mem_out = _memory_attention(v["z", _B_QM[0] // MXU_N], kbd_ref.at[0, 0], vbd_ref.at[0, 0])
                v["y", c] = (mem_out * g).astype(BF16)

        def out_project(k):
            if k == 0:
                v["yb"] = jnp.concatenate([v["y", c] for c in range(MIX_WIDTH // ct)], axis=1)
            v["o", k] = jnp.dot(v["yb"], w_out_ref[:, k * LRU_DOT_N:(k + 1) * LRU_DOT_N], preferred_element_type=F32)

        def finish():
            o = jnp.concatenate([v["o", k] for k in range(D_MODEL // LRU_DOT_N)], axis=1)
            out = _layer_norm(ALPHA * v["hin"] + o, lng_ref[...], lnb_ref[...])
            for c in range(lane_tiles):
                perm_out_ref[r, c] = out[:, c * LANES:(c + 1) * LANES]
            for j in range(n):
                o_ref[0, row0 + j * sl:row0 + (j + 1) * sl, :] = jnp.concatenate(
                    [perm_out_ref[r, c, segment_rows(j), :] for c in range(lane_tiles)], axis=1)

        n_tok, n_mix = TOK_WIDTH // ct, MIX_WIDTH // ct
        phase_a = [load] + [functools.partial(project, k) for k in range(_B_WIDTH // LRU_DOT_N)]
        phase_b = []
        for c in range(n_mix):
            if c < n_tok:
                phase_b.append(functools.partial(recurrence, c))
            phase_b.append(functools.partial(gate, c))
        phase_cd = [functools.partial(out_project, k) for k in range(D_MODEL // LRU_DOT_N)] + [finish]
        return phase_a, phase_b, phase_cd

    phases = [sub_tile_phases(r) for r in range(n_sub)]
    _emit_round_robin(phases[0][0])
    for r in range(n_sub + 1):
        vector_heavy = phases[r][1] if r < n_sub else []
        matmul_heavy = (phases[r + 1][0] if r + 1 < n_sub else []) + (phases[r - 1][2] if r >= 1 else [])
        _emit_round_robin(matmul_heavy, vector_heavy)

    for c in range(TOK_WIDTH // ct):
        tail_ref[:, c * ct:(c + 1) * ct] = tails[c]
        carry_ref[:, c * ct:(c + 1) * ct] = carries[c]


def _lru_layer(h, w_in, conv_w, conv_b, w_r, b_r, w_i, b_i, lam, kbd, vbd, w_out, ln_g, ln_b):
    b, s, _ = h.shape
    tm = TM_LRU
    n_gate_tiles = TOK_WIDTH // GATE_TILE
    n_sub = TM_LRU // SUB_LRU
    return pl.pallas_call(
        _lru_layer_kernel,
        grid=(b, s // tm),
        in_specs=[
            pl.BlockSpec((1, tm, D_MODEL), lambda i, j: (i, j, 0)),
            _const_spec((D_MODEL, _B_WIDTH)),
            _const_spec((CONV_W, TOK_WIDTH)),
            _const_spec((1, TOK_WIDTH)),
            _const_spec((n_gate_tiles, GATE_TILE, GATE_TILE)),
            _const_spec((1, TOK_WIDTH)),
            _const_spec((n_gate_tiles, GATE_TILE, GATE_TILE)),
            _const_spec((1, TOK_WIDTH)),
            _const_spec((1, TOK_WIDTH)),
            pl.BlockSpec((1, 1, MEM_WIDTH, N_MEM_HEADS * MEM_LEN), lambda i, j: (1, i, 0, 0)),
            pl.BlockSpec((1, 1, N_MEM_HEADS * MEM_LEN, MEM_WIDTH), lambda i, j: (1, i, 0, 0)),
            _const_spec((MIX_WIDTH, D_MODEL)),
            _const_spec((1, D_MODEL)),
            _const_spec((1, D_MODEL)),
        ],
        out_specs=pl.BlockSpec((1, tm, D_MODEL), lambda i, j: (i, j, 0)),
        out_shape=jax.ShapeDtypeStruct((b, s, D_MODEL), F32),
        scratch_shapes=[
            pltpu.VMEM((n_sub, SUB_LRU + (CONV_W - 1) * SUBLANES, TOK_WIDTH), F32),
            pltpu.VMEM(((CONV_W - 1) * SUBLANES, TOK_WIDTH), F32),
            pltpu.VMEM((SUBLANES, TOK_WIDTH), F32),
            pltpu.VMEM((n_sub, D_MODEL // LANES, SUB_LRU, LANES), F32),
            pltpu.VMEM((n_sub, D_MODEL // LANES, SUB_LRU, LANES), F32),
        ],
        compiler_params=pltpu.CompilerParams(
            dimension_semantics=("arbitrary", "arbitrary"), vmem_limit_bytes=VMEM_LIMIT),
        name="lru_layer",
    )(h, w_in, conv_w, conv_b, w_r, b_r, w_i, b_i, lam, kbd, vbd, w_out, ln_g, ln_b)


def _block_diag_tiles(w):
    per_tile = GATE_TILE // HEAD_DIM
    w = w.reshape(TOK_WIDTH // GATE_TILE, per_tile, HEAD_DIM, HEAD_DIM)
    eye = jnp.eye(per_tile, dtype=w.dtype)
    t = w[:, :, :, None, :] * eye[None, :, None, :, None]
    return t.reshape(TOK_WIDTH // GATE_TILE, GATE_TILE, GATE_TILE)


def _rope_frequencies(tm):
    inv_freq = ROPE_THETA ** (-np.arange(HALF_ROPE, dtype=np.float32) / np.float32(HALF_ROPE))
    return np.ascontiguousarray(np.broadcast_to(inv_freq[:, None], (HALF_ROPE, tm)))


def _head_lanes(w, nope, t1, t2):
    zero = w.shape[-1]
    idx = np.full((LANES,), zero, np.int32)
    head = LANES // 2 - HALF_ROPE
    if t1 is not None:
        idx[:HALF_ROPE] = t1 + np.arange(HALF_ROPE)
        idx[LANES // 2:LANES // 2 + HALF_ROPE] = t2 + np.arange(HALF_ROPE)
    if nope is not None:
        idx[HALF_ROPE:LANES // 2] = nope + np.arange(head)
        idx[LANES // 2 + HALF_ROPE:LANES // 2 + HALF_ROPE + QK_NOPE - head] = nope + head + np.arange(QK_NOPE - head)
    w_ext = jnp.concatenate([w, jnp.zeros(w.shape[:-1] + (1,), w.dtype)], axis=-1)
    return jnp.take(w_ext, jnp.asarray(idx), axis=-1)


def kernel(x, mem, positions, mla_w_in, mla_q_norm, mla_w_uq, mla_kv_norm, mla_w_ukv, lru_w_in, lru_conv_w,
           lru_conv_b, lru_w_rgate, lru_b_rgate, lru_w_igate, lru_b_igate, lru_lambda, w_mem_kv, w_out, ln_g, ln_b):
    b, s, _ = x.shape
    assert s % TQ_ATTN == 0 and s % TM_LRU == 0 and s % TM_OUT == 0 and s % TM_PROJ == 0 and TK_ATTN % TM_PROJ == 0

    kbd, vbd = _mem_kv(mem, w_mem_kv)

    w_in = mla_w_in[0]
    o_q, o_kv, o_kr, o_gate = Q_LORA, Q_LORA + KV_LORA, Q_LORA + KV_LORA + QK_ROPE, Q_LORA + KV_LORA + QK_ROPE + MIX_WIDTH
    w_kr = _head_lanes(w_in[:, o_kv:o_kr], None, 0, HALF_ROPE)
    w_in_a = jnp.concatenate(
        [w_in[:, :o_q], w_in[:, o_q:o_kv], w_kr, w_in[:, o_kr:o_gate], w_in[:, o_gate:]], axis=1).astype(BF16)
    w_uq = _head_lanes(mla_w_uq[0].reshape(Q_LORA, N_TOK_HEADS, QK_DIM), 0, QK_NOPE, QK_NOPE + HALF_ROPE)
    w_uq = w_uq.reshape(Q_LORA, N_TOK_HEADS * LANES).astype(BF16)
    w_ukv = mla_w_ukv[0].reshape(KV_LORA, N_TOK_HEADS, QK_NOPE + HEAD_DIM)
    w_k = _head_lanes(w_ukv[:, :, :QK_NOPE], 0, None, None)
    w_k = w_k.reshape(KV_LORA, N_TOK_HEADS * LANES).astype(BF16)
    w_vT = w_ukv[:, :, QK_NOPE:].reshape(KV_LORA, TOK_WIDTH).T.astype(BF16)
    pos = positions.astype(F32)[:, None, :]
    w_out_b = w_out.astype(BF16)

    q, k, vT, g, mg = _mla_proj(x, pos, w_in_a, mla_q_norm[0][None], mla_kv_norm[0][None], w_uq, w_k, w_vT,
                                jnp.asarray(_rope_frequencies(TM_PROJ)), kbd, vbd)
    tokT = _attention(q, k, vT)
    h1 = _out_ln(tokT, g, mg, x, w_out_b[0], ln_g[0][None], ln_b[0][None])

    return _lru_layer(h1, lru_w_in[0].astype(BF16), lru_conv_w[0], lru_conv_b[0][None],
                      _block_diag_tiles(lru_w_rgate[0]).astype(BF16), lru_b_rgate[0][None],
                      _block_diag_tiles(lru_w_igate[0]).astype(BF16), lru_b_igate[0][None],
                      lru_lambda[0][None], kbd, vbd, w_out_b[1], ln_g[1][None], ln_b[1][None])
```

```python
import functools
import math

import jax
import jax.numpy as jnp
import numpy as np
from jax import lax
from jax.experimental import pallas as pl
from jax.experimental.pallas import tpu as pltpu

F32 = jnp.float32
BF16 = jnp.bfloat16

D_MODEL = 1024
DEPTH = 2
MEM_LEN = 256
HEAD_DIM = 64
N_MEM_HEADS = 4
N_TOK_HEADS = 12
TOK_WIDTH = N_TOK_HEADS * HEAD_DIM
MEM_WIDTH = N_MEM_HEADS * HEAD_DIM
MIX_WIDTH = TOK_WIDTH + MEM_WIDTH
Q_LORA = 384
KV_LORA = 256
QK_NOPE = 64
QK_ROPE = 32
QK_DIM = QK_NOPE + QK_ROPE
ROPE_THETA = 10000.0
CONV_W = 4
LRU_C = 8.0
ALPHA = (2.0 * DEPTH) ** 0.25
NORM_EPS = 1e-6

MXU_N = 256
LANES = 128
SUBLANES = 8
HALF_ROPE = QK_ROPE // 2
LOG2E = math.log2(math.e)
Q_SCALE = LOG2E / math.sqrt(QK_DIM)

TM_PROJ = 1024
SUB_PROJ = 256
TK_ATTN = 1024
TQ_ATTN = 2 * TK_ATTN
GW_ATTN = 256
TM_OUT = 1024
SUB_OUT = 256
TM_LRU = 1024
SUB_LRU = 256
VMEM_LIMIT = 48 * 1024 * 1024

_NT = (((1,), (1,)), ((), ()))


def _const_spec(shape):
    nd = len(shape)
    return pl.BlockSpec(shape, lambda *_: (0,) * nd, pipeline_mode=pl.Buffered(1))


def _dot_cols(x, w_ref, lo, hi):
    parts = [jnp.dot(x, w_ref[:, c:min(c + MXU_N, hi)], preferred_element_type=F32) for c in range(lo, hi, MXU_N)]
    return parts[0] if len(parts) == 1 else jnp.concatenate(parts, axis=-1)


def _emit_round_robin(*piece_lists):
    lists = [list(p) for p in piece_lists if p]
    longest = max((len(p) for p in lists), default=0)
    for k in range(longest):
        for p in lists:
            lo, hi = (k * len(p)) // longest, ((k + 1) * len(p)) // longest
            for piece in p[lo:hi]:
                piece()


def _sigmoid(t):
    return 0.5 * jnp.tanh(0.5 * t) + 0.5


def _silu(t):
    h = 0.5 * t
    return h * jnp.tanh(h) + h


def _rms_norm(t, g):
    return t * lax.rsqrt(jnp.mean(t * t, axis=-1, keepdims=True) + NORM_EPS) * g


def _layer_norm(t, g, b):
    mu = jnp.mean(t, axis=-1, keepdims=True)
    c = t - mu
    var = jnp.mean(c * c, axis=-1, keepdims=True)
    return c * lax.rsqrt(var + NORM_EPS) * g + b


def _memory_attention(qm, kbd_ref, vbd_ref):
    s = _dot_cols(qm.astype(BF16), kbd_ref, 0, N_MEM_HEADS * MEM_LEN)
    ps = []
    for h in range(N_MEM_HEADS):
        sh = s[:, h * MEM_LEN:(h + 1) * MEM_LEN]
        e = jnp.exp2(sh - jnp.max(sh, axis=-1, keepdims=True))
        ps.append((e * (1.0 / jnp.sum(e, axis=-1, keepdims=True))).astype(BF16))
    p = jnp.concatenate(ps, axis=-1)
    return jnp.dot(p, vbd_ref[...], preferred_element_type=F32)


def _mem_kv_kernel(mem_ref, wkT_ref, wv_ref, kbd_ref, vbd_ref):
    mb = mem_ref[0].astype(BF16)
    kT = lax.dot_general(wkT_ref[0].astype(BF16), mb, _NT, preferred_element_type=F32)
    kT = (kT * (LOG2E / math.sqrt(HEAD_DIM))).astype(BF16)
    v = jnp.dot(mb, wv_ref[0].astype(BF16), preferred_element_type=F32).astype(BF16)
    row = lax.broadcasted_iota(jnp.int32, (MEM_WIDTH, MEM_LEN), 0)
    col = lax.broadcasted_iota(jnp.int32, (MEM_LEN, MEM_WIDTH), 1)
    zero = jnp.zeros((), BF16)
    for h in range(N_MEM_HEADS):
        lo, hi = h * HEAD_DIM, (h + 1) * HEAD_DIM
        kbd_ref[0, 0, :, h * MEM_LEN:(h + 1) * MEM_LEN] = jnp.where((row >= lo) & (row < hi), kT, zero)
        vbd_ref[0, 0, h * MEM_LEN:(h + 1) * MEM_LEN, :] = jnp.where((col >= lo) & (col < hi), v, zero)


def _mem_kv(mem, w_mem_kv):
    b = mem.shape[0]
    wkT = jnp.swapaxes(w_mem_kv[:, :, :MEM_WIDTH], 1, 2)
    wv = w_mem_kv[:, :, MEM_WIDTH:]
    return pl.pallas_call(
        _mem_kv_kernel,
        grid=(DEPTH, b),
        in_specs=[
            pl.BlockSpec((1, MEM_LEN, D_MODEL), lambda l, i: (i, 0, 0)),
            pl.BlockSpec((1, MEM_WIDTH, D_MODEL), lambda l, i: (l, 0, 0)),
            pl.BlockSpec((1, D_MODEL, MEM_WIDTH), lambda l, i: (l, 0, 0)),
        ],
        out_specs=[
            pl.BlockSpec((1, 1, MEM_WIDTH, N_MEM_HEADS * MEM_LEN), lambda l, i: (l, i, 0, 0)),
            pl.BlockSpec((1, 1, N_MEM_HEADS * MEM_LEN, MEM_WIDTH), lambda l, i: (l, i, 0, 0)),
        ],
        out_shape=[
            jax.ShapeDtypeStruct((DEPTH, b, MEM_WIDTH, N_MEM_HEADS * MEM_LEN), BF16),
            jax.ShapeDtypeStruct((DEPTH, b, N_MEM_HEADS * MEM_LEN, MEM_WIDTH), BF16),
        ],
        name="mem_kv",
    )(mem, wkT, wv)


_A_CQ = (0, Q_LORA)
_A_CKV = (_A_CQ[1], _A_CQ[1] + KV_LORA)
_A_KR = (_A_CKV[1], _A_CKV[1] + LANES)
_A_GATE = (_A_KR[1], _A_KR[1] + MIX_WIDTH)
_A_QM = (_A_GATE[1], _A_GATE[1] + MEM_WIDTH)
_A_WIDTH = _A_QM[1]


def _mla_proj_kernel(x_ref, pos_ref, w_in_ref, qn_ref, kvn_ref, w_uq_ref, w_k_ref, w_vT_ref, rope_ref,
                     kbd_ref, vbd_ref, q_ref, k_ref, vT_ref, g_ref, mg_ref):
    sub = SUB_PROJ
    n_sub = x_ref.shape[1] // sub
    heads_per_tile = MXU_N // LANES

    ang = rope_ref[...] * pos_ref[0]
    cos_t, sin_t = jnp.cos(ang), jnp.sin(ang)
    gap = LANES // 2 - HALF_ROPE
    ones, zeros = jnp.ones((gap, ang.shape[1]), F32), jnp.zeros((gap, ang.shape[1]), F32)
    cos_all = jnp.concatenate([cos_t, ones, cos_t, ones], axis=0).T
    sin_all = jnp.concatenate([-sin_t, zeros, sin_t, zeros], axis=0).T

    def sub_tile_pieces(r):
        rows = slice(r * sub, (r + 1) * sub)
        cos, sin = cos_all[rows, :], sin_all[rows, :]
        v = {}

        def rope(t):
            return t * cos + pltpu.roll(t, LANES // 2, 1) * sin

        def load():
            v["xb"] = x_ref[0, rows, :].astype(BF16)

        def project(c):
            v["z", c] = jnp.dot(v["xb"], w_in_ref[:, c * MXU_N:(c + 1) * MXU_N], preferred_element_type=F32)

        def seg(s):
            z = jnp.concatenate([v["z", c] for c in range(s[0] // MXU_N, (s[1] - 1) // MXU_N + 1)], axis=1)
            lo = s[0] - (s[0] // MXU_N) * MXU_N
            return z[:, lo:lo + s[1] - s[0]]

        def q_norm():
            v["cqn"] = _rms_norm(seg(_A_CQ), qn_ref[...]).astype(BF16)

        def q_heads(t):
            q = jnp.dot(v["cqn"], w_uq_ref[:, t * MXU_N:(t + 1) * MXU_N], preferred_element_type=F32)
            for i in range(heads_per_tile):
                q_ref[0, t * heads_per_tile + i, rows, :] = (
                    rope(q[:, i * LANES:(i + 1) * LANES]) * Q_SCALE).astype(BF16)

        def kv_norm():
            v["ckvn"] = _rms_norm(seg(_A_CKV), kvn_ref[...]).astype(BF16)
            v["k_rope"] = rope(seg(_A_KR))

        def k_heads(t):
            k = jnp.dot(v["ckvn"], w_k_ref[:, t * MXU_N:(t + 1) * MXU_N], preferred_element_type=F32)
            for i in range(heads_per_tile):
                k_ref[0, t * heads_per_tile + i, rows, :] = (k[:, i * LANES:(i + 1) * LANES] + v["k_rope"]).astype(BF16)

        def values():
            vT = lax.dot_general(w_vT_ref[...], v["ckvn"], _NT, preferred_element_type=F32)
            for h in range(N_TOK_HEADS):
                vT_ref[0, h, 0, :, rows] = vT[h * HEAD_DIM:(h + 1) * HEAD_DIM, :].astype(BF16)

        def gates():
            g = _silu(seg(_A_GATE))
            g_ref[0, rows, :] = g[:, :TOK_WIDTH].astype(g_ref.dtype)
            mem_out = _memory_attention(seg(_A_QM), kbd_ref.at[0, 0], vbd_ref.at[0, 0])
            mg_ref[0, rows, :] = (mem_out * g[:, TOK_WIDTH:]).astype(mg_ref.dtype)

        n_tiles = N_TOK_HEADS // heads_per_tile
        matmul_heavy = [load] + [functools.partial(project, c) for c in range(_A_WIDTH // MXU_N)]
        vector_heavy = ([q_norm] + [functools.partial(q_heads, t) for t in range(n_tiles)] + [kv_norm]
                        + [functools.partial(k_heads, t) for t in range(n_tiles)] + [values, gates])
        return matmul_heavy, vector_heavy

    pieces = [sub_tile_pieces(r) for r in range(n_sub)]
    _emit_round_robin(pieces[0][0])
    for r in range(n_sub):
        _emit_round_robin(pieces[r + 1][0] if r + 1 < n_sub else [], pieces[r][1])


def _mla_proj(x, pos, w_in_a, qn, kvn, w_uq, w_k, w_vT, rope_c, kbd, vbd):
    b, s, _ = x.shape
    tm = TM_PROJ
    nt = s // tm
    return pl.pallas_call(
        _mla_proj_kernel,
        grid=(b, nt),
        in_specs=[
            pl.BlockSpec((1, tm, D_MODEL), lambda i, j: (i, j, 0)),
            pl.BlockSpec((1, 1, tm), lambda i, j: (i, 0, j)),
            _const_spec((D_MODEL, _A_WIDTH)),
            _const_spec((1, Q_LORA)),
            _const_spec((1, KV_LORA)),
            _const_spec((Q_LORA, N_TOK_HEADS * LANES)),
            _const_spec((KV_LORA, N_TOK_HEADS * LANES)),
            _const_spec((TOK_WIDTH, KV_LORA)),
            _const_spec((HALF_ROPE, tm)),
            pl.BlockSpec((1, 1, MEM_WIDTH, N_MEM_HEADS * MEM_LEN), lambda i, j: (0, i, 0, 0)),
            pl.BlockSpec((1, 1, N_MEM_HEADS * MEM_LEN, MEM_WIDTH), lambda i, j: (0, i, 0, 0)),
        ],
        out_specs=[
            pl.BlockSpec((1, N_TOK_HEADS, tm, LANES), lambda i, j: (i, 0, j, 0)),
            pl.BlockSpec((1, N_TOK_HEADS, tm, LANES), lambda i, j: (i, 0, j, 0)),
            pl.BlockSpec((1, N_TOK_HEADS, 1, HEAD_DIM, tm), lambda i, j: (i, 0, j, 0, 0)),
            pl.BlockSpec((1, tm, TOK_WIDTH), lambda i, j: (i, j, 0)),
            pl.BlockSpec((1, tm, MEM_WIDTH), lambda i, j: (i, j, 0)),
        ],
        out_shape=[
            jax.ShapeDtypeStruct((b, N_TOK_HEADS, s, LANES), BF16),
            jax.ShapeDtypeStruct((b, N_TOK_HEADS, s, LANES), BF16),
            jax.ShapeDtypeStruct((b, N_TOK_HEADS, nt, HEAD_DIM, tm), BF16),
            jax.ShapeDtypeStruct((b, s, TOK_WIDTH), BF16),
            jax.ShapeDtypeStruct((b, s, MEM_WIDTH), BF16),
        ],
        compiler_params=pltpu.CompilerParams(
            dimension_semantics=("arbitrary", "arbitrary"), vmem_limit_bytes=VMEM_LIMIT),
        name="mla_proj",
    )(x, pos, w_in_a, qn, kvn, w_uq, w_k, w_vT, rope_c, kbd, vbd)


def _attn_kernel(q_ref, q_next_ref, k_ref, vT_ref, o_ref, s_ref, bm_ref, m_ref, l_ref, acc_ref):
    tq, tk, gw, tv = TQ_ATTN, TK_ATTN, GW_ATTN, TM_PROJ
    groups = range(tq // gw)
    qi = pl.program_id(2)
    last_tile = pl.num_programs(2) - 1
    m_ref[...] = jnp.full((1, tq), -jnp.inf, F32)
    l_ref[...] = jnp.zeros((1, tq), F32)
    acc_ref[...] = jnp.zeros((HEAD_DIM, tq), F32)

    def visible_rows(g, key_off):
        return tk if key_off is None else max(0, min(tk, (g + 1) * gw - key_off))

    def scores(chunk, g, slot, key_off=None, queries=q_ref):
        c0, rows = g * gw, visible_rows(g, key_off)
        if rows == 0:
            return
        start = pl.multiple_of(chunk * tk, tk)
        kt = k_ref[0, 0, pl.ds(start, rows), :]
        st = lax.dot_general(kt, queries[0, 0, c0:c0 + gw, :], _NT, preferred_element_type=F32)
        if key_off is not None and key_off + rows - 1 > c0:
            key = lax.broadcasted_iota(jnp.int32, st.shape, 0) + key_off
            qry = lax.broadcasted_iota(jnp.int32, st.shape, 1) + c0
            st = jnp.where(key <= qry, st, -jnp.inf)
        s_ref[slot, :rows, c0:c0 + gw] = st
        bm_ref[slot, :, c0:c0 + gw] = jnp.max(st, axis=0, keepdims=True)

    def update(chunk, g, slot, key_off=None):
        c0, c1, rows = g * gw, (g + 1) * gw, visible_rows(g, key_off)
        if rows == 0:
            return
        m_old = m_ref[:, c0:c1]
        m_new = jnp.maximum(m_old, bm_ref[slot, :, c0:c1])
        alpha = jnp.exp2(m_old - m_new)
        p = jnp.exp2(s_ref[slot, :rows, c0:c1] - m_new)
        l_ref[:, c0:c1] = alpha * l_ref[:, c0:c1] + jnp.sum(p, axis=0, keepdims=True)
        m_ref[:, c0:c1] = m_new
        pb = p.astype(BF16)
        pv = None
        for r0 in range(0, rows, tv):
            r1 = min(rows, r0 + tv)
            part = jnp.dot(vT_ref[0, 0, chunk * (tk // tv) + r0 // tv, :, :r1 - r0], pb[r0:r1, :],
                           preferred_element_type=F32)
            pv = part if pv is None else pv + part
        acc_ref[:, c0:c1] = alpha * acc_ref[:, c0:c1] + pv

    def stage(nxt_chunk, nxt_key_off, cur_chunk, cur_slot, cur_key_off=None, after=None):
        for g in groups:
            if nxt_chunk is not None:
                scores(nxt_chunk, g, 1 - cur_slot, nxt_key_off)
            update(cur_chunk, g, cur_slot, cur_key_off)
            if after is not None:
                after(g)

    def next_tile_first_chunk(g):
        scores(0, g, 0, None, q_next_ref)

    d = 2 * qi

    def diagonal_tail(look_ahead):
        stage(d + 1, tk, d, 0, 0, next_tile_first_chunk if look_ahead else None)
        stage(None, None, d + 1, 1, tk)
        o_ref[0] = (acc_ref[...] * (1.0 / l_ref[...])).astype(o_ref.dtype)

    @pl.when(qi > 0)
    def _():
        def pair(p, _):
            c = 2 * p
            stage(c + 1, None, c, 0)
            stage(c + 2, None, c + 1, 1)
            return 0

        lax.fori_loop(0, qi - 1, pair, 0)
        stage(d - 1, None, d - 2, 0)
        stage(d, 0, d - 1, 1)

        @pl.when(qi < last_tile)
        def _():
            diagonal_tail(True)

        @pl.when(qi == last_tile)
        def _():
            diagonal_tail(False)

    @pl.when(qi == 0)
    def _():
        for g in groups:
            scores(d, g, 0, 0)
        diagonal_tail(True)


def _attention(q, k, vT):
    b, h, s, _ = q.shape
    tq, tk = TQ_ATTN, TK_ATTN
    nkv = vT.shape[2]
    return pl.pallas_call(
        _attn_kernel,
        grid=(b, h, s // tq),
        in_specs=[
            pl.BlockSpec((1, 1, tq, LANES), lambda i, j, t: (i, j, t, 0)),
            pl.BlockSpec((1, 1, tq, LANES), lambda i, j, t: (i, j, jnp.minimum(t + 1, s // tq - 1), 0)),
            pl.BlockSpec((1, 1, s, LANES), lambda i, j, t: (i, j, 0, 0)),
            pl.BlockSpec((1, 1, nkv, HEAD_DIM, TM_PROJ), lambda i, j, t: (i, j, 0, 0, 0)),
        ],
        out_specs=pl.BlockSpec((1, HEAD_DIM, tq), lambda i, j, t: (i, j, t)),
        out_shape=jax.ShapeDtypeStruct((b, h * HEAD_DIM, s), BF16),
        scratch_shapes=[
            pltpu.VMEM((2, tk, tq), F32),
            pltpu.VMEM((2, 1, tq), F32),
            pltpu.VMEM((1, tq), F32),
            pltpu.VMEM((1, tq), F32),
            pltpu.VMEM((HEAD_DIM, tq), F32),
        ],
        compiler_params=pltpu.CompilerParams(
            dimension_semantics=("arbitrary", "arbitrary", "arbitrary"), vmem_limit_bytes=VMEM_LIMIT),
        name="mla_attn",
    )(q, q, k, vT)


def _out_ln_kernel(tokT_ref, g_ref, mg_ref, h_ref, w_out_ref, lng_ref, lnb_ref, o_ref):
    sub = SUB_OUT
    n_sub = o_ref.shape[1] // sub
    v = {}

    def gate(r):
        rows = slice(r * sub, (r + 1) * sub)
        tok = tokT_ref[0, :, rows].astype(F32).T
        v["y", r] = (tok * g_ref[0, rows, :].astype(F32)).astype(BF16)

    def project(r, c):
        cols = slice(c * MXU_N, (c + 1) * MXU_N)
        v["o", r, c] = (
            jnp.dot(v["y", r], w_out_ref[:TOK_WIDTH, cols], preferred_element_type=F32)
            + jnp.dot(mg_ref[0, r * sub:(r + 1) * sub, :], w_out_ref[TOK_WIDTH:, cols], preferred_element_type=F32))

    def finish(r):
        rows = slice(r * sub, (r + 1) * sub)
        o = jnp.concatenate([v["o", r, c] for c in range(D_MODEL // MXU_N)], axis=1)
        o_ref[0, rows, :] = _layer_norm(ALPHA * h_ref[0, rows, :] + o, lng_ref[...], lnb_ref[...])

    def matmul_pieces(r):
        return [functools.partial(gate, r)] + [functools.partial(project, r, c) for c in range(D_MODEL // MXU_N)]

    _emit_round_robin(matmul_pieces(0))
    for r in range(n_sub):
        _emit_round_robin(matmul_pieces(r + 1) if r + 1 < n_sub else [], [functools.partial(finish, r)])


def _out_ln(tokT, g, mg, h, w_out, ln_g, ln_b):
    b, s, _ = h.shape
    tm = TM_OUT
    return pl.pallas_call(
        _out_ln_kernel,
        grid=(b, s // tm),
        in_specs=[
            pl.BlockSpec((1, TOK_WIDTH, tm), lambda i, j: (i, 0, j)),
            pl.BlockSpec((1, tm, TOK_WIDTH), lambda i, j: (i, j, 0)),
            pl.BlockSpec((1, tm, MEM_WIDTH), lambda i, j: (i, j, 0)),
            pl.BlockSpec((1, tm, D_MODEL), lambda i, j: (i, j, 0)),
            _const_spec((MIX_WIDTH, D_MODEL)),
            _const_spec((1, D_MODEL)),
            _const_spec((1, D_MODEL)),
        ],
        out_specs=pl.BlockSpec((1, tm, D_MODEL), lambda i, j: (i, j, 0)),
        out_shape=jax.ShapeDtypeStruct((b, s, D_MODEL), F32),
        compiler_params=pltpu.CompilerParams(
            dimension_semantics=("arbitrary", "arbitrary"), vmem_limit_bytes=VMEM_LIMIT),
        name="mla_out_ln",
    )(tokT, g, mg, h, w_out, ln_g, ln_b)


_B_U = (0, TOK_WIDTH)
_B_GATE = (_B_U[1], _B_U[1] + MIX_WIDTH)
_B_QM = (_B_GATE[1], _B_GATE[1] + MEM_WIDTH)
_B_WIDTH = _B_QM[1]
GATE_TILE = 256


def _linear_scan(a, b):
    n = a.shape[0]
    row = lax.broadcasted_iota(jnp.int32, a.shape, 0)
    d = 1
    while d < n:
        live = row >= d
        b = a * jnp.where(live, pltpu.roll(b, d, 0), 0.0) + b
        if 2 * d < n:
            a = a * jnp.where(live, pltpu.roll(a, d, 0), 1.0)
        d *= 2
    return b


def _lru_layer_kernel(h_ref, w_in_ref, conv_w_ref, conv_b_ref, w_r_ref, b_r_ref, w_i_ref, b_i_ref, lam_ref,
                      kbd_ref, vbd_ref, w_out_ref, lng_ref, lnb_ref, o_ref,
                      ebuf_ref, tail_ref, carry_ref, perm_in_ref, perm_out_ref):
    tm, sl, ct = SUB_LRU, SUBLANES, GATE_TILE
    n = tm // sl
    halo = (CONV_W - 1) * sl
    lane_tiles = D_MODEL // LANES
    n_sub = TM_LRU // SUB_LRU
    step = pl.program_id(1)

    @pl.when(step == 0)
    def _():
        tail_ref[...] = jnp.zeros((halo, TOK_WIDTH), F32)
        carry_ref[...] = jnp.zeros((sl, TOK_WIDTH), F32)

    sub8 = lax.broadcasted_iota(jnp.int32, (sl, ct), 0)

    def segment_rows(j):
        s, i0 = divmod(sl * j, n)
        return pl.ds(sl * i0 + s, sl, stride=sl)

    def from_previous_segment(prev_tile, cur_tile):
        return jnp.where(sub8 == 0, pltpu.roll(prev_tile, 1, 0), pltpu.roll(cur_tile, 1, 0))

    tails = [tail_ref[:, c * ct:(c + 1) * ct] for c in range(TOK_WIDTH // ct)]
    carries = [carry_ref[:, c * ct:(c + 1) * ct] for c in range(TOK_WIDTH // ct)]

    def sub_tile_phases(r):
        row0 = r * tm
        v = {}

        def load():
            for j in range(n):
                rows = h_ref[0, row0 + j * sl:row0 + (j + 1) * sl, :]
                for c in range(lane_tiles):
                    perm_in_ref[r, c, segment_rows(j), :] = rows[:, c * LANES:(c + 1) * LANES]
            v["hin"] = jnp.concatenate([perm_in_ref[r, c] for c in range(lane_tiles)], axis=1)
            v["hb"] = v["hin"].astype(BF16)

        def project(c):
            v["z", c] = jnp.dot(v["hb"], w_in_ref[:, c * MXU_N:(c + 1) * MXU_N], preferred_element_type=F32)

        def recurrence(c):
            cols = slice(c * ct, (c + 1) * ct)
            u = v["z", _B_U[0] // MXU_N + c]
            for k in range(CONV_W - 1):
                cur = u[tm - halo + k * sl:tm - halo + (k + 1) * sl, :]
                ebuf_ref[r, k * sl:(k + 1) * sl, cols] = from_previous_segment(tails[c][k * sl:(k + 1) * sl, :], cur)
            ebuf_ref[r, halo:halo + tm, cols] = u
            tails[c] = u[tm - halo:, :]
            xc = conv_b_ref[:, cols] + u * conv_w_ref[CONV_W - 1:CONV_W, cols]
            for back in range(1, CONV_W):
                tap = CONV_W - 1 - back
                xc = xc + ebuf_ref[r, halo - back * sl:halo - back * sl + tm, cols] * conv_w_ref[tap:tap + 1, cols]

            xcb = xc.astype(BF16)
            gate_r = _sigmoid(jnp.dot(xcb, w_r_ref[c], preferred_element_type=F32) + b_r_ref[:, cols])
            gate_i = _sigmoid(jnp.dot(xcb, w_i_ref[c], preferred_element_type=F32) + b_i_ref[:, cols])

            neg_lam = -lam_ref[:, cols]
            softplus = jnp.maximum(neg_lam, 0.0) + jnp.log1p(jnp.exp(-jnp.abs(neg_lam)))
            log_a = (-LRU_C * softplus) * gate_r
            a = jnp.exp(log_a)
            one_minus_a2 = -jnp.tanh(log_a) * (a * a + 1.0)
            root = jnp.where(one_minus_a2 > 0.0, one_minus_a2 * lax.rsqrt(one_minus_a2), 0.0)
            b = root * (gate_i * xc)

            h_loc, a_run = [b[0:sl, :]], [a[0:sl, :]]
            for g in range(1, n):
                ag = a[g * sl:(g + 1) * sl, :]
                h_loc.append(ag * h_loc[-1] + b[g * sl:(g + 1) * sl, :])
                a_run.append(ag * a_run[-1])
            state_in = pltpu.roll(carries[c], 1, 0)
            seg_b = jnp.where(sub8 == 0, a_run[-1] * state_in + h_loc[-1], h_loc[-1])
            seg_end = _linear_scan(a_run[-1], seg_b)
            start = from_previous_segment(carries[c], seg_end)
            carries[c] = seg_end
            v["hs", c] = jnp.concatenate([h_loc[g] + a_run[g] * start for g in range(n)], axis=0)

        def gate(c):
            g = _silu(v["z", _B_GATE[0] // MXU_N + c])
            if c < TOK_WIDTH // ct:
                v["y", c] = (v["hs", c] * g).astype(BF16)
            else:
                mem_out = _memory_attention(v["z", _B_QM[0] // MXU_N], kbd_ref.at[0, 0], vbd_ref.at[0, 0])
                v["y", c] = (mem_out * g).astype(BF16)

        def out_project(c):
            if c == 0:
                v["yb"] = jnp.concatenate([v["y", k] for k in range(MIX_WIDTH // ct)], axis=1)
            v["o", c] = jnp.dot(v["yb"], w_out_ref[:, c * MXU_N:(c + 1) * MXU_N], preferred_element_type=F32)

        def finish():
            o = jnp.concatenate([v["o", c] for c in range(D_MODEL // MXU_N)], axis=1)
            out = _layer_norm(ALPHA * v["hin"] + o, lng_ref[...], lnb_ref[...])
            for c in range(lane_tiles):
                perm_out_ref[r, c] = out[:, c * LANES:(c + 1) * LANES]
            for j in range(n):
                o_ref[0, row0 + j * sl:row0 + (j + 1) * sl, :] = jnp.concatenate(
                    [perm_out_ref[r, c, segment_rows(j), :] for c in range(lane_tiles)], axis=1)

        n_tok, n_mix = TOK_WIDTH // ct, MIX_WIDTH // ct
        phase_a = [load] + [functools.partial(project, c) for c in range(_B_WIDTH // MXU_N)]
        phase_b = []
        for c in range(n_mix):
            if c < n_tok:
                phase_b.append(functools.partial(recurrence, c))
            phase_b.append(functools.partial(gate, c))
        phase_cd = [functools.partial(out_project, c) for c in range(D_MODEL // MXU_N)] + [finish]
        return phase_a, phase_b, phase_cd

    phases = [sub_tile_phases(r) for r in range(n_sub)]
    _emit_round_robin(phases[0][0])
    for r in range(n_sub + 1):
        vector_heavy = phases[r][1] if r < n_sub else []
        matmul_heavy = (phases[r + 1][0] if r + 1 < n_sub else []) + (phases[r - 1][2] if r >= 1 else [])
        _emit_round_robin(matmul_heavy, vector_heavy)

    for c in range(TOK_WIDTH // ct):
        tail_ref[:, c * ct:(c + 1) * ct] = tails[c]
        carry_ref[:, c * ct:(c + 1) * ct] = carries[c]


def _lru_layer(h, w_in, conv_w, conv_b, w_r, b_r, w_i, b_i, lam, kbd, vbd, w_out, ln_g, ln_b):
    b, s, _ = h.shape
    tm = TM_LRU
    n_gate_tiles = TOK_WIDTH // GATE_TILE
    n_sub = TM_LRU // SUB_LRU
    return pl.pallas_call(
        _lru_layer_kernel,
        grid=(b, s // tm),
        in_specs=[
            pl.BlockSpec((1, tm, D_MODEL), lambda i, j: (i, j, 0)),
            _const_spec((D_MODEL, _B_WIDTH)),
            _const_spec((CONV_W, TOK_WIDTH)),
            _const_spec((1, TOK_WIDTH)),
            _const_spec((n_gate_tiles, GATE_TILE, GATE_TILE)),
            _const_spec((1, TOK_WIDTH)),
            _const_spec((n_gate_tiles, GATE_TILE, GATE_TILE)),
            _const_spec((1, TOK_WIDTH)),
            _const_spec((1, TOK_WIDTH)),
            pl.BlockSpec((1, 1, MEM_WIDTH, N_MEM_HEADS * MEM_LEN), lambda i, j: (1, i, 0, 0)),
            pl.BlockSpec((1, 1, N_MEM_HEADS * MEM_LEN, MEM_WIDTH), lambda i, j: (1, i, 0, 0)),
            _const_spec((MIX_WIDTH, D_MODEL)),
            _const_spec((1, D_MODEL)),
            _const_spec((1, D_MODEL)),
        ],
        out_specs=pl.BlockSpec((1, tm, D_MODEL), lambda i, j: (i, j, 0)),
        out_shape=jax.ShapeDtypeStruct((b, s, D_MODEL), F32),
        scratch_shapes=[
            pltpu.VMEM((n_sub, SUB_LRU + (CONV_W - 1) * SUBLANES, TOK_WIDTH), F32),
            pltpu.VMEM(((CONV_W - 1) * SUBLANES, TOK_WIDTH), F32),
            pltpu.VMEM((SUBLANES, TOK_WIDTH), F32),
            pltpu.VMEM((n_sub, D_MODEL // LANES, SUB_LRU, LANES), F32),
            pltpu.VMEM((n_sub, D_MODEL // LANES, SUB_LRU, LANES), F32),
        ],
        compiler_params=pltpu.CompilerParams(
            dimension_semantics=("arbitrary", "arbitrary"), vmem_limit_bytes=VMEM_LIMIT),
        name="lru_layer",
    )(h, w_in, conv_w, conv_b, w_r, b_r, w_i, b_i, lam, kbd, vbd, w_out, ln_g, ln_b)


def _block_diag_tiles(w):
    per_tile = GATE_TILE // HEAD_DIM
    w = w.reshape(TOK_WIDTH // GATE_TILE, per_tile, HEAD_DIM, HEAD_DIM)
    eye = jnp.eye(per_tile, dtype=w.dtype)
    t = w[:, :, :, None, :] * eye[None, :, None, :, None]
    return t.reshape(TOK_WIDTH // GATE_TILE, GATE_TILE, GATE_TILE)


def _rope_frequencies(tm):
    inv_freq = ROPE_THETA ** (-np.arange(HALF_ROPE, dtype=np.float32) / np.float32(HALF_ROPE))
    return np.ascontiguousarray(np.broadcast_to(inv_freq[:, None], (HALF_ROPE, tm)))


def _head_lanes(w, nope, t1, t2):
    zero = w.shape[-1]
    idx = np.full((LANES,), zero, np.int32)
    head = LANES // 2 - HALF_ROPE
    if t1 is not None:
        idx[:HALF_ROPE] = t1 + np.arange(HALF_ROPE)
        idx[LANES // 2:LANES // 2 + HALF_ROPE] = t2 + np.arange(HALF_ROPE)
    if nope is not None:
        idx[HALF_ROPE:LANES // 2] = nope + np.arange(head)
        idx[LANES // 2 + HALF_ROPE:LANES // 2 + HALF_ROPE + QK_NOPE - head] = nope + head + np.arange(QK_NOPE - head)
    w_ext = jnp.concatenate([w, jnp.zeros(w.shape[:-1] + (1,), w.dtype)], axis=-1)
    return jnp.take(w_ext, jnp.asarray(idx), axis=-1)


def kernel(x, mem, positions, mla_w_in, mla_q_norm, mla_w_uq, mla_kv_norm, mla_w_ukv, lru_w_in, lru_conv_w,
           lru_conv_b, lru_w_rgate, lru_b_rgate, lru_w_igate, lru_b_igate, lru_lambda, w_mem_kv, w_out, ln_g, ln_b):
    b, s, _ = x.shape
    assert s % TQ_ATTN == 0 and s % TM_LRU == 0 and s % TM_OUT == 0 and s % TM_PROJ == 0 and TK_ATTN % TM_PROJ == 0

    kbd, vbd = _mem_kv(mem, w_mem_kv)

    w_in = mla_w_in[0]
    o_q, o_kv, o_kr, o_gate = Q_LORA, Q_LORA + KV_LORA, Q_LORA + KV_LORA + QK_ROPE, Q_LORA + KV_LORA + QK_ROPE + MIX_WIDTH
    w_kr = _head_lanes(w_in[:, o_kv:o_kr], None, 0, HALF_ROPE)
    w_in_a = jnp.concatenate(
        [w_in[:, :o_q], w_in[:, o_q:o_kv], w_kr, w_in[:, o_kr:o_gate], w_in[:, o_gate:]], axis=1).astype(BF16)
    w_uq = _head_lanes(mla_w_uq[0].reshape(Q_LORA, N_TOK_HEADS, QK_DIM), 0, QK_NOPE, QK_NOPE + HALF_ROPE)
    w_uq = w_uq.reshape(Q_LORA, N_TOK_HEADS * LANES).astype(BF16)
    w_ukv = mla_w_ukv[0].reshape(KV_LORA, N_TOK_HEADS, QK_NOPE + HEAD_DIM)
    w_k = _head_lanes(w_ukv[:, :, :QK_NOPE], 0, None, None)
    w_k = w_k.reshape(KV_LORA, N_TOK_HEADS * LANES).astype(BF16)
    w_vT = w_ukv[:, :, QK_NOPE:].reshape(KV_LORA, TOK_WIDTH).T.astype(BF16)
    pos = positions.astype(F32)[:, None, :]
    w_out_b = w_out.astype(BF16)

    q, k, vT, g, mg = _mla_proj(x, pos, w_in_a, mla_q_norm[0][None], mla_kv_norm[0][None], w_uq, w_k, w_vT,
                                jnp.asarray(_rope_frequencies(TM_PROJ)), kbd, vbd)
    tokT = _attention(q, k, vT)
    h1 = _out_ln(tokT, g, mg, x, w_out_b[0], ln_g[0][None], ln_b[0][None])

    return _lru_layer(h1, lru_w_in[0].astype(BF16), lru_conv_w[0], lru_conv_b[0][None],
                      _block_diag_tiles(lru_w_rgate[0]).astype(BF16), lru_b_rgate[0][None],
                      _block_diag_tiles(lru_w_igate[0]).astype(BF16), lru_b_igate[0][None],
                      lru_lambda[0][None], kbd, vbd, w_out_b[1], ln_g[1][None], ln_b[1][None])
```

```python
import functools
import math

import jax
import jax.numpy as jnp
import numpy as np
from jax import lax
from jax.experimental import pallas as pl
from jax.experimental.pallas import tpu as pltpu

F32 = jnp.float32
BF16 = jnp.bfloat16

D_MODEL = 1024
DEPTH = 2
MEM_LEN = 256
HEAD_DIM = 64
N_MEM_HEADS = 4
N_TOK_HEADS = 12
TOK_WIDTH = N_TOK_HEADS * HEAD_DIM
MEM_WIDTH = N_MEM_HEADS * HEAD_DIM
MIX_WIDTH = TOK_WIDTH + MEM_WIDTH
Q_LORA = 384
KV_LORA = 256
QK_NOPE = 64
QK_ROPE = 32
QK_DIM = QK_NOPE + QK_ROPE
ROPE_THETA = 10000.0
CONV_W = 4
LRU_C = 8.0
ALPHA = (2.0 * DEPTH) ** 0.25
NORM_EPS = 1e-6

MXU_N = 256
LANES = 128
SUBLANES = 8
HALF_ROPE = QK_ROPE // 2
LOG2E = math.log2(math.e)
Q_SCALE = LOG2E / math.sqrt(QK_DIM)

TM_PROJ = 1024
SUB_PROJ = 256
TK_ATTN = 1024
TQ_ATTN = 2 * TK_ATTN
GW_ATTN = 256
TM_OUT = 1024
SUB_OUT = 256
TM_LRU = 1024
SUB_LRU = 256
VMEM_LIMIT = 48 * 1024 * 1024

_NT = (((1,), (1,)), ((), ()))


def _const_spec(shape):
    nd = len(shape)
    return pl.BlockSpec(shape, lambda *_: (0,) * nd, pipeline_mode=pl.Buffered(1))


def _dot_cols(x, w_ref, lo, hi):
    parts = [jnp.dot(x, w_ref[:, c:min(c + MXU_N, hi)], preferred_element_type=F32) for c in range(lo, hi, MXU_N)]
    return parts[0] if len(parts) == 1 else jnp.concatenate(parts, axis=-1)


def _emit_round_robin(*piece_lists):
    lists = [list(p) for p in piece_lists if p]
    longest = max((len(p) for p in lists), default=0)
    for k in range(longest):
        for p in lists:
            lo, hi = (k * len(p)) // longest, ((k + 1) * len(p)) // longest
            for piece in p[lo:hi]:
                piece()


def _sigmoid(t):
    return 0.5 * jnp.tanh(0.5 * t) + 0.5


def _silu(t):
    h = 0.5 * t
    return h * jnp.tanh(h) + h


def _rms_norm(t, g):
    return t * lax.rsqrt(jnp.mean(t * t, axis=-1, keepdims=True) + NORM_EPS) * g


def _layer_norm(t, g, b):
    mu = jnp.mean(t, axis=-1, keepdims=True)
    c = t - mu
    var = jnp.mean(c * c, axis=-1, keepdims=True)
    return c * lax.rsqrt(var + NORM_EPS) * g + b


def _memory_attention(qm, kbd_ref, vbd_ref):
    s = _dot_cols(qm.astype(BF16), kbd_ref, 0, N_MEM_HEADS * MEM_LEN)
    ps = []
    for h in range(N_MEM_HEADS):
        sh = s[:, h * MEM_LEN:(h + 1) * MEM_LEN]
        e = jnp.exp2(sh - jnp.max(sh, axis=-1, keepdims=True))
        ps.append((e * (1.0 / jnp.sum(e, axis=-1, keepdims=True))).astype(BF16))
    p = jnp.concatenate(ps, axis=-1)
    return jnp.dot(p, vbd_ref[...], preferred_element_type=F32)


def _mem_kv_kernel(mem_ref, wkT_ref, wv_ref, kbd_ref, vbd_ref):
    mb = mem_ref[0].astype(BF16)
    kT = lax.dot_general(wkT_ref[0].astype(BF16), mb, _NT, preferred_element_type=F32)
    kT = (kT * (LOG2E / math.sqrt(HEAD_DIM))).astype(BF16)
    v = jnp.dot(mb, wv_ref[0].astype(BF16), preferred_element_type=F32).astype(BF16)
    row = lax.broadcasted_iota(jnp.int32, (MEM_WIDTH, MEM_LEN), 0)
    col = lax.broadcasted_iota(jnp.int32, (MEM_LEN, MEM_WIDTH), 1)
    zero = jnp.zeros((), BF16)
    for h in range(N_MEM_HEADS):
        lo, hi = h * HEAD_DIM, (h + 1) * HEAD_DIM
        kbd_ref[0, 0, :, h * MEM_LEN:(h + 1) * MEM_LEN] = jnp.where((row >= lo) & (row < hi), kT, zero)
        vbd_ref[0, 0, h * MEM_LEN:(h + 1) * MEM_LEN, :] = jnp.where((col >= lo) & (col < hi), v, zero)


def _mem_kv(mem, w_mem_kv):
    b = mem.shape[0]
    wkT = jnp.swapaxes(w_mem_kv[:, :, :MEM_WIDTH], 1, 2)
    wv = w_mem_kv[:, :, MEM_WIDTH:]
    return pl.pallas_call(
        _mem_kv_kernel,
        grid=(DEPTH, b),
        in_specs=[
            pl.BlockSpec((1, MEM_LEN, D_MODEL), lambda l, i: (i, 0, 0)),
            pl.BlockSpec((1, MEM_WIDTH, D_MODEL), lambda l, i: (l, 0, 0)),
            pl.BlockSpec((1, D_MODEL, MEM_WIDTH), lambda l, i: (l, 0, 0)),
        ],
        out_specs=[
            pl.BlockSpec((1, 1, MEM_WIDTH, N_MEM_HEADS * MEM_LEN), lambda l, i: (l, i, 0, 0)),
            pl.BlockSpec((1, 1, N_MEM_HEADS * MEM_LEN, MEM_WIDTH), lambda l, i: (l, i, 0, 0)),
        ],
        out_shape=[
            jax.ShapeDtypeStruct((DEPTH, b, MEM_WIDTH, N_MEM_HEADS * MEM_LEN), BF16),
            jax.ShapeDtypeStruct((DEPTH, b, N_MEM_HEADS * MEM_LEN, MEM_WIDTH), BF16),
        ],
        name="mem_kv",
    )(mem, wkT, wv)


_A_CQ = (0, Q_LORA)
_A_CKV = (_A_CQ[1], _A_CQ[1] + KV_LORA)
_A_KR = (_A_CKV[1], _A_CKV[1] + LANES)
_A_GATE = (_A_KR[1], _A_KR[1] + MIX_WIDTH)
_A_QM = (_A_GATE[1], _A_GATE[1] + MEM_WIDTH)
_A_WIDTH = _A_QM[1]


def _mla_proj_kernel(x_ref, pos_ref, w_in_ref, qn_ref, kvn_ref, w_uq_ref, w_k_ref, w_vT_ref, rope_ref,
                     kbd_ref, vbd_ref, q_ref, k_ref, vT_ref, g_ref, mg_ref):
    sub = SUB_PROJ
    n_sub = x_ref.shape[1] // sub
    heads_per_tile = MXU_N // LANES

    ang = rope_ref[...] * pos_ref[0]
    cos_t, sin_t = jnp.cos(ang), jnp.sin(ang)
    gap = LANES // 2 - HALF_ROPE
    ones, zeros = jnp.ones((gap, ang.shape[1]), F32), jnp.zeros((gap, ang.shape[1]), F32)
    cos_all = jnp.concatenate([cos_t, ones, cos_t, ones], axis=0).T
    sin_all = jnp.concatenate([-sin_t, zeros, sin_t, zeros], axis=0).T

    def sub_tile_pieces(r):
        rows = slice(r * sub, (r + 1) * sub)
        cos, sin = cos_all[rows, :], sin_all[rows, :]
        v = {}

        def rope(t):
            return t * cos + pltpu.roll(t, LANES // 2, 1) * sin

        def load():
            v["xb"] = x_ref[0, rows, :].astype(BF16)

        def project(c):
            v["z", c] = jnp.dot(v["xb"], w_in_ref[:, c * MXU_N:(c + 1) * MXU_N], preferred_element_type=F32)

        def seg(s):
            z = jnp.concatenate([v["z", c] for c in range(s[0] // MXU_N, (s[1] - 1) // MXU_N + 1)], axis=1)
            lo = s[0] - (s[0] // MXU_N) * MXU_N
            return z[:, lo:lo + s[1] - s[0]]

        def q_norm():
            v["cqn"] = _rms_norm(seg(_A_CQ), qn_ref[...]).astype(BF16)

        def q_heads(t):
            q = jnp.dot(v["cqn"], w_uq_ref[:, t * MXU_N:(t + 1) * MXU_N], preferred_element_type=F32)
            for i in range(heads_per_tile):
                q_ref[0, t * heads_per_tile + i, rows, :] = (
                    rope(q[:, i * LANES:(i + 1) * LANES]) * Q_SCALE).astype(BF16)

        def kv_norm():
            v["ckvn"] = _rms_norm(seg(_A_CKV), kvn_ref[...]).astype(BF16)
            v["k_rope"] = rope(seg(_A_KR))

        def k_heads(t):
            k = jnp.dot(v["ckvn"], w_k_ref[:, t * MXU_N:(t + 1) * MXU_N], preferred_element_type=F32)
            for i in range(heads_per_tile):
                k_ref[0, t * heads_per_tile + i, rows, :] = (k[:, i * LANES:(i + 1) * LANES] + v["k_rope"]).astype(BF16)

        def values():
            vT = lax.dot_general(w_vT_ref[...], v["ckvn"], _NT, preferred_element_type=F32)
            for h in range(N_TOK_HEADS):
                vT_ref[0, h, 0, :, rows] = vT[h * HEAD_DIM:(h + 1) * HEAD_DIM, :].astype(BF16)

        def gates():
            g = _silu(seg(_A_GATE))
            g_ref[0, rows, :] = g[:, :TOK_WIDTH].astype(g_ref.dtype)
            mem_out = _memory_attention(seg(_A_QM), kbd_ref.at[0, 0], vbd_ref.at[0, 0])
            mg_ref[0, rows, :] = (mem_out * g[:, TOK_WIDTH:]).astype(mg_ref.dtype)

        n_tiles = N_TOK_HEADS // heads_per_tile
        matmul_heavy = [load] + [functools.partial(project, c) for c in range(_A_WIDTH // MXU_N)]
        vector_heavy = ([q_norm] + [functools.partial(q_heads, t) for t in range(n_tiles)] + [kv_norm]
                        + [functools.partial(k_heads, t) for t in range(n_tiles)] + [values, gates])
        return matmul_heavy, vector_heavy

    pieces = [sub_tile_pieces(r) for r in range(n_sub)]
    _emit_round_robin(pieces[0][0])
    for r in range(n_sub):
        _emit_round_robin(pieces[r + 1][0] if r + 1 < n_sub else [], pieces[r][1])


def _mla_proj(x, pos, w_in_a, qn, kvn, w_uq, w_k, w_vT, rope_c, kbd, vbd):
    b, s, _ = x.shape
    tm = TM_PROJ
    nt = s // tm
    return pl.pallas_call(
        _mla_proj_kernel,
        grid=(b, nt),
        in_specs=[
            pl.BlockSpec((1, tm, D_MODEL), lambda i, j: (i, j, 0)),
            pl.BlockSpec((1, 1, tm), lambda i, j: (i, 0, j)),
            _const_spec((D_MODEL, _A_WIDTH)),
            _const_spec((1, Q_LORA)),
            _const_spec((1, KV_LORA)),
            _const_spec((Q_LORA, N_TOK_HEADS * LANES)),
            _const_spec((KV_LORA, N_TOK_HEADS * LANES)),
            _const_spec((TOK_WIDTH, KV_LORA)),
            _const_spec((HALF_ROPE, tm)),
            pl.BlockSpec((1, 1, MEM_WIDTH, N_MEM_HEADS * MEM_LEN), lambda i, j: (0, i, 0, 0)),
            pl.BlockSpec((1, 1, N_MEM_HEADS * MEM_LEN, MEM_WIDTH), lambda i, j: (0, i, 0, 0)),
        ],
        out_specs=[
            pl.BlockSpec((1, N_TOK_HEADS, tm, LANES), lambda i, j: (i, 0, j, 0)),
            pl.BlockSpec((1, N_TOK_HEADS, tm, LANES), lambda i, j: (i, 0, j, 0)),
            pl.BlockSpec((1, N_TOK_HEADS, 1, HEAD_DIM, tm), lambda i, j: (i, 0, j, 0, 0)),
            pl.BlockSpec((1, tm, TOK_WIDTH), lambda i, j: (i, j, 0)),
            pl.BlockSpec((1, tm, MEM_WIDTH), lambda i, j: (i, j, 0)),
        ],
        out_shape=[
            jax.ShapeDtypeStruct((b, N_TOK_HEADS, s, LANES), BF16),
            jax.ShapeDtypeStruct((b, N_TOK_HEADS, s, LANES), BF16),
            jax.ShapeDtypeStruct((b, N_TOK_HEADS, nt, HEAD_DIM, tm), BF16),
            jax.ShapeDtypeStruct((b, s, TOK_WIDTH), BF16),
            jax.ShapeDtypeStruct((b, s, MEM_WIDTH), BF16),
        ],
        compiler_params=pltpu.CompilerParams(
            dimension_semantics=("arbitrary", "arbitrary"), vmem_limit_bytes=VMEM_LIMIT),
        name="mla_proj",
    )(x, pos, w_in_a, qn, kvn, w_uq, w_k, w_vT, rope_c, kbd, vbd)


def _attn_kernel(q_ref, q_next_ref, k_ref, vT_ref, o_ref, s_ref, bm_ref, m_ref, l_ref, acc_ref):
    tq, tk, gw, tv = TQ_ATTN, TK_ATTN, GW_ATTN, TM_PROJ
    groups = range(tq // gw)
    qi = pl.program_id(2)
    last_tile = pl.num_programs(2) - 1
    m_ref[...] = jnp.full((1, tq), -jnp.inf, F32)
    l_ref[...] = jnp.zeros((1, tq), F32)
    acc_ref[...] = jnp.zeros((HEAD_DIM, tq), F32)

    def visible_rows(g, key_off):
        return tk if key_off is None else max(0, min(tk, (g + 1) * gw - key_off))

    def scores(chunk, g, slot, key_off=None, queries=q_ref):
        c0, rows = g * gw, visible_rows(g, key_off)
        if rows == 0:
            return
        start = pl.multiple_of(chunk * tk, tk)
        kt = k_ref[0, 0, pl.ds(start, rows), :]
        st = lax.dot_general(kt, queries[0, 0, c0:c0 + gw, :], _NT, preferred_element_type=F32)
        if key_off is not None and key_off + rows - 1 > c0:
            key = lax.broadcasted_iota(jnp.int32, st.shape, 0) + key_off
            qry = lax.broadcasted_iota(jnp.int32, st.shape, 1) + c0
            st = jnp.where(key <= qry, st, -jnp.inf)
        s_ref[slot, :rows, c0:c0 + gw] = st
        bm_ref[slot, :, c0:c0 + gw] = jnp.max(st, axis=0, keepdims=True)

    def update(chunk, g, slot, key_off=None):
        c0, c1, rows = g * gw, (g + 1) * gw, visible_rows(g, key_off)
        if rows == 0:
            return
        m_old = m_ref[:, c0:c1]
        m_new = jnp.maximum(m_old, bm_ref[slot, :, c0:c1])
        alpha = jnp.exp2(m_old - m_new)
        p = jnp.exp2(s_ref[slot, :rows, c0:c1] - m_new)
        l_ref[:, c0:c1] = alpha * l_ref[:, c0:c1] + jnp.sum(p, axis=0, keepdims=True)
        m_ref[:, c0:c1] = m_new
        pb = p.astype(BF16)
        pv = None
        for r0 in range(0, rows, tv):
            r1 = min(rows, r0 + tv)
            part = jnp.dot(vT_ref[0, 0, chunk * (tk // tv) + r0 // tv, :, :r1 - r0], pb[r0:r1, :],
                           preferred_element_type=F32)
            pv = part if pv is None else pv + part
        acc_ref[:, c0:c1] = alpha * acc_ref[:, c0:c1] + pv

    def stage(nxt_chunk, nxt_key_off, cur_chunk, cur_slot, cur_key_off=None, after=None):
        for g in list(groups) + [len(groups)]:
            if nxt_chunk is not None and g < len(groups):
                scores(nxt_chunk, g, 1 - cur_slot, nxt_key_off)
            if g >= 1:
                update(cur_chunk, g - 1, cur_slot, cur_key_off)
                if after is not None:
                    after(g - 1)

    def next_tile_first_chunk(g):
        scores(0, g, 0, None, q_next_ref)

    d = 2 * qi

    def diagonal_tail(look_ahead):
        stage(d + 1, tk, d, 0, 0, next_tile_first_chunk if look_ahead else None)
        stage(None, None, d + 1, 1, tk)
        o_ref[0] = (acc_ref[...] * (1.0 / l_ref[...])).astype(o_ref.dtype)

    @pl.when(qi > 0)
    def _():
        def pair(p, _):
            c = 2 * p
            stage(c + 1, None, c, 0)
            stage(c + 2, None, c + 1, 1)
            return 0

        lax.fori_loop(0, qi - 1, pair, 0)
        stage(d - 1, None, d - 2, 0)
        stage(d, 0, d - 1, 1)

        @pl.when(qi < last_tile)
        def _():
            diagonal_tail(True)

        @pl.when(qi == last_tile)
        def _():
            diagonal_tail(False)

    @pl.when(qi == 0)
    def _():
        for g in groups:
            scores(d, g, 0, 0)
        diagonal_tail(True)


def _attention(q, k, vT):
    b, h, s, _ = q.shape
    tq, tk = TQ_ATTN, TK_ATTN
    nkv = vT.shape[2]
    return pl.pallas_call(
        _attn_kernel,
        grid=(b, h, s // tq),
        in_specs=[
            pl.BlockSpec((1, 1, tq, LANES), lambda i, j, t: (i, j, t, 0)),
            pl.BlockSpec((1, 1, tq, LANES), lambda i, j, t: (i, j, jnp.minimum(t + 1, s // tq - 1), 0)),
            pl.BlockSpec((1, 1, s, LANES), lambda i, j, t: (i, j, 0, 0)),
            pl.BlockSpec((1, 1, nkv, HEAD_DIM, TM_PROJ), lambda i, j, t: (i, j, 0, 0, 0)),
        ],
        out_specs=pl.BlockSpec((1, HEAD_DIM, tq), lambda i, j, t: (i, j, t)),
        out_shape=jax.ShapeDtypeStruct((b, h * HEAD_DIM, s), BF16),
        scratch_shapes=[
            pltpu.VMEM((2, tk, tq), F32),
            pltpu.VMEM((2, 1, tq), F32),
            pltpu.VMEM((1, tq), F32),
            pltpu.VMEM((1, tq), F32),
            pltpu.VMEM((HEAD_DIM, tq), F32),
        ],
        compiler_params=pltpu.CompilerParams(
            dimension_semantics=("arbitrary", "arbitrary", "arbitrary"), vmem_limit_bytes=VMEM_LIMIT),
        name="mla_attn",
    )(q, q, k, vT)


def _out_ln_kernel(tokT_ref, g_ref, mg_ref, h_ref, w_out_ref, lng_ref, lnb_ref, o_ref):
    sub = SUB_OUT
    n_sub = o_ref.shape[1] // sub
    v = {}

    def gate(r):
        rows = slice(r * sub, (r + 1) * sub)
        tok = tokT_ref[0, :, rows].astype(F32).T
        v["y", r] = (tok * g_ref[0, rows, :].astype(F32)).astype(BF16)

    def project(r, c):
        cols = slice(c * MXU_N, (c + 1) * MXU_N)
        v["o", r, c] = (
            jnp.dot(v["y", r], w_out_ref[:TOK_WIDTH, cols], preferred_element_type=F32)
            + jnp.dot(mg_ref[0, r * sub:(r + 1) * sub, :], w_out_ref[TOK_WIDTH:, cols], preferred_element_type=F32))

    def finish(r):
        rows = slice(r * sub, (r + 1) * sub)
        o = jnp.concatenate([v["o", r, c] for c in range(D_MODEL // MXU_N)], axis=1)
        o_ref[0, rows, :] = _layer_norm(ALPHA * h_ref[0, rows, :] + o, lng_ref[...], lnb_ref[...])

    def matmul_pieces(r):
        return [functools.partial(gate, r)] + [functools.partial(project, r, c) for c in range(D_MODEL // MXU_N)]

    _emit_round_robin(matmul_pieces(0))
    for r in range(n_sub):
        _emit_round_robin(matmul_pieces(r + 1) if r + 1 < n_sub else [], [functools.partial(finish, r)])


def _out_ln(tokT, g, mg, h, w_out, ln_g, ln_b):
    b, s, _ = h.shape
    tm = TM_OUT
    return pl.pallas_call(
        _out_ln_kernel,
        grid=(b, s // tm),
        in_specs=[
            pl.BlockSpec((1, TOK_WIDTH, tm), lambda i, j: (i, 0, j)),
            pl.BlockSpec((1, tm, TOK_WIDTH), lambda i, j: (i, j, 0)),
            pl.BlockSpec((1, tm, MEM_WIDTH), lambda i, j: (i, j, 0)),
            pl.BlockSpec((1, tm, D_MODEL), lambda i, j: (i, j, 0)),
            _const_spec((MIX_WIDTH, D_MODEL)),
            _const_spec((1, D_MODEL)),
            _const_spec((1, D_MODEL)),
        ],
        out_specs=pl.BlockSpec((1, tm, D_MODEL), lambda i, j: (i, j, 0)),
        out_shape=jax.ShapeDtypeStruct((b, s, D_MODEL), F32),
        compiler_params=pltpu.CompilerParams(
            dimension_semantics=("arbitrary", "arbitrary"), vmem_limit_bytes=VMEM_LIMIT),
        name="mla_out_ln",
    )(tokT, g, mg, h, w_out, ln_g, ln_b)


_B_U = (0, TOK_WIDTH)
_B_GATE = (_B_U[1], _B_U[1] + MIX_WIDTH)
_B_QM = (_B_GATE[1], _B_GATE[1] + MEM_WIDTH)
_B_WIDTH = _B_QM[1]
GATE_TILE = 256


def _linear_scan(a, b):
    n = a.shape[0]
    row = lax.broadcasted_iota(jnp.int32, a.shape, 0)
    d = 1
    while d < n:
        live = row >= d
        b = a * jnp.where(live, pltpu.roll(b, d, 0), 0.0) + b
        if 2 * d < n:
            a = a * jnp.where(live, pltpu.roll(a, d, 0), 1.0)
        d *= 2
    return b


def _lru_layer_kernel(h_ref, w_in_ref, conv_w_ref, conv_b_ref, w_r_ref, b_r_ref, w_i_ref, b_i_ref, lam_ref,
                      kbd_ref, vbd_ref, w_out_ref, lng_ref, lnb_ref, o_ref,
                      ebuf_ref, tail_ref, carry_ref, perm_in_ref, perm_out_ref):
    tm, sl, ct = SUB_LRU, SUBLANES, GATE_TILE
    n = tm // sl
    halo = (CONV_W - 1) * sl
    lane_tiles = D_MODEL // LANES
    n_sub = TM_LRU // SUB_LRU
    step = pl.program_id(1)

    @pl.when(step == 0)
    def _():
        tail_ref[...] = jnp.zeros((halo, TOK_WIDTH), F32)
        carry_ref[...] = jnp.zeros((sl, TOK_WIDTH), F32)

    sub8 = lax.broadcasted_iota(jnp.int32, (sl, ct), 0)

    def segment_rows(j):
        s, i0 = divmod(sl * j, n)
        return pl.ds(sl * i0 + s, sl, stride=sl)

    def from_previous_segment(prev_tile, cur_tile):
        return jnp.where(sub8 == 0, pltpu.roll(prev_tile, 1, 0), pltpu.roll(cur_tile, 1, 0))

    tails = [tail_ref[:, c * ct:(c + 1) * ct] for c in range(TOK_WIDTH // ct)]
    carries = [carry_ref[:, c * ct:(c + 1) * ct] for c in range(TOK_WIDTH // ct)]

    def sub_tile_phases(r):
        row0 = r * tm
        v = {}

        def load():
            for j in range(n):
                rows = h_ref[0, row0 + j * sl:row0 + (j + 1) * sl, :]
                for c in range(lane_tiles):
                    perm_in_ref[r, c, segment_rows(j), :] = rows[:, c * LANES:(c + 1) * LANES]
            v["hin"] = jnp.concatenate([perm_in_ref[r, c] for c in range(lane_tiles)], axis=1)
            v["hb"] = v["hin"].astype(BF16)

        def project(c):
            v["z", c] = jnp.dot(v["hb"], w_in_ref[:, c * MXU_N:(c + 1) * MXU_N], preferred_element_type=F32)

        def recurrence(c):
            cols = slice(c * ct, (c + 1) * ct)
            u = v["z", _B_U[0] // MXU_N + c]
            for k in range(CONV_W - 1):
                cur = u[tm - halo + k * sl:tm - halo + (k + 1) * sl, :]
                ebuf_ref[r, k * sl:(k + 1) * sl, cols] = from_previous_segment(tails[c][k * sl:(k + 1) * sl, :], cur)
            ebuf_ref[r, halo:halo + tm, cols] = u
            tails[c] = u[tm - halo:, :]
            xc = conv_b_ref[:, cols] + u * conv_w_ref[CONV_W - 1:CONV_W, cols]
            for back in range(1, CONV_W):
                tap = CONV_W - 1 - back
                xc = xc + ebuf_ref[r, halo - back * sl:halo - back * sl + tm, cols] * conv_w_ref[tap:tap + 1, cols]

            xcb = xc.astype(BF16)
            gate_r = _sigmoid(jnp.dot(xcb, w_r_ref[c], preferred_element_type=F32) + b_r_ref[:, cols])
            gate_i = _sigmoid(jnp.dot(xcb, w_i_ref[c], preferred_element_type=F32) + b_i_ref[:, cols])

            neg_lam = -lam_ref[:, cols]
            softplus = jnp.maximum(neg_lam, 0.0) + jnp.log1p(jnp.exp(-jnp.abs(neg_lam)))
            log_a = (-LRU_C * softplus) * gate_r
            a = jnp.exp(log_a)
            one_minus_a2 = -jnp.tanh(log_a) * (a * a + 1.0)
            root = jnp.where(one_minus_a2 > 0.0, one_minus_a2 * lax.rsqrt(one_minus_a2), 0.0)
            b = root * (gate_i * xc)

            h_loc, a_run = [b[0:sl, :]], [a[0:sl, :]]
            for g in range(1, n):
                ag = a[g * sl:(g + 1) * sl, :]
                h_loc.append(ag * h_loc[-1] + b[g * sl:(g + 1) * sl, :])
                a_run.append(ag * a_run[-1])
            state_in = pltpu.roll(carries[c], 1, 0)
            seg_b = jnp.where(sub8 == 0, a_run[-1] * state_in + h_loc[-1], h_loc[-1])
            seg_end = _linear_scan(a_run[-1], seg_b)
            start = from_previous_segment(carries[c], seg_end)
            carries[c] = seg_end
            v["hs", c] = jnp.concatenate([h_loc[g] + a_run[g] * start for g in range(n)], axis=0)

        def gate(c):
            g = _silu(v["z", _B_GATE[0] // MXU_N + c])
            if c < TOK_WIDTH // ct:
                v["y", c] = (v["hs", c] * g).astype(BF16)
            else:
                mem_out = _memory_attention(v["z", _B_QM[0] // MXU_N], kbd_ref.at[0, 0], vbd_ref.at[0, 0])
                v["y", c] = (mem_out * g).astype(BF16)

        def out_project(c):
            if c == 0:
                v["yb"] = jnp.concatenate([v["y", k] for k in range(MIX_WIDTH // ct)], axis=1)
            v["o", c] = jnp.dot(v["yb"], w_out_ref[:, c * MXU_N:(c + 1) * MXU_N], preferred_element_type=F32)

        def finish():
            o = jnp.concatenate([v["o", c] for c in range(D_MODEL // MXU_N)], axis=1)
            out = _layer_norm(ALPHA * v["hin"] + o, lng_ref[...], lnb_ref[...])
            for c in range(lane_tiles):
                perm_out_ref[r, c] = out[:, c * LANES:(c + 1) * LANES]
            for j in range(n):
                o_ref[0, row0 + j * sl:row0 + (j + 1) * sl, :] = jnp.concatenate(
                    [perm_out_ref[r, c, segment_rows(j), :] for c in range(lane_tiles)], axis=1)

        n_tok, n_mix = TOK_WIDTH // ct, MIX_WIDTH // ct
        phase_a = [load] + [functools.partial(project, c) for c in range(_B_WIDTH // MXU_N)]
        phase_b = []
        for c in range(n_mix):
            if c < n_tok:
                phase_b.append(functools.partial(recurrence, c))
            phase_b.append(functools.partial(gate, c))
        phase_cd = [functools.partial(out_project, c) for c in range(D_MODEL // MXU_N)] + [finish]
        return phase_a, phase_b, phase_cd

    phases = [sub_tile_phases(r) for r in range(n_sub)]
    _emit_round_robin(phases[0][0])
    for r in range(n_sub + 1):
        vector_heavy = phases[r][1] if r < n_sub else []
        matmul_heavy = (phases[r + 1][0] if r + 1 < n_sub else []) + (phases[r - 1][2] if r >= 1 else [])
        _emit_round_robin(matmul_heavy, vector_heavy)

    for c in range(TOK_WIDTH // ct):
        tail_ref[:, c * ct:(c + 1) * ct] = tails[c]
        carry_ref[:, c * ct:(c + 1) * ct] = carries[c]


def _lru_layer(h, w_in, conv_w, conv_b, w_r, b_r, w_i, b_i, lam, kbd, vbd, w_out, ln_g, ln_b):
    b, s, _ = h.shape
    tm = TM_LRU
    n_gate_tiles = TOK_WIDTH // GATE_TILE
    n_sub = TM_LRU // SUB_LRU
    return pl.pallas_call(
        _lru_layer_kernel,
        grid=(b, s // tm),
        in_specs=[
            pl.BlockSpec((1, tm, D_MODEL), lambda i, j: (i, j, 0)),
            _const_spec((D_MODEL, _B_WIDTH)),
            _const_spec((CONV_W, TOK_WIDTH)),
            _const_spec((1, TOK_WIDTH)),
            _const_spec((n_gate_tiles, GATE_TILE, GATE_TILE)),
            _const_spec((1, TOK_WIDTH)),
            _const_spec((n_gate_tiles, GATE_TILE, GATE_TILE)),
            _const_spec((1, TOK_WIDTH)),
            _const_spec((1, TOK_WIDTH)),
            pl.BlockSpec((1, 1, MEM_WIDTH, N_MEM_HEADS * MEM_LEN), lambda i, j: (1, i, 0, 0)),
            pl.BlockSpec((1, 1, N_MEM_HEADS * MEM_LEN, MEM_WIDTH), lambda i, j: (1, i, 0, 0)),
            _const_spec((MIX_WIDTH, D_MODEL)),
            _const_spec((1, D_MODEL)),
            _const_spec((1, D_MODEL)),
        ],
        out_specs=pl.BlockSpec((1, tm, D_MODEL), lambda i, j: (i, j, 0)),
        out_shape=jax.ShapeDtypeStruct((b, s, D_MODEL), F32),
        scratch_shapes=[
            pltpu.VMEM((n_sub, SUB_LRU + (CONV_W - 1) * SUBLANES, TOK_WIDTH), F32),
            pltpu.VMEM(((CONV_W - 1) * SUBLANES, TOK_WIDTH), F32),
            pltpu.VMEM((SUBLANES, TOK_WIDTH), F32),
            pltpu.VMEM((n_sub, D_MODEL // LANES, SUB_LRU, LANES), F32),
            pltpu.VMEM((n_sub, D_MODEL // LANES, SUB_LRU, LANES), F32),
        ],
        compiler_params=pltpu.CompilerParams(
            dimension_semantics=("arbitrary", "arbitrary"), vmem_limit_bytes=VMEM_LIMIT),
        name="lru_layer",
    )(h, w_in, conv_w, conv_b, w_r, b_r, w_i, b_i, lam, kbd, vbd, w_out, ln_g, ln_b)


def _block_diag_tiles(w):
    per_tile = GATE_TILE // HEAD_DIM
    w = w.reshape(TOK_WIDTH // GATE_TILE, per_tile, HEAD_DIM, HEAD_DIM)
    eye = jnp.eye(per_tile, dtype=w.dtype)
    t = w[:, :, :, None, :] * eye[None, :, None, :, None]
    return t.reshape(TOK_WIDTH // GATE_TILE, GATE_TILE, GATE_TILE)


def _rope_frequencies(tm):
    inv_freq = ROPE_THETA ** (-np.arange(HALF_ROPE, dtype=np.float32) / np.float32(HALF_ROPE))
    return np.ascontiguousarray(np.broadcast_to(inv_freq[:, None], (HALF_ROPE, tm)))


def _head_lanes(w, nope, t1, t2):
    zero = w.shape[-1]
    idx = np.full((LANES,), zero, np.int32)
    head = LANES // 2 - HALF_ROPE
    if t1 is not None:
        idx[:HALF_ROPE] = t1 + np.arange(HALF_ROPE)
        idx[LANES // 2:LANES // 2 + HALF_ROPE] = t2 + np.arange(HALF_ROPE)
    if nope is not None:
        idx[HALF_ROPE:LANES // 2] = nope + np.arange(head)
        idx[LANES // 2 + HALF_ROPE:LANES // 2 + HALF_ROPE + QK_NOPE - head] = nope + head + np.arange(QK_NOPE - head)
    w_ext = jnp.concatenate([w, jnp.zeros(w.shape[:-1] + (1,), w.dtype)], axis=-1)
    return jnp.take(w_ext, jnp.asarray(idx), axis=-1)


def kernel(x, mem, positions, mla_w_in, mla_q_norm, mla_w_uq, mla_kv_norm, mla_w_ukv, lru_w_in, lru_conv_w,
           lru_conv_b, lru_w_rgate, lru_b_rgate, lru_w_igate, lru_b_igate, lru_lambda, w_mem_kv, w_out, ln_g, ln_b):
    b, s, _ = x.shape
    assert s % TQ_ATTN == 0 and s % TM_LRU == 0 and s % TM_OUT == 0 and s % TM_PROJ == 0 and TK_ATTN % TM_PROJ == 0

    kbd, vbd = _mem_kv(mem, w_mem_kv)

    w_in = mla_w_in[0]
    o_q, o_kv, o_kr, o_gate = Q_LORA, Q_LORA + KV_LORA, Q_LORA + KV_LORA + QK_ROPE, Q_LORA + KV_LORA + QK_ROPE + MIX_WIDTH
    w_kr = _head_lanes(w_in[:, o_kv:o_kr], None, 0, HALF_ROPE)
    w_in_a = jnp.concatenate(
        [w_in[:, :o_q], w_in[:, o_q:o_kv], w_kr, w_in[:, o_kr:o_gate], w_in[:, o_gate:]], axis=1).astype(BF16)
    w_uq = _head_lanes(mla_w_uq[0].reshape(Q_LORA, N_TOK_HEADS, QK_DIM), 0, QK_NOPE, QK_NOPE + HALF_ROPE)
    w_uq = w_uq.reshape(Q_LORA, N_TOK_HEADS * LANES).astype(BF16)
    w_ukv = mla_w_ukv[0].reshape(KV_LORA, N_TOK_HEADS, QK_NOPE + HEAD_DIM)
    w_k = _head_lanes(w_ukv[:, :, :QK_NOPE], 0, None, None)
    w_k = w_k.reshape(KV_LORA, N_TOK_HEADS * LANES).astype(BF16)
    w_vT = w_ukv[:, :, QK_NOPE:].reshape(KV_LORA, TOK_WIDTH).T.astype(BF16)
    pos = positions.astype(F32)[:, None, :]
    w_out_b = w_out.astype(BF16)

    q, k, vT, g, mg = _mla_proj(x, pos, w_in_a, mla_q_norm[0][None], mla_kv_norm[0][None], w_uq, w_k, w_vT,
                                jnp.asarray(_rope_frequencies(TM_PROJ)), kbd, vbd)
    tokT = _attention(q, k, vT)
    h1 = _out_ln(tokT, g, mg, x, w_out_b[0], ln_g[0][None], ln_b[0][None])

    return _lru_layer(h1, lru_w_in[0].astype(BF16), lru_conv_w[0], lru_conv_b[0][None],
                      _block_diag_tiles(lru_w_rgate[0]).astype(BF16), lru_b_rgate[0][None],
                      _block_diag_tiles(lru_w_igate[0]).astype(BF16), lru_b_igate[0][None],
                      lru_lambda[0][None], kbd, vbd, w_out_b[1], ln_g[1][None], ln_b[1][None])
```

```python
import functools
import math

import jax
import jax.numpy as jnp
import numpy as np
from jax import lax
from jax.experimental import pallas as pl
from jax.experimental.pallas import tpu as pltpu

F32 = jnp.float32
BF16 = jnp.bfloat16

D_MODEL = 1024
DEPTH = 2
MEM_LEN = 256
HEAD_DIM = 64
N_MEM_HEADS = 4
N_TOK_HEADS = 12
TOK_WIDTH = N_TOK_HEADS * HEAD_DIM
MEM_WIDTH = N_MEM_HEADS * HEAD_DIM
MIX_WIDTH = TOK_WIDTH + MEM_WIDTH
Q_LORA = 384
KV_LORA = 256
QK_NOPE = 64
QK_ROPE = 32
QK_DIM = QK_NOPE + QK_ROPE
ROPE_THETA = 10000.0
CONV_W = 4
LRU_C = 8.0
ALPHA = (2.0 * DEPTH) ** 0.25
NORM_EPS = 1e-6

MXU_N = 256
LANES = 128
SUBLANES = 8
BF16_ROWS = 16
HALF_ROPE = QK_ROPE // 2
LOG2E = math.log2(math.e)
Q_SCALE = LOG2E / math.sqrt(QK_DIM)

TM_PROJ = 1024
SUB_PROJ = 256
TK_ATTN = 1024
TQ_ATTN = 2 * TK_ATTN
GW_ATTN = 256
TM_OUT = 1024
SUB_OUT = 256
TM_LRU = 1024
SUB_LRU = 256
VMEM_LIMIT = 48 * 1024 * 1024

_NT = (((1,), (1,)), ((), ()))


def _const_spec(shape):
    nd = len(shape)
    return pl.BlockSpec(shape, lambda *_: (0,) * nd, pipeline_mode=pl.Buffered(1))


def _dot_cols(x, w_ref, lo, hi):
    parts = [jnp.dot(x, w_ref[:, c:min(c + MXU_N, hi)], preferred_element_type=F32) for c in range(lo, hi, MXU_N)]
    return parts[0] if len(parts) == 1 else jnp.concatenate(parts, axis=-1)


def _emit_round_robin(*piece_lists):
    lists = [list(p) for p in piece_lists if p]
    longest = max((len(p) for p in lists), default=0)
    for k in range(longest):
        for p in lists:
            lo, hi = (k * len(p)) // longest, ((k + 1) * len(p)) // longest
            for piece in p[lo:hi]:
                piece()


def _sigmoid(t):
    return 0.5 * jnp.tanh(0.5 * t) + 0.5


def _silu(t):
    h = 0.5 * t
    return h * jnp.tanh(h) + h


def _rms_norm(t, g):
    return t * lax.rsqrt(jnp.mean(t * t, axis=-1, keepdims=True) + NORM_EPS) * g


def _layer_norm(t, g, b):
    mu = jnp.mean(t, axis=-1, keepdims=True)
    c = t - mu
    var = jnp.mean(c * c, axis=-1, keepdims=True)
    return c * lax.rsqrt(var + NORM_EPS) * g + b


def _memory_attention(qm, kbd_ref, vbd_ref):
    s = _dot_cols(qm.astype(BF16), kbd_ref, 0, N_MEM_HEADS * MEM_LEN)
    ps = []
    for h in range(N_MEM_HEADS):
        sh = s[:, h * MEM_LEN:(h + 1) * MEM_LEN]
        e = jnp.exp2(sh - jnp.max(sh, axis=-1, keepdims=True))
        ps.append((e * (1.0 / jnp.sum(e, axis=-1, keepdims=True))).astype(BF16))
    p = jnp.concatenate(ps, axis=-1)
    return jnp.dot(p, vbd_ref[...], preferred_element_type=F32)


def _mem_kv_kernel(mem_ref, wkT_ref, wv_ref, kbd_ref, vbd_ref):
    mb = mem_ref[0].astype(BF16)
    kT = lax.dot_general(wkT_ref[0].astype(BF16), mb, _NT, preferred_element_type=F32)
    kT = (kT * (LOG2E / math.sqrt(HEAD_DIM))).astype(BF16)
    v = jnp.dot(mb, wv_ref[0].astype(BF16), preferred_element_type=F32).astype(BF16)
    row = lax.broadcasted_iota(jnp.int32, (MEM_WIDTH, MEM_LEN), 0)
    col = lax.broadcasted_iota(jnp.int32, (MEM_LEN, MEM_WIDTH), 1)
    zero = jnp.zeros((), BF16)
    for h in range(N_MEM_HEADS):
        lo, hi = h * HEAD_DIM, (h + 1) * HEAD_DIM
        kbd_ref[0, 0, :, h * MEM_LEN:(h + 1) * MEM_LEN] = jnp.where((row >= lo) & (row < hi), kT, zero)
        vbd_ref[0, 0, h * MEM_LEN:(h + 1) * MEM_LEN, :] = jnp.where((col >= lo) & (col < hi), v, zero)


def _mem_kv(mem, w_mem_kv):
    b = mem.shape[0]
    wkT = jnp.swapaxes(w_mem_kv[:, :, :MEM_WIDTH], 1, 2)
    wv = w_mem_kv[:, :, MEM_WIDTH:]
    return pl.pallas_call(
        _mem_kv_kernel,
        grid=(DEPTH, b),
        in_specs=[
            pl.BlockSpec((1, MEM_LEN, D_MODEL), lambda l, i: (i, 0, 0)),
            pl.BlockSpec((1, MEM_WIDTH, D_MODEL), lambda l, i: (l, 0, 0)),
            pl.BlockSpec((1, D_MODEL, MEM_WIDTH), lambda l, i: (l, 0, 0)),
        ],
        out_specs=[
            pl.BlockSpec((1, 1, MEM_WIDTH, N_MEM_HEADS * MEM_LEN), lambda l, i: (l, i, 0, 0)),
            pl.BlockSpec((1, 1, N_MEM_HEADS * MEM_LEN, MEM_WIDTH), lambda l, i: (l, i, 0, 0)),
        ],
        out_shape=[
            jax.ShapeDtypeStruct((DEPTH, b, MEM_WIDTH, N_MEM_HEADS * MEM_LEN), BF16),
            jax.ShapeDtypeStruct((DEPTH, b, N_MEM_HEADS * MEM_LEN, MEM_WIDTH), BF16),
        ],
        name="mem_kv",
    )(mem, wkT, wv)


_A_CQ = (0, Q_LORA)
_A_CKV = (_A_CQ[1], _A_CQ[1] + KV_LORA)
_A_KR = (_A_CKV[1], _A_CKV[1] + LANES)
_A_GATE = (_A_KR[1], _A_KR[1] + MIX_WIDTH)
_A_QM = (_A_GATE[1], _A_GATE[1] + MEM_WIDTH)
_A_WIDTH = _A_QM[1]


def _mla_proj_kernel(x_ref, pos_ref, w_in_ref, qn_ref, kvn_ref, w_uq_ref, w_k_ref, w_vT_ref, rope_ref,
                     kbd_ref, vbd_ref, q_ref, k_ref, vT_ref, g_ref, mg_ref):
    sub = SUB_PROJ
    n_sub = x_ref.shape[1] // sub
    heads_per_tile = MXU_N // LANES

    ang = rope_ref[...] * pos_ref[0]
    cos_t, sin_t = jnp.cos(ang), jnp.sin(ang)
    gap = LANES // 2 - HALF_ROPE
    ones, zeros = jnp.ones((gap, ang.shape[1]), F32), jnp.zeros((gap, ang.shape[1]), F32)
    cos_all = jnp.concatenate([cos_t, ones, cos_t, ones], axis=0).T
    sin_all = jnp.concatenate([-sin_t, zeros, sin_t, zeros], axis=0).T

    def sub_tile_pieces(r):
        rows = slice(r * sub, (r + 1) * sub)
        cos, sin = cos_all[rows, :], sin_all[rows, :]
        v = {}

        def rope(t):
            return t * cos + pltpu.roll(t, LANES // 2, 1) * sin

        def load():
            v["xb"] = x_ref[0, rows, :].astype(BF16)

        def project(c):
            v["z", c] = jnp.dot(v["xb"], w_in_ref[:, c * MXU_N:(c + 1) * MXU_N], preferred_element_type=F32)

        def seg(s):
            z = jnp.concatenate([v["z", c] for c in range(s[0] // MXU_N, (s[1] - 1) // MXU_N + 1)], axis=1)
            lo = s[0] - (s[0] // MXU_N) * MXU_N
            return z[:, lo:lo + s[1] - s[0]]

        def q_norm():
            v["cqn"] = _rms_norm(seg(_A_CQ), qn_ref[...]).astype(BF16)

        def q_heads(t):
            q = jnp.dot(v["cqn"], w_uq_ref[:, t * MXU_N:(t + 1) * MXU_N], preferred_element_type=F32)
            for i in range(heads_per_tile):
                q_ref[0, t * heads_per_tile + i, rows, :] = (
                    rope(q[:, i * LANES:(i + 1) * LANES]) * Q_SCALE).astype(BF16)

        def kv_norm():
            v["ckvn"] = _rms_norm(seg(_A_CKV), kvn_ref[...]).astype(BF16)
            v["k_rope"] = rope(seg(_A_KR))

        def k_heads(t):
            k = jnp.dot(v["ckvn"], w_k_ref[:, t * MXU_N:(t + 1) * MXU_N], preferred_element_type=F32)
            for i in range(heads_per_tile):
                k_ref[0, t * heads_per_tile + i, rows, :] = (k[:, i * LANES:(i + 1) * LANES] + v["k_rope"]).astype(BF16)

        def values():
            vT = lax.dot_general(w_vT_ref[...], v["ckvn"], _NT, preferred_element_type=F32)
            for h in range(N_TOK_HEADS):
                vT_ref[0, h, 0, :, rows] = vT[h * HEAD_DIM:(h + 1) * HEAD_DIM, :].astype(BF16)

        def gates():
            g = _silu(seg(_A_GATE))
            g_ref[0, rows, :] = g[:, :TOK_WIDTH].astype(g_ref.dtype)
            mem_out = _memory_attention(seg(_A_QM), kbd_ref.at[0, 0], vbd_ref.at[0, 0])
            mg_ref[0, rows, :] = (mem_out * g[:, TOK_WIDTH:]).astype(mg_ref.dtype)

        n_tiles = N_TOK_HEADS // heads_per_tile
        matmul_heavy = [load] + [functools.partial(project, c) for c in range(_A_WIDTH // MXU_N)]
        vector_heavy = ([q_norm] + [functools.partial(q_heads, t) for t in range(n_tiles)] + [kv_norm]
                        + [functools.partial(k_heads, t) for t in range(n_tiles)] + [values, gates])
        return matmul_heavy, vector_heavy

    pieces = [sub_tile_pieces(r) for r in range(n_sub)]
    _emit_round_robin(pieces[0][0])
    for r in range(n_sub):
        _emit_round_robin(pieces[r + 1][0] if r + 1 < n_sub else [], pieces[r][1])


def _mla_proj(x, pos, w_in_a, qn, kvn, w_uq, w_k, w_vT, rope_c, kbd, vbd):
    b, s, _ = x.shape
    tm = TM_PROJ
    nt = s // tm
    return pl.pallas_call(
        _mla_proj_kernel,
        grid=(b, nt),
        in_specs=[
            pl.BlockSpec((1, tm, D_MODEL), lambda i, j: (i, j, 0)),
            pl.BlockSpec((1, 1, tm), lambda i, j: (i, 0, j)),
            _const_spec((D_MODEL, _A_WIDTH)),
            _const_spec((1, Q_LORA)),
            _const_spec((1, KV_LORA)),
            _const_spec((Q_LORA, N_TOK_HEADS * LANES)),
            _const_spec((KV_LORA, N_TOK_HEADS * LANES)),
            _const_spec((TOK_WIDTH, KV_LORA)),
            _const_spec((HALF_ROPE, tm)),
            pl.BlockSpec((1, 1, MEM_WIDTH, N_MEM_HEADS * MEM_LEN), lambda i, j: (0, i, 0, 0)),
            pl.BlockSpec((1, 1, N_MEM_HEADS * MEM_LEN, MEM_WIDTH), lambda i, j: (0, i, 0, 0)),
        ],
        out_specs=[
            pl.BlockSpec((1, N_TOK_HEADS, tm, LANES), lambda i, j: (i, 0, j, 0)),
            pl.BlockSpec((1, N_TOK_HEADS, tm, LANES), lambda i, j: (i, 0, j, 0)),
            pl.BlockSpec((1, N_TOK_HEADS, 1, HEAD_DIM, tm), lambda i, j: (i, 0, j, 0, 0)),
            pl.BlockSpec((1, tm, TOK_WIDTH), lambda i, j: (i, j, 0)),
            pl.BlockSpec((1, tm, MEM_WIDTH), lambda i, j: (i, j, 0)),
        ],
        out_shape=[
            jax.ShapeDtypeStruct((b, N_TOK_HEADS, s, LANES), BF16),
            jax.ShapeDtypeStruct((b, N_TOK_HEADS, s, LANES), BF16),
            jax.ShapeDtypeStruct((b, N_TOK_HEADS, nt, HEAD_DIM, tm), BF16),
            jax.ShapeDtypeStruct((b, s, TOK_WIDTH), BF16),
            jax.ShapeDtypeStruct((b, s, MEM_WIDTH), BF16),
        ],
        compiler_params=pltpu.CompilerParams(
            dimension_semantics=("arbitrary", "arbitrary"), vmem_limit_bytes=VMEM_LIMIT),
        name="mla_proj",
    )(x, pos, w_in_a, qn, kvn, w_uq, w_k, w_vT, rope_c, kbd, vbd)


def _attn_kernel(q_ref, q_next_ref, k_ref, vT_ref, o_ref, s_ref, bm_ref, m_ref, l_ref, acc_ref):
    tq, tk, gw, tv = TQ_ATTN, TK_ATTN, GW_ATTN, TM_PROJ
    groups = range(tq // gw)
    qi = pl.program_id(2)
    last_tile = pl.num_programs(2) - 1
    m_ref[...] = jnp.full((1, tq), -jnp.inf, F32)
    l_ref[...] = jnp.zeros((1, tq), F32)
    acc_ref[...] = jnp.zeros((HEAD_DIM, tq), F32)
    def ones_rows(n):
        return jnp.where(lax.broadcasted_iota(jnp.int32, (BF16_ROWS, n), 0) == 0, 1.0, 0.0).astype(BF16)

    def visible_rows(g, key_off):
        return tk if key_off is None else max(0, min(tk, (g + 1) * gw - key_off))

    def scores(chunk, g, slot, key_off=None, queries=q_ref):
        c0, rows = g * gw, visible_rows(g, key_off)
        if rows == 0:
            return
        start = pl.multiple_of(chunk * tk, tk)
        kt = k_ref[0, 0, pl.ds(start, rows), :]
        st = lax.dot_general(kt, queries[0, 0, c0:c0 + gw, :], _NT, preferred_element_type=F32)
        if key_off is not None and key_off + rows - 1 > c0:
            key = lax.broadcasted_iota(jnp.int32, st.shape, 0) + key_off
            qry = lax.broadcasted_iota(jnp.int32, st.shape, 1) + c0
            st = jnp.where(key <= qry, st, -jnp.inf)
        s_ref[slot, :rows, c0:c0 + gw] = st
        bm_ref[slot, :, c0:c0 + gw] = jnp.max(st, axis=0, keepdims=True)

    def update(chunk, g, slot, key_off=None):
        c0, c1, rows = g * gw, (g + 1) * gw, visible_rows(g, key_off)
        if rows == 0:
            return
        m_old = m_ref[:, c0:c1]
        m_new = jnp.maximum(m_old, bm_ref[slot, :, c0:c1])
        alpha = jnp.exp2(m_old - m_new)
        p = jnp.exp2(s_ref[slot, :rows, c0:c1] - m_new)
        m_ref[:, c0:c1] = m_new
        pb = p.astype(BF16)
        pv = None
        for r0 in range(0, rows, tv):
            r1 = min(rows, r0 + tv)
            v_ones = jnp.concatenate([vT_ref[0, 0, chunk * (tk // tv) + r0 // tv, :, :r1 - r0],
                                      ones_rows(r1 - r0)], axis=0)
            part = jnp.dot(v_ones, pb[r0:r1, :], preferred_element_type=F32)
            pv = part if pv is None else pv + part
        l_ref[:, c0:c1] = alpha * l_ref[:, c0:c1] + pv[HEAD_DIM:HEAD_DIM + 1, :]
        acc_ref[:, c0:c1] = alpha * acc_ref[:, c0:c1] + pv[:HEAD_DIM, :]

    def stage(nxt_chunk, nxt_key_off, cur_chunk, cur_slot, cur_key_off=None, after=None):
        for g in list(groups) + [len(groups)]:
            if nxt_chunk is not None and g < len(groups):
                scores(nxt_chunk, g, 1 - cur_slot, nxt_key_off)
            if g >= 1:
                update(cur_chunk, g - 1, cur_slot, cur_key_off)
                if after is not None:
                    after(g - 1)

    def next_tile_first_chunk(g):
        scores(0, g, 0, None, q_next_ref)

    d = 2 * qi

    def diagonal_tail(look_ahead):
        stage(d + 1, tk, d, 0, 0, next_tile_first_chunk if look_ahead else None)
        stage(None, None, d + 1, 1, tk)
        o_ref[0] = (acc_ref[...] * (1.0 / l_ref[...])).astype(o_ref.dtype)

    @pl.when(qi > 0)
    def _():
        def pair(p, _):
            c = 2 * p
            stage(c + 1, None, c, 0)
            stage(c + 2, None, c + 1, 1)
            return 0

        lax.fori_loop(0, qi - 1, pair, 0)
        stage(d - 1, None, d - 2, 0)
        stage(d, 0, d - 1, 1)

        @pl.when(qi < last_tile)
        def _():
            diagonal_tail(True)

        @pl.when(qi == last_tile)
        def _():
            diagonal_tail(False)

    @pl.when(qi == 0)
    def _():
        for g in groups:
            scores(d, g, 0, 0)
        diagonal_tail(True)


def _attention(q, k, vT):
    b, h, s, _ = q.shape
    tq, tk = TQ_ATTN, TK_ATTN
    nkv = vT.shape[2]
    return pl.pallas_call(
        _attn_kernel,
        grid=(b, h, s // tq),
        in_specs=[
            pl.BlockSpec((1, 1, tq, LANES), lambda i, j, t: (i, j, t, 0)),
            pl.BlockSpec((1, 1, tq, LANES), lambda i, j, t: (i, j, jnp.minimum(t + 1, s // tq - 1), 0)),
            pl.BlockSpec((1, 1, s, LANES), lambda i, j, t: (i, j, 0, 0)),
            pl.BlockSpec((1, 1, nkv, HEAD_DIM, TM_PROJ), lambda i, j, t: (i, j, 0, 0, 0)),
        ],
        out_specs=pl.BlockSpec((1, HEAD_DIM, tq), lambda i, j, t: (i, j, t)),
        out_shape=jax.ShapeDtypeStruct((b, h * HEAD_DIM, s), BF16),
        scratch_shapes=[
            pltpu.VMEM((2, tk, tq), F32),
            pltpu.VMEM((2, 1, tq), F32),
            pltpu.VMEM((1, tq), F32),
            pltpu.VMEM((1, tq), F32),
            pltpu.VMEM((HEAD_DIM, tq), F32),
        ],
        compiler_params=pltpu.CompilerParams(
            dimension_semantics=("arbitrary", "arbitrary", "arbitrary"), vmem_limit_bytes=VMEM_LIMIT),
        name="mla_attn",
    )(q, q, k, vT)


def _out_ln_kernel(tokT_ref, g_ref, mg_ref, h_ref, w_out_ref, lng_ref, lnb_ref, o_ref):
    sub = SUB_OUT
    n_sub = o_ref.shape[1] // sub
    v = {}

    def gate(r):
        rows = slice(r * sub, (r + 1) * sub)
        tok = tokT_ref[0, :, rows].astype(F32).T
        v["y", r] = (tok * g_ref[0, rows, :].astype(F32)).astype(BF16)

    def project(r, c):
        cols = slice(c * MXU_N, (c + 1) * MXU_N)
        v["o", r, c] = (
            jnp.dot(v["y", r], w_out_ref[:TOK_WIDTH, cols], preferred_element_type=F32)
            + jnp.dot(mg_ref[0, r * sub:(r + 1) * sub, :], w_out_ref[TOK_WIDTH:, cols], preferred_element_type=F32))

    def finish(r):
        rows = slice(r * sub, (r + 1) * sub)
        o = jnp.concatenate([v["o", r, c] for c in range(D_MODEL // MXU_N)], axis=1)
        o_ref[0, rows, :] = _layer_norm(ALPHA * h_ref[0, rows, :] + o, lng_ref[...], lnb_ref[...])

    def matmul_pieces(r):
        return [functools.partial(gate, r)] + [functools.partial(project, r, c) for c in range(D_MODEL // MXU_N)]

    _emit_round_robin(matmul_pieces(0))
    for r in range(n_sub):
        _emit_round_robin(matmul_pieces(r + 1) if r + 1 < n_sub else [], [functools.partial(finish, r)])


def _out_ln(tokT, g, mg, h, w_out, ln_g, ln_b):
    b, s, _ = h.shape
    tm = TM_OUT
    return pl.pallas_call(
        _out_ln_kernel,
        grid=(b, s // tm),
        in_specs=[
            pl.BlockSpec((1, TOK_WIDTH, tm), lambda i, j: (i, 0, j)),
            pl.BlockSpec((1, tm, TOK_WIDTH), lambda i, j: (i, j, 0)),
            pl.BlockSpec((1, tm, MEM_WIDTH), lambda i, j: (i, j, 0)),
            pl.BlockSpec((1, tm, D_MODEL), lambda i, j: (i, j, 0)),
            _const_spec((MIX_WIDTH, D_MODEL)),
            _const_spec((1, D_MODEL)),
            _const_spec((1, D_MODEL)),
        ],
        out_specs=pl.BlockSpec((1, tm, D_MODEL), lambda i, j: (i, j, 0)),
        out_shape=jax.ShapeDtypeStruct((b, s, D_MODEL), F32),
        compiler_params=pltpu.CompilerParams(
            dimension_semantics=("arbitrary", "arbitrary"), vmem_limit_bytes=VMEM_LIMIT),
        name="mla_out_ln",
    )(tokT, g, mg, h, w_out, ln_g, ln_b)


_B_U = (0, TOK_WIDTH)
_B_GATE = (_B_U[1], _B_U[1] + MIX_WIDTH)
_B_QM = (_B_GATE[1], _B_GATE[1] + MEM_WIDTH)
_B_WIDTH = _B_QM[1]
GATE_TILE = 256


def _linear_scan(a, b):
    n = a.shape[0]
    row = lax.broadcasted_iota(jnp.int32, a.shape, 0)
    d = 1
    while d < n:
        live = row >= d
        b = a * jnp.where(live, pltpu.roll(b, d, 0), 0.0) + b
        if 2 * d < n:
            a = a * jnp.where(live, pltpu.roll(a, d, 0), 1.0)
        d *= 2
    return b


def _lru_layer_kernel(h_ref, w_in_ref, conv_w_ref, conv_b_ref, w_r_ref, b_r_ref, w_i_ref, b_i_ref, lam_ref,
                      kbd_ref, vbd_ref, w_out_ref, lng_ref, lnb_ref, o_ref,
                      ebuf_ref, tail_ref, carry_ref, perm_in_ref, perm_out_ref):
    tm, sl, ct = SUB_LRU, SUBLANES, GATE_TILE
    n = tm // sl
    halo = (CONV_W - 1) * sl
    lane_tiles = D_MODEL // LANES
    n_sub = TM_LRU // SUB_LRU
    step = pl.program_id(1)

    @pl.when(step == 0)
    def _():
        tail_ref[...] = jnp.zeros((halo, TOK_WIDTH), F32)
        carry_ref[...] = jnp.zeros((sl, TOK_WIDTH), F32)

    sub8 = lax.broadcasted_iota(jnp.int32, (sl, ct), 0)

    def segment_rows(j):
        s, i0 = divmod(sl * j, n)
        return pl.ds(sl * i0 + s, sl, stride=sl)

    def from_previous_segment(prev_tile, cur_tile):
        return jnp.where(sub8 == 0, pltpu.roll(prev_tile, 1, 0), pltpu.roll(cur_tile, 1, 0))

    tails = [tail_ref[:, c * ct:(c + 1) * ct] for c in range(TOK_WIDTH // ct)]
    carries = [carry_ref[:, c * ct:(c + 1) * ct] for c in range(TOK_WIDTH // ct)]

    def sub_tile_phases(r):
        row0 = r * tm
        v = {}

        def load():
            for j in range(n):
                rows = h_ref[0, row0 + j * sl:row0 + (j + 1) * sl, :]
                for c in range(lane_tiles):
                    perm_in_ref[r, c, segment_rows(j), :] = rows[:, c * LANES:(c + 1) * LANES]
            v["hin"] = jnp.concatenate([perm_in_ref[r, c] for c in range(lane_tiles)], axis=1)
            v["hb"] = v["hin"].astype(BF16)

        def project(c):
            v["z", c] = jnp.dot(v["hb"], w_in_ref[:, c * MXU_N:(c + 1) * MXU_N], preferred_element_type=F32)

        def recurrence(c):
            cols = slice(c * ct, (c + 1) * ct)
            u = v["z", _B_U[0] // MXU_N + c]
            for k in range(CONV_W - 1):
                cur = u[tm - halo + k * sl:tm - halo + (k + 1) * sl, :]
                ebuf_ref[r, k * sl:(k + 1) * sl, cols] = from_previous_segment(tails[c][k * sl:(k + 1) * sl, :], cur)
            ebuf_ref[r, halo:halo + tm, cols] = u
            tails[c] = u[tm - halo:, :]
            xc = conv_b_ref[:, cols] + u * conv_w_ref[CONV_W - 1:CONV_W, cols]
            for back in range(1, CONV_W):
                tap = CONV_W - 1 - back
                xc = xc + ebuf_ref[r, halo - back * sl:halo - back * sl + tm, cols] * conv_w_ref[tap:tap + 1, cols]

            xcb = xc.astype(BF16)
            gate_r = _sigmoid(jnp.dot(xcb, w_r_ref[c], preferred_element_type=F32) + b_r_ref[:, cols])
            gate_i = _sigmoid(jnp.dot(xcb, w_i_ref[c], preferred_element_type=F32) + b_i_ref[:, cols])

            neg_lam = -lam_ref[:, cols]
            softplus = jnp.maximum(neg_lam, 0.0) + jnp.log1p(jnp.exp(-jnp.abs(neg_lam)))
            log_a = (-LRU_C * softplus) * gate_r
            a = jnp.exp(log_a)
            one_minus_a2 = -jnp.tanh(log_a) * (a * a + 1.0)
            root = jnp.where(one_minus_a2 > 0.0, one_minus_a2 * lax.rsqrt(one_minus_a2), 0.0)
            b = root * (gate_i * xc)

            h_loc, a_run = [b[0:sl, :]], [a[0:sl, :]]
            for g in range(1, n):
                ag = a[g * sl:(g + 1) * sl, :]
                h_loc.append(ag * h_loc[-1] + b[g * sl:(g + 1) * sl, :])
                a_run.append(ag * a_run[-1])
            state_in = pltpu.roll(carries[c], 1, 0)
            seg_b = jnp.where(sub8 == 0, a_run[-1] * state_in + h_loc[-1], h_loc[-1])
            seg_end = _linear_scan(a_run[-1], seg_b)
            start = from_previous_segment(carries[c], seg_end)
            carries[c] = seg_end
            v["hs", c] = jnp.concatenate([h_loc[g] + a_run[g] * start for g in range(n)], axis=0)

        def gate(c):
            g = _silu(v["z", _B_GATE[0] // MXU_N + c])
            if c < TOK_WIDTH // ct:
                v["y", c] = (v["hs", c] * g).astype(BF16)
            else:
                mem_out = _memory_attention(v["z", _B_QM[0] // MXU_N], kbd_ref.at[0, 0], vbd_ref.at[0, 0])
                v["y", c] = (mem_out * g).astype(BF16)

        def out_project(c):
            if c == 0:
                v["yb"] = jnp.concatenate([v["y", k] for k in range(MIX_WIDTH // ct)], axis=1)
            v["o", c] = jnp.dot(v["yb"], w_out_ref[:, c * MXU_N:(c + 1) * MXU_N], preferred_element_type=F32)

        def finish():
            o = jnp.concatenate([v["o", c] for c in range(D_MODEL // MXU_N)], axis=1)
            out = _layer_norm(ALPHA * v["hin"] + o, lng_ref[...], lnb_ref[...])
            for c in range(lane_tiles):
                perm_out_ref[r, c] = out[:, c * LANES:(c + 1) * LANES]
            for j in range(n):
                o_ref[0, row0 + j * sl:row0 + (j + 1) * sl, :] = jnp.concatenate(
                    [perm_out_ref[r, c, segment_rows(j), :] for c in range(lane_tiles)], axis=1)

        n_tok, n_mix = TOK_WIDTH // ct, MIX_WIDTH // ct
        phase_a = [load] + [functools.partial(project, c) for c in range(_B_WIDTH // MXU_N)]
        phase_b = []
        for c in range(n_mix):
            if c < n_tok:
                phase_b.append(functools.partial(recurrence, c))
            phase_b.append(functools.partial(gate, c))
        phase_cd = [functools.partial(out_project, c) for c in range(D_MODEL // MXU_N)] + [finish]
        return phase_a, phase_b, phase_cd

    phases = [sub_tile_phases(r) for r in range(n_sub)]
    _emit_round_robin(phases[0][0])
    for r in range(n_sub + 1):
        vector_heavy = phases[r][1] if r < n_sub else []
        matmul_heavy = (phases[r + 1][0] if r + 1 < n_sub else []) + (phases[r - 1][2] if r >= 1 else [])
        _emit_round_robin(matmul_heavy, vector_heavy)

    for c in range(TOK_WIDTH // ct):
        tail_ref[:, c * ct:(c + 1) * ct] = tails[c]
        carry_ref[:, c * ct:(c + 1) * ct] = carries[c]


def _lru_layer(h, w_in, conv_w, conv_b, w_r, b_r, w_i, b_i, lam, kbd, vbd, w_out, ln_g, ln_b):
    b, s, _ = h.shape
    tm = TM_LRU
    n_gate_tiles = TOK_WIDTH // GATE_TILE
    n_sub = TM_LRU // SUB_LRU
    return pl.pallas_call(
        _lru_layer_kernel,
        grid=(b, s // tm),
        in_specs=[
            pl.BlockSpec((1, tm, D_MODEL), lambda i, j: (i, j, 0)),
            _const_spec((D_MODEL, _B_WIDTH)),
            _const_spec((CONV_W, TOK_WIDTH)),
            _const_spec((1, TOK_WIDTH)),
            _const_spec((n_gate_tiles, GATE_TILE, GATE_TILE)),
            _const_spec((1, TOK_WIDTH)),
            _const_spec((n_gate_tiles, GATE_TILE, GATE_TILE)),
            _const_spec((1, TOK_WIDTH)),
            _const_spec((1, TOK_WIDTH)),
            pl.BlockSpec((1, 1, MEM_WIDTH, N_MEM_HEADS * MEM_LEN), lambda i, j: (1, i, 0, 0)),
            pl.BlockSpec((1, 1, N_MEM_HEADS * MEM_LEN, MEM_WIDTH), lambda i, j: (1, i, 0, 0)),
            _const_spec((MIX_WIDTH, D_MODEL)),
            _const_spec((1, D_MODEL)),
            _const_spec((1, D_MODEL)),
        ],
        out_specs=pl.BlockSpec((1, tm, D_MODEL), lambda i, j: (i, j, 0)),
        out_shape=jax.ShapeDtypeStruct((b, s, D_MODEL), F32),
        scratch_shapes=[
            pltpu.VMEM((n_sub, SUB_LRU + (CONV_W - 1) * SUBLANES, TOK_WIDTH), F32),
            pltpu.VMEM(((CONV_W - 1) * SUBLANES, TOK_WIDTH), F32),
            pltpu.VMEM((SUBLANES, TOK_WIDTH), F32),
            pltpu.VMEM((n_sub, D_MODEL // LANES, SUB_LRU, LANES), F32),
            pltpu.VMEM((n_sub, D_MODEL // LANES, SUB_LRU, LANES), F32),
        ],
        compiler_params=pltpu.CompilerParams(
            dimension_semantics=("arbitrary", "arbitrary"), vmem_limit_bytes=VMEM_LIMIT),
        name="lru_layer",
    )(h, w_in, conv_w, conv_b, w_r, b_r, w_i, b_i, lam, kbd, vbd, w_out, ln_g, ln_b)


def _block_diag_tiles(w):
    per_tile = GATE_TILE // HEAD_DIM
    w = w.reshape(TOK_WIDTH // GATE_TILE, per_tile, HEAD_DIM, HEAD_DIM)
    eye = jnp.eye(per_tile, dtype=w.dtype)
    t = w[:, :, :, None, :] * eye[None, :, None, :, None]
    return t.reshape(TOK_WIDTH // GATE_TILE, GATE_TILE, GATE_TILE)


def _rope_frequencies(tm):
    inv_freq = ROPE_THETA ** (-np.arange(HALF_ROPE, dtype=np.float32) / np.float32(HALF_ROPE))
    return np.ascontiguousarray(np.broadcast_to(inv_freq[:, None], (HALF_ROPE, tm)))


def _head_lanes(w, nope, t1, t2):
    zero = w.shape[-1]
    idx = np.full((LANES,), zero, np.int32)
    head = LANES // 2 - HALF_ROPE
    if t1 is not None:
        idx[:HALF_ROPE] = t1 + np.arange(HALF_ROPE)
        idx[LANES // 2:LANES // 2 + HALF_ROPE] = t2 + np.arange(HALF_ROPE)
    if nope is not None:
        idx[HALF_ROPE:LANES // 2] = nope + np.arange(head)
        idx[LANES // 2 + HALF_ROPE:LANES // 2 + HALF_ROPE + QK_NOPE - head] = nope + head + np.arange(QK_NOPE - head)
    w_ext = jnp.concatenate([w, jnp.zeros(w.shape[:-1] + (1,), w.dtype)], axis=-1)
    return jnp.take(w_ext, jnp.asarray(idx), axis=-1)


def kernel(x, mem, positions, mla_w_in, mla_q_norm, mla_w_uq, mla_kv_norm, mla_w_ukv, lru_w_in, lru_conv_w,
           lru_conv_b, lru_w_rgate, lru_b_rgate, lru_w_igate, lru_b_igate, lru_lambda, w_mem_kv, w_out, ln_g, ln_b):
    b, s, _ = x.shape
    assert s % TQ_ATTN == 0 and s % TM_LRU == 0 and s % TM_OUT == 0 and s % TM_PROJ == 0 and TK_ATTN % TM_PROJ == 0

    kbd, vbd = _mem_kv(mem, w_mem_kv)

    w_in = mla_w_in[0]
    o_q, o_kv, o_kr, o_gate = Q_LORA, Q_LORA + KV_LORA, Q_LORA + KV_LORA + QK_ROPE, Q_LORA + KV_LORA + QK_ROPE + MIX_WIDTH
    w_kr = _head_lanes(w_in[:, o_kv:o_kr], None, 0, HALF_ROPE)
    w_in_a = jnp.concatenate(
        [w_in[:, :o_q], w_in[:, o_q:o_kv], w_kr, w_in[:, o_kr:o_gate], w_in[:, o_gate:]], axis=1).astype(BF16)
    w_uq = _head_lanes(mla_w_uq[0].reshape(Q_LORA, N_TOK_HEADS, QK_DIM), 0, QK_NOPE, QK_NOPE + HALF_ROPE)
    w_uq = w_uq.reshape(Q_LORA, N_TOK_HEADS * LANES).astype(BF16)
    w_ukv = mla_w_ukv[0].reshape(KV_LORA, N_TOK_HEADS, QK_NOPE + HEAD_DIM)
    w_k = _head_lanes(w_ukv[:, :, :QK_NOPE], 0, None, None)
    w_k = w_k.reshape(KV_LORA, N_TOK_HEADS * LANES).astype(BF16)
    w_vT = w_ukv[:, :, QK_NOPE:].reshape(KV_LORA, TOK_WIDTH).T.astype(BF16)
    pos = positions.astype(F32)[:, None, :]
    w_out_b = w_out.astype(BF16)

    q, k, vT, g, mg = _mla_proj(x, pos, w_in_a, mla_q_norm[0][None], mla_kv_norm[0][None], w_uq, w_k, w_vT,
                                jnp.asarray(_rope_frequencies(TM_PROJ)), kbd, vbd)
    tokT = _attention(q, k, vT)
    h1 = _out_ln(tokT, g, mg, x, w_out_b[0], ln_g[0][None], ln_b[0][None])

    return _lru_layer(h1, lru_w_in[0].astype(BF16), lru_conv_w[0], lru_conv_b[0][None],
                      _block_diag_tiles(lru_w_rgate[0]).astype(BF16), lru_b_rgate[0][None],
                      _block_diag_tiles(lru_w_igate[0]).astype(BF16), lru_b_igate[0][None],
                      lru_lambda[0][None], kbd, vbd, w_out_b[1], ln_g[1][None], ln_b[1][None])
```

```python
import functools
import math

import jax
import jax.numpy as jnp
import numpy as np
from jax import lax
from jax.experimental import pallas as pl
from jax.experimental.pallas import tpu as pltpu

F32 = jnp.float32
BF16 = jnp.bfloat16

D_MODEL = 1024
DEPTH = 2
MEM_LEN = 256
HEAD_DIM = 64
N_MEM_HEADS = 4
N_TOK_HEADS = 12
TOK_WIDTH = N_TOK_HEADS * HEAD_DIM
MEM_WIDTH = N_MEM_HEADS * HEAD_DIM
MIX_WIDTH = TOK_WIDTH + MEM_WIDTH
Q_LORA = 384
KV_LORA = 256
QK_NOPE = 64
QK_ROPE = 32
QK_DIM = QK_NOPE + QK_ROPE
ROPE_THETA = 10000.0
CONV_W = 4
LRU_C = 8.0
ALPHA = (2.0 * DEPTH) ** 0.25
NORM_EPS = 1e-6

MXU_N = 256
LANES = 128
SUBLANES = 8
BF16_ROWS = 16
HALF_ROPE = QK_ROPE // 2
LOG2E = math.log2(math.e)
Q_SCALE = LOG2E / math.sqrt(QK_DIM)

TM_PROJ = 1024
SUB_PROJ = 256
TK_ATTN = 1024
TQ_ATTN = 2 * TK_ATTN
GW_ATTN = 256
TM_OUT = 1024
SUB_OUT = 256
TM_LRU = 1024
SUB_LRU = 256
VMEM_LIMIT = 48 * 1024 * 1024

_NT = (((1,), (1,)), ((), ()))


def _const_spec(shape):
    nd = len(shape)
    return pl.BlockSpec(shape, lambda *_: (0,) * nd, pipeline_mode=pl.Buffered(1))


def _dot_cols(x, w_ref, lo, hi):
    parts = [jnp.dot(x, w_ref[:, c:min(c + MXU_N, hi)], preferred_element_type=F32) for c in range(lo, hi, MXU_N)]
    return parts[0] if len(parts) == 1 else jnp.concatenate(parts, axis=-1)


def _emit_round_robin(*piece_lists):
    lists = [list(p) for p in piece_lists if p]
    longest = max((len(p) for p in lists), default=0)
    for k in range(longest):
        for p in lists:
            lo, hi = (k * len(p)) // longest, ((k + 1) * len(p)) // longest
            for piece in p[lo:hi]:
                piece()


def _sigmoid(t):
    return 0.5 * jnp.tanh(0.5 * t) + 0.5


def _silu(t):
    h = 0.5 * t
    return h * jnp.tanh(h) + h


def _rms_norm(t, g):
    return t * lax.rsqrt(jnp.mean(t * t, axis=-1, keepdims=True) + NORM_EPS) * g


def _layer_norm(t, g, b):
    mu = jnp.mean(t, axis=-1, keepdims=True)
    c = t - mu
    var = jnp.mean(c * c, axis=-1, keepdims=True)
    return c * lax.rsqrt(var + NORM_EPS) * g + b


def _memory_attention(qm, kbd_ref, vbd_ref):
    s = _dot_cols(qm.astype(BF16), kbd_ref, 0, N_MEM_HEADS * MEM_LEN)
    ps = []
    for h in range(N_MEM_HEADS):
        sh = s[:, h * MEM_LEN:(h + 1) * MEM_LEN]
        e = jnp.exp2(sh - jnp.max(sh, axis=-1, keepdims=True))
        ps.append((e * (1.0 / jnp.sum(e, axis=-1, keepdims=True))).astype(BF16))
    p = jnp.concatenate(ps, axis=-1)
    return jnp.dot(p, vbd_ref[...], preferred_element_type=F32)


def _mem_kv_kernel(mem_ref, wkT_ref, wv_ref, kbd_ref, vbd_ref):
    mb = mem_ref[0].astype(BF16)
    kT = lax.dot_general(wkT_ref[0].astype(BF16), mb, _NT, preferred_element_type=F32)
    kT = (kT * (LOG2E / math.sqrt(HEAD_DIM))).astype(BF16)
    v = jnp.dot(mb, wv_ref[0].astype(BF16), preferred_element_type=F32).astype(BF16)
    row = lax.broadcasted_iota(jnp.int32, (MEM_WIDTH, MEM_LEN), 0)
    col = lax.broadcasted_iota(jnp.int32, (MEM_LEN, MEM_WIDTH), 1)
    zero = jnp.zeros((), BF16)
    for h in range(N_MEM_HEADS):
        lo, hi = h * HEAD_DIM, (h + 1) * HEAD_DIM
        kbd_ref[0, 0, :, h * MEM_LEN:(h + 1) * MEM_LEN] = jnp.where((row >= lo) & (row < hi), kT, zero)
        vbd_ref[0, 0, h * MEM_LEN:(h + 1) * MEM_LEN, :] = jnp.where((col >= lo) & (col < hi), v, zero)


def _mem_kv(mem, w_mem_kv):
    b = mem.shape[0]
    wkT = jnp.swapaxes(w_mem_kv[:, :, :MEM_WIDTH], 1, 2)
    wv = w_mem_kv[:, :, MEM_WIDTH:]
    return pl.pallas_call(
        _mem_kv_kernel,
        grid=(DEPTH, b),
        in_specs=[
            pl.BlockSpec((1, MEM_LEN, D_MODEL), lambda l, i: (i, 0, 0)),
            pl.BlockSpec((1, MEM_WIDTH, D_MODEL), lambda l, i: (l, 0, 0)),
            pl.BlockSpec((1, D_MODEL, MEM_WIDTH), lambda l, i: (l, 0, 0)),
        ],
        out_specs=[
            pl.BlockSpec((1, 1, MEM_WIDTH, N_MEM_HEADS * MEM_LEN), lambda l, i: (l, i, 0, 0)),
            pl.BlockSpec((1, 1, N_MEM_HEADS * MEM_LEN, MEM_WIDTH), lambda l, i: (l, i, 0, 0)),
        ],
        out_shape=[
            jax.ShapeDtypeStruct((DEPTH, b, MEM_WIDTH, N_MEM_HEADS * MEM_LEN), BF16),
            jax.ShapeDtypeStruct((DEPTH, b, N_MEM_HEADS * MEM_LEN, MEM_WIDTH), BF16),
        ],
        name="mem_kv",
    )(mem, wkT, wv)


_A_CQ = (0, Q_LORA)
_A_CKV = (_A_CQ[1], _A_CQ[1] + KV_LORA)
_A_KR = (_A_CKV[1], _A_CKV[1] + LANES)
_A_GATE = (_A_KR[1], _A_KR[1] + MIX_WIDTH)
_A_QM = (_A_GATE[1], _A_GATE[1] + MEM_WIDTH)
_A_WIDTH = _A_QM[1]


def _mla_proj_kernel(x_ref, pos_ref, w_in_ref, qn_ref, kvn_ref, w_uq_ref, w_k_ref, w_vT_ref, rope_ref,
                     kbd_ref, vbd_ref, q_ref, k_ref, vT_ref, g_ref, mg_ref):
    sub = SUB_PROJ
    n_sub = x_ref.shape[1] // sub
    heads_per_tile = MXU_N // LANES

    ang = rope_ref[...] * pos_ref[0]
    cos_t, sin_t = jnp.cos(ang), jnp.sin(ang)
    gap = LANES // 2 - HALF_ROPE
    ones, zeros = jnp.ones((gap, ang.shape[1]), F32), jnp.zeros((gap, ang.shape[1]), F32)
    cos_all = jnp.concatenate([cos_t, ones, cos_t, ones], axis=0).T
    sin_all = jnp.concatenate([-sin_t, zeros, sin_t, zeros], axis=0).T

    def sub_tile_pieces(r):
        rows = slice(r * sub, (r + 1) * sub)
        cos, sin = cos_all[rows, :], sin_all[rows, :]
        v = {}

        def rope(t):
            return t * cos + pltpu.roll(t, LANES // 2, 1) * sin

        def load():
            v["xb"] = x_ref[0, rows, :].astype(BF16)

        def project(c):
            v["z", c] = jnp.dot(v["xb"], w_in_ref[:, c * MXU_N:(c + 1) * MXU_N], preferred_element_type=F32)

        def seg(s):
            z = jnp.concatenate([v["z", c] for c in range(s[0] // MXU_N, (s[1] - 1) // MXU_N + 1)], axis=1)
            lo = s[0] - (s[0] // MXU_N) * MXU_N
            return z[:, lo:lo + s[1] - s[0]]

        def q_norm():
            v["cqn"] = _rms_norm(seg(_A_CQ), qn_ref[...]).astype(BF16)

        def q_heads(t):
            q = jnp.dot(v["cqn"], w_uq_ref[:, t * MXU_N:(t + 1) * MXU_N], preferred_element_type=F32)
            for i in range(heads_per_tile):
                q_ref[0, t * heads_per_tile + i, rows, :] = (
                    rope(q[:, i * LANES:(i + 1) * LANES]) * Q_SCALE).astype(BF16)

        def kv_norm():
            v["ckvn"] = _rms_norm(seg(_A_CKV), kvn_ref[...]).astype(BF16)
            v["k_rope"] = rope(seg(_A_KR))

        def k_heads(t):
            k = jnp.dot(v["ckvn"], w_k_ref[:, t * MXU_N:(t + 1) * MXU_N], preferred_element_type=F32)
            for i in range(heads_per_tile):
                k_ref[0, t * heads_per_tile + i, rows, :] = (k[:, i * LANES:(i + 1) * LANES] + v["k_rope"]).astype(BF16)

        def values():
            vT = lax.dot_general(w_vT_ref[...], v["ckvn"], _NT, preferred_element_type=F32)
            for h in range(N_TOK_HEADS):
                vT_ref[0, h, 0, :, rows] = vT[h * HEAD_DIM:(h + 1) * HEAD_DIM, :].astype(BF16)

        def gates():
            g = _silu(seg(_A_GATE))
            g_ref[0, rows, :] = g[:, :TOK_WIDTH].astype(g_ref.dtype)
            mem_out = _memory_attention(seg(_A_QM), kbd_ref.at[0, 0], vbd_ref.at[0, 0])
            mg_ref[0, rows, :] = (mem_out * g[:, TOK_WIDTH:]).astype(mg_ref.dtype)

        n_tiles = N_TOK_HEADS // heads_per_tile
        matmul_heavy = [load] + [functools.partial(project, c) for c in range(_A_WIDTH // MXU_N)]
        vector_heavy = ([q_norm] + [functools.partial(q_heads, t) for t in range(n_tiles)] + [kv_norm]
                        + [functools.partial(k_heads, t) for t in range(n_tiles)] + [values, gates])
        return matmul_heavy, vector_heavy

    pieces = [sub_tile_pieces(r) for r in range(n_sub)]
    _emit_round_robin(pieces[0][0])
    for r in range(n_sub):
        _emit_round_robin(pieces[r + 1][0] if r + 1 < n_sub else [], pieces[r][1])


def _mla_proj(x, pos, w_in_a, qn, kvn, w_uq, w_k, w_vT, rope_c, kbd, vbd):
    b, s, _ = x.shape
    tm = TM_PROJ
    nt = s // tm
    return pl.pallas_call(
        _mla_proj_kernel,
        grid=(b, nt),
        in_specs=[
            pl.BlockSpec((1, tm, D_MODEL), lambda i, j: (i, j, 0)),
            pl.BlockSpec((1, 1, tm), lambda i, j: (i, 0, j)),
            _const_spec((D_MODEL, _A_WIDTH)),
            _const_spec((1, Q_LORA)),
            _const_spec((1, KV_LORA)),
            _const_spec((Q_LORA, N_TOK_HEADS * LANES)),
            _const_spec((KV_LORA, N_TOK_HEADS * LANES)),
            _const_spec((TOK_WIDTH, KV_LORA)),
            _const_spec((HALF_ROPE, tm)),
            pl.BlockSpec((1, 1, MEM_WIDTH, N_MEM_HEADS * MEM_LEN), lambda i, j: (0, i, 0, 0)),
            pl.BlockSpec((1, 1, N_MEM_HEADS * MEM_LEN, MEM_WIDTH), lambda i, j: (0, i, 0, 0)),
        ],
        out_specs=[
            pl.BlockSpec((1, N_TOK_HEADS, tm, LANES), lambda i, j: (i, 0, j, 0)),
            pl.BlockSpec((1, N_TOK_HEADS, tm, LANES), lambda i, j: (i, 0, j, 0)),
            pl.BlockSpec((1, N_TOK_HEADS, 1, HEAD_DIM, tm), lambda i, j: (i, 0, j, 0, 0)),
            pl.BlockSpec((1, tm, TOK_WIDTH), lambda i, j: (i, j, 0)),
            pl.BlockSpec((1, tm, MEM_WIDTH), lambda i, j: (i, j, 0)),
        ],
        out_shape=[
            jax.ShapeDtypeStruct((b, N_TOK_HEADS, s, LANES), BF16),
            jax.ShapeDtypeStruct((b, N_TOK_HEADS, s, LANES), BF16),
            jax.ShapeDtypeStruct((b, N_TOK_HEADS, nt, HEAD_DIM, tm), BF16),
            jax.ShapeDtypeStruct((b, s, TOK_WIDTH), BF16),
            jax.ShapeDtypeStruct((b, s, MEM_WIDTH), BF16),
        ],
        compiler_params=pltpu.CompilerParams(
            dimension_semantics=("arbitrary", "arbitrary"), vmem_limit_bytes=VMEM_LIMIT),
        name="mla_proj",
    )(x, pos, w_in_a, qn, kvn, w_uq, w_k, w_vT, rope_c, kbd, vbd)


def _attn_kernel(q_ref, q_next_ref, k_ref, vT_ref, o_ref, s_ref, bm_ref, m_ref, l_ref, acc_ref):
    tq, tk, gw, tv = TQ_ATTN, TK_ATTN, GW_ATTN, TM_PROJ
    groups = range(tq // gw)
    qi = pl.program_id(2)
    last_tile = pl.num_programs(2) - 1
    m_ref[...] = jnp.full((1, tq), -jnp.inf, F32)
    l_ref[...] = jnp.zeros((1, tq), F32)
    acc_ref[...] = jnp.zeros((HEAD_DIM, tq), F32)
    def ones_rows(n):
        return jnp.where(lax.broadcasted_iota(jnp.int32, (BF16_ROWS, n), 0) == 0, 1.0, 0.0).astype(BF16)

    def visible_rows(g, key_off):
        return tk if key_off is None else max(0, min(tk, (g + 1) * gw - key_off))

    def scores(chunk, g, slot, key_off=None, queries=q_ref):
        c0, rows = g * gw, visible_rows(g, key_off)
        if rows == 0:
            return
        start = pl.multiple_of(chunk * tk, tk)
        kt = k_ref[0, 0, pl.ds(start, rows), :]
        st = lax.dot_general(kt, queries[0, 0, c0:c0 + gw, :], _NT, preferred_element_type=F32)
        if key_off is not None and key_off + rows - 1 > c0:
            key = lax.broadcasted_iota(jnp.int32, st.shape, 0) + key_off
            qry = lax.broadcasted_iota(jnp.int32, st.shape, 1) + c0
            st = jnp.where(key <= qry, st, -jnp.inf)
        s_ref[slot, :rows, c0:c0 + gw] = st
        bm_ref[slot, :, c0:c0 + gw] = jnp.max(st, axis=0, keepdims=True)

    def update(chunk, g, slot, key_off=None):
        c0, c1, rows = g * gw, (g + 1) * gw, visible_rows(g, key_off)
        if rows == 0:
            return
        m_old = m_ref[:, c0:c1]
        m_new = jnp.maximum(m_old, bm_ref[slot, :, c0:c1])
        alpha = jnp.exp2(m_old - m_new)
        pb = jnp.exp2((s_ref[slot, :rows, c0:c1] - m_new).astype(BF16))
        m_ref[:, c0:c1] = m_new
        pv = None
        for r0 in range(0, rows, tv):
            r1 = min(rows, r0 + tv)
            v_ones = jnp.concatenate([vT_ref[0, 0, chunk * (tk // tv) + r0 // tv, :, :r1 - r0],
                                      ones_rows(r1 - r0)], axis=0)
            part = jnp.dot(v_ones, pb[r0:r1, :], preferred_element_type=F32)
            pv = part if pv is None else pv + part
        l_ref[:, c0:c1] = alpha * l_ref[:, c0:c1] + pv[HEAD_DIM:HEAD_DIM + 1, :]
        acc_ref[:, c0:c1] = alpha * acc_ref[:, c0:c1] + pv[:HEAD_DIM, :]

    def stage(nxt_chunk, nxt_key_off, cur_chunk, cur_slot, cur_key_off=None, after=None):
        for g in list(groups) + [len(groups)]:
            if nxt_chunk is not None and g < len(groups):
                scores(nxt_chunk, g, 1 - cur_slot, nxt_key_off)
            if g >= 1:
                update(cur_chunk, g - 1, cur_slot, cur_key_off)
                if after is not None:
                    after(g - 1)

    def next_tile_first_chunk(g):
        scores(0, g, 0, None, q_next_ref)

    d = 2 * qi

    def diagonal_tail(look_ahead):
        stage(d + 1, tk, d, 0, 0, next_tile_first_chunk if look_ahead else None)
        stage(None, None, d + 1, 1, tk)
        o_ref[0] = (acc_ref[...] * (1.0 / l_ref[...])).astype(o_ref.dtype)

    @pl.when(qi > 0)
    def _():
        def pair(p, _):
            c = 2 * p
            stage(c + 1, None, c, 0)
            stage(c + 2, None, c + 1, 1)
            return 0

        lax.fori_loop(0, qi - 1, pair, 0)
        stage(d - 1, None, d - 2, 0)
        stage(d, 0, d - 1, 1)

        @pl.when(qi < last_tile)
        def _():
            diagonal_tail(True)

        @pl.when(qi == last_tile)
        def _():
            diagonal_tail(False)

    @pl.when(qi == 0)
    def _():
        for g in groups:
            scores(d, g, 0, 0)
        diagonal_tail(True)


def _attention(q, k, vT):
    b, h, s, _ = q.shape
    tq, tk = TQ_ATTN, TK_ATTN
    nkv = vT.shape[2]
    return pl.pallas_call(
        _attn_kernel,
        grid=(b, h, s // tq),
        in_specs=[
            pl.BlockSpec((1, 1, tq, LANES), lambda i, j, t: (i, j, t, 0)),
            pl.BlockSpec((1, 1, tq, LANES), lambda i, j, t: (i, j, jnp.minimum(t + 1, s // tq - 1), 0)),
            pl.BlockSpec((1, 1, s, LANES), lambda i, j, t: (i, j, 0, 0)),
            pl.BlockSpec((1, 1, nkv, HEAD_DIM, TM_PROJ), lambda i, j, t: (i, j, 0, 0, 0)),
        ],
        out_specs=pl.BlockSpec((1, HEAD_DIM, tq), lambda i, j, t: (i, j, t)),
        out_shape=jax.ShapeDtypeStruct((b, h * HEAD_DIM, s), BF16),
        scratch_shapes=[
            pltpu.VMEM((2, tk, tq), F32),
            pltpu.VMEM((2, 1, tq), F32),
            pltpu.VMEM((1, tq), F32),
            pltpu.VMEM((1, tq), F32),
            pltpu.VMEM((HEAD_DIM, tq), F32),
        ],
        compiler_params=pltpu.CompilerParams(
            dimension_semantics=("arbitrary", "arbitrary", "arbitrary"), vmem_limit_bytes=VMEM_LIMIT),
        name="mla_attn",
    )(q, q, k, vT)


def _out_ln_kernel(tokT_ref, g_ref, mg_ref, h_ref, w_out_ref, lng_ref, lnb_ref, o_ref):
    sub = SUB_OUT
    n_sub = o_ref.shape[1] // sub
    v = {}

    def gate(r):
        rows = slice(r * sub, (r + 1) * sub)
        tok = tokT_ref[0, :, rows].astype(F32).T
        v["y", r] = (tok * g_ref[0, rows, :].astype(F32)).astype(BF16)

    def project(r, c):
        cols = slice(c * MXU_N, (c + 1) * MXU_N)
        v["o", r, c] = (
            jnp.dot(v["y", r], w_out_ref[:TOK_WIDTH, cols], preferred_element_type=F32)
            + jnp.dot(mg_ref[0, r * sub:(r + 1) * sub, :], w_out_ref[TOK_WIDTH:, cols], preferred_element_type=F32))

    def finish(r):
        rows = slice(r * sub, (r + 1) * sub)
        o = jnp.concatenate([v["o", r, c] for c in range(D_MODEL // MXU_N)], axis=1)
        o_ref[0, rows, :] = _layer_norm(ALPHA * h_ref[0, rows, :] + o, lng_ref[...], lnb_ref[...])

    def matmul_pieces(r):
        return [functools.partial(gate, r)] + [functools.partial(project, r, c) for c in range(D_MODEL // MXU_N)]

    _emit_round_robin(matmul_pieces(0))
    for r in range(n_sub):
        _emit_round_robin(matmul_pieces(r + 1) if r + 1 < n_sub else [], [functools.partial(finish, r)])


def _out_ln(tokT, g, mg, h, w_out, ln_g, ln_b):
    b, s, _ = h.shape
    tm = TM_OUT
    return pl.pallas_call(
        _out_ln_kernel,
        grid=(b, s // tm),
        in_specs=[
            pl.BlockSpec((1, TOK_WIDTH, tm), lambda i, j: (i, 0, j)),
            pl.BlockSpec((1, tm, TOK_WIDTH), lambda i, j: (i, j, 0)),
            pl.BlockSpec((1, tm, MEM_WIDTH), lambda i, j: (i, j, 0)),
            pl.BlockSpec((1, tm, D_MODEL), lambda i, j: (i, j, 0)),
            _const_spec((MIX_WIDTH, D_MODEL)),
            _const_spec((1, D_MODEL)),
            _const_spec((1, D_MODEL)),
        ],
        out_specs=pl.BlockSpec((1, tm, D_MODEL), lambda i, j: (i, j, 0)),
        out_shape=jax.ShapeDtypeStruct((b, s, D_MODEL), F32),
        compiler_params=pltpu.CompilerParams(
            dimension_semantics=("arbitrary", "arbitrary"), vmem_limit_bytes=VMEM_LIMIT),
        name="mla_out_ln",
    )(tokT, g, mg, h, w_out, ln_g, ln_b)


_B_U = (0, TOK_WIDTH)
_B_GATE = (_B_U[1], _B_U[1] + MIX_WIDTH)
_B_QM = (_B_GATE[1], _B_GATE[1] + MEM_WIDTH)
_B_WIDTH = _B_QM[1]
GATE_TILE = 256


def _linear_scan(a, b):
    n = a.shape[0]
    row = lax.broadcasted_iota(jnp.int32, a.shape, 0)
    d = 1
    while d < n:
        live = row >= d
        b = a * jnp.where(live, pltpu.roll(b, d, 0), 0.0) + b
        if 2 * d < n:
            a = a * jnp.where(live, pltpu.roll(a, d, 0), 1.0)
        d *= 2
    return b


def _lru_layer_kernel(h_ref, w_in_ref, conv_w_ref, conv_b_ref, w_r_ref, b_r_ref, w_i_ref, b_i_ref, lam_ref,
                      kbd_ref, vbd_ref, w_out_ref, lng_ref, lnb_ref, o_ref,
                      ebuf_ref, tail_ref, carry_ref, perm_in_ref, perm_out_ref):
    tm, sl, ct = SUB_LRU, SUBLANES, GATE_TILE
    n = tm // sl
    halo = (CONV_W - 1) * sl
    lane_tiles = D_MODEL // LANES
    n_sub = TM_LRU // SUB_LRU
    step = pl.program_id(1)

    @pl.when(step == 0)
    def _():
        tail_ref[...] = jnp.zeros((halo, TOK_WIDTH), F32)
        carry_ref[...] = jnp.zeros((sl, TOK_WIDTH), F32)

    sub8 = lax.broadcasted_iota(jnp.int32, (sl, ct), 0)

    def segment_rows(j):
        s, i0 = divmod(sl * j, n)
        return pl.ds(sl * i0 + s, sl, stride=sl)

    def from_previous_segment(prev_tile, cur_tile):
        return jnp.where(sub8 == 0, pltpu.roll(prev_tile, 1, 0), pltpu.roll(cur_tile, 1, 0))

    tails = [tail_ref[:, c * ct:(c + 1) * ct] for c in range(TOK_WIDTH // ct)]
    carries = [carry_ref[:, c * ct:(c + 1) * ct] for c in range(TOK_WIDTH // ct)]

    def sub_tile_phases(r):
        row0 = r * tm
        v = {}

        def load():
            for j in range(n):
                rows = h_ref[0, row0 + j * sl:row0 + (j + 1) * sl, :]
                for c in range(lane_tiles):
                    perm_in_ref[r, c, segment_rows(j), :] = rows[:, c * LANES:(c + 1) * LANES]
            v["hin"] = jnp.concatenate([perm_in_ref[r, c] for c in range(lane_tiles)], axis=1)
            v["hb"] = v["hin"].astype(BF16)

        def project(c):
            v["z", c] = jnp.dot(v["hb"], w_in_ref[:, c * MXU_N:(c + 1) * MXU_N], preferred_element_type=F32)

        def recurrence(c):
            cols = slice(c * ct, (c + 1) * ct)
            u = v["z", _B_U[0] // MXU_N + c]
            for k in range(CONV_W - 1):
                cur = u[tm - halo + k * sl:tm - halo + (k + 1) * sl, :]
                ebuf_ref[r, k * sl:(k + 1) * sl, cols] = from_previous_segment(tails[c][k * sl:(k + 1) * sl, :], cur)
            ebuf_ref[r, halo:halo + tm, cols] = u
            tails[c] = u[tm - halo:, :]
            xc = conv_b_ref[:, cols] + u * conv_w_ref[CONV_W - 1:CONV_W, cols]
            for back in range(1, CONV_W):
                tap = CONV_W - 1 - back
                xc = xc + ebuf_ref[r, halo - back * sl:halo - back * sl + tm, cols] * conv_w_ref[tap:tap + 1, cols]

            xcb = xc.astype(BF16)
            gate_r = _sigmoid(jnp.dot(xcb, w_r_ref[c], preferred_element_type=F32) + b_r_ref[:, cols])
            gate_i = _sigmoid(jnp.dot(xcb, w_i_ref[c], preferred_element_type=F32) + b_i_ref[:, cols])

            neg_lam = -lam_ref[:, cols]
            softplus = jnp.maximum(neg_lam, 0.0) + jnp.log1p(jnp.exp(-jnp.abs(neg_lam)))
            log_a = (-LRU_C * softplus) * gate_r
            a = jnp.exp(log_a)
            one_minus_a2 = -jnp.tanh(log_a) * (a * a + 1.0)
            root = jnp.where(one_minus_a2 > 0.0, one_minus_a2 * lax.rsqrt(one_minus_a2), 0.0)
            b = root * (gate_i * xc)

            h_loc, a_run = [b[0:sl, :]], [a[0:sl, :]]
            for g in range(1, n):
                ag = a[g * sl:(g + 1) * sl, :]
                h_loc.append(ag * h_loc[-1] + b[g * sl:(g + 1) * sl, :])
                a_run.append(ag * a_run[-1])
            state_in = pltpu.roll(carries[c], 1, 0)
            seg_b = jnp.where(sub8 == 0, a_run[-1] * state_in + h_loc[-1], h_loc[-1])
            seg_end = _linear_scan(a_run[-1], seg_b)
            start = from_previous_segment(carries[c], seg_end)
            carries[c] = seg_end
            v["hs", c] = jnp.concatenate([h_loc[g] + a_run[g] * start for g in range(n)], axis=0)

        def gate(c):
            g = _silu(v["z", _B_GATE[0] // MXU_N + c])
            if c < TOK_WIDTH // ct:
                v["y", c] = (v["hs", c] * g).astype(BF16)
            else:
                mem_out = _memory_attention(v["z", _B_QM[0] // MXU_N], kbd_ref.at[0, 0], vbd_ref.at[0, 0])
                v["y", c] = (mem_out * g).astype(BF16)

        def out_project(c):
            if c == 0:
                v["yb"] = jnp.concatenate([v["y", k] for k in range(MIX_WIDTH // ct)], axis=1)
            v["o", c] = jnp.dot(v["yb"], w_out_ref[:, c * MXU_N:(c + 1) * MXU_N], preferred_element_type=F32)

        def finish():
            o = jnp.concatenate([v["o", c] for c in range(D_MODEL // MXU_N)], axis=1)
            out = _layer_norm(ALPHA * v["hin"] + o, lng_ref[...], lnb_ref[...])
            for c in range(lane_tiles):
                perm_out_ref[r, c] = out[:, c * LANES:(c + 1) * LANES]
            for j in range(n):
                o_ref[0, row0 + j * sl:row0 + (j + 1) * sl, :] = jnp.concatenate(
                    [perm_out_ref[r, c, segment_rows(j), :] for c in range(lane_tiles)], axis=1)

        n_tok, n_mix = TOK_WIDTH // ct, MIX_WIDTH // ct
        phase_a = [load] + [functools.partial(project, c) for c in range(_B_WIDTH // MXU_N)]
        phase_b = []
        for c in range(n_mix):
            if c < n_tok:
                phase_b.append(functools.partial(recurrence, c))
            phase_b.append(functools.partial(gate, c))
        phase_cd = [functools.partial(out_project, c) for c in range(D_MODEL // MXU_N)] + [finish]
        return phase_a, phase_b, phase_cd

    phases = [sub_tile_phases(r) for r in range(n_sub)]
    _emit_round_robin(phases[0][0])
    for r in range(n_sub + 1):
        vector_heavy = phases[r][1] if r < n_sub else []
        matmul_heavy = (phases[r + 1][0] if r + 1 < n_sub else []) + (phases[r - 1][2] if r >= 1 else [])
        _emit_round_robin(matmul_heavy, vector_heavy)

    for c in range(TOK_WIDTH // ct):
        tail_ref[:, c * ct:(c + 1) * ct] = tails[c]
        carry_ref[:, c * ct:(c + 1) * ct] = carries[c]


def _lru_layer(h, w_in, conv_w, conv_b, w_r, b_r, w_i, b_i, lam, kbd, vbd, w_out, ln_g, ln_b):
    b, s, _ = h.shape
    tm = TM_LRU
    n_gate_tiles = TOK_WIDTH // GATE_TILE
    n_sub = TM_LRU // SUB_LRU
    return pl.pallas_call(
        _lru_layer_kernel,
        grid=(b, s // tm),
        in_specs=[
            pl.BlockSpec((1, tm, D_MODEL), lambda i, j: (i, j, 0)),
            _const_spec((D_MODEL, _B_WIDTH)),
            _const_spec((CONV_W, TOK_WIDTH)),
            _const_spec((1, TOK_WIDTH)),
            _const_spec((n_gate_tiles, GATE_TILE, GATE_TILE)),
            _const_spec((1, TOK_WIDTH)),
            _const_spec((n_gate_tiles, GATE_TILE, GATE_TILE)),
            _const_spec((1, TOK_WIDTH)),
            _const_spec((1, TOK_WIDTH)),
            pl.BlockSpec((1, 1, MEM_WIDTH, N_MEM_HEADS * MEM_LEN), lambda i, j: (1, i, 0, 0)),
            pl.BlockSpec((1, 1, N_MEM_HEADS * MEM_LEN, MEM_WIDTH), lambda i, j: (1, i, 0, 0)),
            _const_spec((MIX_WIDTH, D_MODEL)),
            _const_spec((1, D_MODEL)),
            _const_spec((1, D_MODEL)),
        ],
        out_specs=pl.BlockSpec((1, tm, D_MODEL), lambda i, j: (i, j, 0)),
        out_shape=jax.ShapeDtypeStruct((b, s, D_MODEL), F32),
        scratch_shapes=[
            pltpu.VMEM((n_sub, SUB_LRU + (CONV_W - 1) * SUBLANES, TOK_WIDTH), F32),
            pltpu.VMEM(((CONV_W - 1) * SUBLANES, TOK_WIDTH), F32),
            pltpu.VMEM((SUBLANES, TOK_WIDTH), F32),
            pltpu.VMEM((n_sub, D_MODEL // LANES, SUB_LRU, LANES), F32),
            pltpu.VMEM((n_sub, D_MODEL // LANES, SUB_LRU, LANES), F32),
        ],
        compiler_params=pltpu.CompilerParams(
            dimension_semantics=("arbitrary", "arbitrary"), vmem_limit_bytes=VMEM_LIMIT),
        name="lru_layer",
    )(h, w_in, conv_w, conv_b, w_r, b_r, w_i, b_i, lam, kbd, vbd, w_out, ln_g, ln_b)


def _block_diag_tiles(w):
    per_tile = GATE_TILE // HEAD_DIM
    w = w.reshape(TOK_WIDTH // GATE_TILE, per_tile, HEAD_DIM, HEAD_DIM)
    eye = jnp.eye(per_tile, dtype=w.dtype)
    t = w[:, :, :, None, :] * eye[None, :, None, :, None]
    return t.reshape(TOK_WIDTH // GATE_TILE, GATE_TILE, GATE_TILE)


def _rope_frequencies(tm):
    inv_freq = ROPE_THETA ** (-np.arange(HALF_ROPE, dtype=np.float32) / np.float32(HALF_ROPE))
    return np.ascontiguousarray(np.broadcast_to(inv_freq[:, None], (HALF_ROPE, tm)))


def _head_lanes(w, nope, t1, t2):
    zero = w.shape[-1]
    idx = np.full((LANES,), zero, np.int32)
    head = LANES // 2 - HALF_ROPE
    if t1 is not None:
        idx[:HALF_ROPE] = t1 + np.arange(HALF_ROPE)
        idx[LANES // 2:LANES // 2 + HALF_ROPE] = t2 + np.arange(HALF_ROPE)
    if nope is not None:
        idx[HALF_ROPE:LANES // 2] = nope + np.arange(head)
        idx[LANES // 2 + HALF_ROPE:LANES // 2 + HALF_ROPE + QK_NOPE - head] = nope + head + np.arange(QK_NOPE - head)
    w_ext = jnp.concatenate([w, jnp.zeros(w.shape[:-1] + (1,), w.dtype)], axis=-1)
    return jnp.take(w_ext, jnp.asarray(idx), axis=-1)


def kernel(x, mem, positions, mla_w_in, mla_q_norm, mla_w_uq, mla_kv_norm, mla_w_ukv, lru_w_in, lru_conv_w,
           lru_conv_b, lru_w_rgate, lru_b_rgate, lru_w_igate, lru_b_igate, lru_lambda, w_mem_kv, w_out, ln_g, ln_b):
    b, s, _ = x.shape
    assert s % TQ_ATTN == 0 and s % TM_LRU == 0 and s % TM_OUT == 0 and s % TM_PROJ == 0 and TK_ATTN % TM_PROJ == 0

    kbd, vbd = _mem_kv(mem, w_mem_kv)

    w_in = mla_w_in[0]
    o_q, o_kv, o_kr, o_gate = Q_LORA, Q_LORA + KV_LORA, Q_LORA + KV_LORA + QK_ROPE, Q_LORA + KV_LORA + QK_ROPE + MIX_WIDTH
    w_kr = _head_lanes(w_in[:, o_kv:o_kr], None, 0, HALF_ROPE)
    w_in_a = jnp.concatenate(
        [w_in[:, :o_q], w_in[:, o_q:o_kv], w_kr, w_in[:, o_kr:o_gate], w_in[:, o_gate:]], axis=1).astype(BF16)
    w_uq = _head_lanes(mla_w_uq[0].reshape(Q_LORA, N_TOK_HEADS, QK_DIM), 0, QK_NOPE, QK_NOPE + HALF_ROPE)
    w_uq = w_uq.reshape(Q_LORA, N_TOK_HEADS * LANES).astype(BF16)
    w_ukv = mla_w_ukv[0].reshape(KV_LORA, N_TOK_HEADS, QK_NOPE + HEAD_DIM)
    w_k = _head_lanes(w_ukv[:, :, :QK_NOPE], 0, None, None)
    w_k = w_k.reshape(KV_LORA, N_TOK_HEADS * LANES).astype(BF16)
    w_vT = w_ukv[:, :, QK_NOPE:].reshape(KV_LORA, TOK_WIDTH).T.astype(BF16)
    pos = positions.astype(F32)[:, None, :]
    w_out_b = w_out.astype(BF16)

    q, k, vT, g, mg = _mla_proj(x, pos, w_in_a, mla_q_norm[0][None], mla_kv_norm[0][None], w_uq, w_k, w_vT,
                                jnp.asarray(_rope_frequencies(TM_PROJ)), kbd, vbd)
    tokT = _attention(q, k, vT)
    h1 = _out_ln(tokT, g, mg, x, w_out_b[0], ln_g[0][None], ln_b[0][None])

    return _lru_layer(h1, lru_w_in[0].astype(BF16), lru_conv_w[0], lru_conv_b[0][None],
                      _block_diag_tiles(lru_w_rgate[0]).astype(BF16), lru_b_rgate[0][None],
                      _block_diag_tiles(lru_w_igate[0]).astype(BF16), lru_b_igate[0][None],
                      lru_lambda[0][None], kbd, vbd, w_out_b[1], ln_g[1][None], ln_b[1][None])
```
